```python
import math
import jax, jax.numpy as jnp
from jax import lax
import numpy as np

D_MODEL = 1024
BATCH = 8
SEQ = 2048
DEPTH = 2
DEC_BATCH = 128
DEC_SEQ = 1
PAST_LEN = 2048
PAGE_SIZE = 128

N_MIXERS = 4
GROUP_WIDTH = D_MODEL // N_MIXERS
HEAD_DIM = 64
SSD_HEADS = GROUP_WIDTH // HEAD_DIM
SSD_GROUPS = 2
SSD_STATE = 64
SSD_CHUNK = 128
SSD_XBC = GROUP_WIDTH + 2 * SSD_GROUPS * SSD_STATE
SSD_COLS = GROUP_WIDTH + SSD_XBC + SSD_HEADS
FOX_HEADS = GROUP_WIDTH // HEAD_DIM
FOX_QBLOCK = 128
FOX_COLS = 3 * GROUP_WIDTH + FOX_HEADS
GDN_HEADS = 4
GDN_DK = GROUP_WIDTH // GDN_HEADS
GDN_DV = GROUP_WIDTH // GDN_HEADS
GDN_CHUNK = 128
GDN_QKV = 2 * GDN_HEADS * GDN_DK + GDN_HEADS * GDN_DV
GDN_COLS = GDN_QKV + 2 * GDN_HEADS + GDN_HEADS * GDN_DV
MOBA_HEADS = GROUP_WIDTH // HEAD_DIM
MOBA_BLOCK = 256
MOBA_TOPK = 3
MOBA_QBLOCK = 32
MOBA_COLS = 3 * GROUP_WIDTH
IN_COLS = SSD_COLS + FOX_COLS + GDN_COLS + MOBA_COLS
CONV_W = 4
D_FF = 2816
N_EXPERTS = 8
TOP_K = 2
D_FF_EXPERT = 1792
N_DENSE = (DEPTH + 1) // 2
N_MOE = DEPTH // 2
EPS = 1e-6

kernel_name = 'hymba_ssd_fox_gdn_moba_adaln_step'


def rmsnorm(x, g):
    xf = x.astype(jnp.float32)
    y = xf * lax.rsqrt(jnp.mean(xf * xf, axis=-1, keepdims=True) + EPS)
    return (y * g.astype(jnp.float32)).astype(x.dtype)


def l2norm(x):
    xf = x.astype(jnp.float32)
    return xf * lax.rsqrt(jnp.sum(xf * xf, axis=-1, keepdims=True) + EPS)


def causal_dwconv(u, buf, w, b=None):
    full = jnp.concatenate([buf.astype(u.dtype), u], axis=1)
    out = lax.conv_general_dilated(full, w.astype(u.dtype)[:, None, :], (1,), 'VALID',
                                   dimension_numbers=('NWC', 'WIO', 'NWC'),
                                   feature_group_count=u.shape[-1])
    if b is not None:
        out = out + b.astype(u.dtype)
    return out, full[:, full.shape[1] - (CONV_W - 1):]


def ssd_chunked(x, dt, a_neg, bm, cm, h0):
    bsz, L, H, P = x.shape
    c = SSD_CHUNK if L % SSD_CHUNK == 0 else L
    nc = L // c
    rep = SSD_HEADS // SSD_GROUPS
    bh = jnp.repeat(bm, rep, axis=2).reshape(bsz, nc, c, H, SSD_STATE)
    ch = jnp.repeat(cm, rep, axis=2).reshape(bsz, nc, c, H, SSD_STATE)
    xc = x.reshape(bsz, nc, c, H, P)
    dtc = dt.reshape(bsz, nc, c, H)
    gam = jnp.cumsum(dtc * a_neg, axis=2)
    idx = jnp.arange(c)
    causal = (idx[:, None] >= idx[None, :])[None, None, :, :, None]
    decay = jnp.exp(jnp.where(causal, gam[:, :, :, None, :] - gam[:, :, None, :, :], -jnp.inf))
    scores = jnp.einsum('bkihn,bkjhn->bkijh', ch, bh) * decay
    y_intra = jnp.einsum('bkijh,bkjh,bkjhp->bkihp', scores, dtc, xc)
    w_state = jnp.exp(gam[:, :, -1:, :] - gam) * dtc
    chunk_states = jnp.einsum('bkjh,bkjhp,bkjhn->bkhpn', w_state, xc, bh)
    chunk_decay = jnp.exp(gam[:, :, -1, :])

    def step(h, inp):
        dec, st = inp
        return dec[:, :, None, None] * h + st, h

    h_last, h_in = lax.scan(step, h0, (jnp.moveaxis(chunk_decay, 1, 0), jnp.moveaxis(chunk_states, 1, 0)))
    h_in = jnp.moveaxis(h_in, 0, 1)
    y_inter = jnp.einsum('bkihn,bkhpn->bkihp', ch, h_in) * jnp.exp(gam)[..., None]
    return (y_intra + y_inter).reshape(bsz, L, H, P), h_last


def gated_delta_chunked(q, k, v, beta, g, s0):
    bsz, L, H, DK = q.shape
    DV = v.shape[-1]
    c = GDN_CHUNK if L % GDN_CHUNK == 0 else L
    nc = L // c

    def chunks(t):
        t = t.reshape((bsz, nc, c) + t.shape[2:])
        return jnp.moveaxis(jnp.moveaxis(t, 1, 0), 2, 3)

    qc, kc, vc, bc, gc = chunks(q), chunks(k), chunks(v), chunks(beta), chunks(g)
    gam = jnp.cumsum(gc, axis=-1)
    diff = gam[..., :, None] - gam[..., None, :]
    idx = jnp.arange(c)
    dec_strict = jnp.exp(jnp.where(idx[:, None] > idx[None, :], diff, -jnp.inf))
    dec_incl = jnp.exp(jnp.where(idx[:, None] >= idx[None, :], diff, -jnp.inf))
    kk = jnp.einsum('nbhid,nbhjd->nbhij', kc, kc)
    amat = jnp.eye(c, dtype=kc.dtype) + bc[..., :, None] * kk * dec_strict
    rhs = jnp.concatenate([bc[..., None] * vc, (bc * jnp.exp(gam))[..., None] * kc], axis=-1)
    sol = lax.linalg.triangular_solve(amat, rhs, left_side=True, lower=True, unit_diagonal=True)
    u, wk = sol[..., :DV], sol[..., DV:]
    qk = jnp.einsum('nbhid,nbhjd->nbhij', qc, kc) * dec_incl
    g_last = gam[..., -1]
    k_tail = kc * jnp.exp(g_last[..., None] - gam)[..., None]
    q_head = qc * jnp.exp(gam)[..., None]

    def step(s, inp):
        u_, wk_, qk_, qh_, kt_, gl_ = inp
        w = u_ - jnp.einsum('bhcd,bhde->bhce', wk_, s)
        o = jnp.einsum('bhcd,bhde->bhce', qh_, s) + jnp.einsum('bhij,bhje->bhie', qk_, w)
        s = jnp.exp(gl_)[..., None, None] * s + jnp.einsum('bhcd,bhce->bhde', kt_, w)
        return s, o

    s_last, o = lax.scan(step, s0, (u, wk, qk, q_head, k_tail, g_last))
    o = jnp.moveaxis(jnp.moveaxis(o, 3, 2), 0, 1).reshape(bsz, L, H, DV)
    return o, s_last


def fox_attend(q, k, v, logf):
    bsz, Lq, H, D = q.shape
    T = k.shape[1]
    q_start = T - Lq
    cum = jnp.cumsum(logf.astype(jnp.float32), axis=1)
    ck = jnp.moveaxis(cum, 1, 2)
    cq = cum[:, q_start:]
    qb = FOX_QBLOCK if Lq % FOX_QBLOCK == 0 else Lq
    nqb = Lq // qb
    scale = HEAD_DIM ** -0.5
    kpos = jnp.arange(T)

    def block(args):
        qi, ci, pos = args
        s = jnp.einsum('bqhd,bkhd->bhqk', qi, k).astype(jnp.float32) * scale
        s = s + (jnp.moveaxis(ci, 1, 2)[..., :, None] - ck[:, :, None, :])
        s = jnp.where(kpos[None, :] <= pos[:, None], s, -jnp.inf)
        p = jax.nn.softmax(s, axis=-1)
        return jnp.einsum('bhqk,bkhd->bqhd', p.astype(v.dtype), v)

    qs = jnp.moveaxis(q.reshape(bsz, nqb, qb, H, D), 1, 0)
    cs = jnp.moveaxis(cq.reshape(bsz, nqb, qb, H), 1, 0)
    ps = (q_start + jnp.arange(Lq)).reshape(nqb, qb)
    out = lax.map(block, (qs, cs, ps))
    return jnp.moveaxis(out, 0, 1).reshape(bsz, Lq, H, D)


def moba_attend(q, k, v):
    bsz, Lq, H, D = q.shape
    T = k.shape[1]
    q_start = T - Lq
    nb = -(-T // MOBA_BLOCK)
    pad = nb * MOBA_BLOCK - T
    kp = jnp.pad(k, ((0, 0), (0, pad), (0, 0), (0, 0)))
    vp = jnp.pad(v, ((0, 0), (0, pad), (0, 0), (0, 0)))
    kb = jnp.moveaxis(kp.reshape(bsz, nb, MOBA_BLOCK, H, D), 3, 1)
    vb = jnp.moveaxis(vp.reshape(bsz, nb, MOBA_BLOCK, H, D), 3, 1)
    kmean = jnp.mean(kb.astype(jnp.float32), axis=3)
    n_sel = min(MOBA_TOPK, nb)
    slopes = jnp.exp2(-8.0 * jnp.arange(1, H + 1, dtype=jnp.float32) / H)
    blk_ids = jnp.arange(nb)
    offs = jnp.arange(MOBA_BLOCK)
    scale = HEAD_DIM ** -0.5
    gather = jax.vmap(jax.vmap(lambda t, ix: t[ix]))
    qb = MOBA_QBLOCK if Lq % MOBA_QBLOCK == 0 else Lq
    nqb = Lq // qb

    def block(args):
        qi, pos = args
        qh = jnp.moveaxis(qi, 1, 2)
        own = pos // MOBA_BLOCK
        gate = jnp.einsum('bhqd,bhnd->bhqn', qh.astype(jnp.float32), kmean)
        gate = jnp.where(blk_ids[None, :] < own[:, None], gate, -jnp.inf)
        top_v, top_i = lax.top_k(gate, n_sel)
        valid = jnp.isfinite(top_v)
        own_b = jnp.broadcast_to(own[None, None, :, None], top_i.shape[:3] + (1,)).astype(top_i.dtype)
        sel = jnp.concatenate([top_i, own_b], axis=-1)
        kg = gather(kb, sel)
        vg = gather(vb, sel)
        kpos = sel[..., None] * MOBA_BLOCK + offs
        s = jnp.einsum('bhqd,bhqnsd->bhqns', qh, kg).astype(jnp.float32) * scale
        dist = (pos[None, None, :, None, None] - kpos).astype(jnp.float32)
        s = s - slopes[None, :, None, None, None] * dist
        keep = jnp.concatenate([
            jnp.broadcast_to(valid[..., None], valid.shape + (MOBA_BLOCK,)),
            kpos[..., -1:, :] <= pos[None, None, :, None, None]], axis=3)
        s = jnp.where(keep, s, -jnp.inf)
        p = jax.nn.softmax(s.reshape(s.shape[:3] + (-1,)), axis=-1).reshape(s.shape)
        return jnp.einsum('bhqns,bhqnsd->bqhd', p.astype(vg.dtype), vg)

    qs = jnp.moveaxis(q.reshape(bsz, nqb, qb, H, D), 1, 0)
    ps = (q_start + jnp.arange(Lq)).reshape(nqb, qb)
    out = lax.map(block, (qs, ps))
    return jnp.moveaxis(out, 0, 1).reshape(bsz, Lq, H, D)


def token_mixers(h, lp, past):
    f32 = jnp.float32
    bsz, L, _ = h.shape
    proj = h @ lp['w_in']
    pa, pb, pc, pd = jnp.split(proj, [SSD_COLS, SSD_COLS + FOX_COLS, SSD_COLS + FOX_COLS + GDN_COLS], axis=-1)

    z_a, xbc, dt_raw = jnp.split(pa, [GROUP_WIDTH, GROUP_WIDTH + SSD_XBC], axis=-1)
    xbc, ssm_buf = causal_dwconv(xbc, past['ssm_buf'], lp['ssd_conv_w'], lp['ssd_conv_b'])
    xbc = jax.nn.silu(xbc).astype(f32)
    xa, b_a, c_a = jnp.split(xbc, [GROUP_WIDTH, GROUP_WIDTH + SSD_GROUPS * SSD_STATE], axis=-1)
    xa = xa.reshape(bsz, L, SSD_HEADS, HEAD_DIM)
    b_a = b_a.reshape(bsz, L, SSD_GROUPS, SSD_STATE)
    c_a = c_a.reshape(bsz, L, SSD_GROUPS, SSD_STATE)
    dt = jax.nn.softplus(dt_raw.astype(f32) + lp['ssd_dt_bias'].astype(f32))
    a_neg = -jnp.exp(lp['ssd_a_log'].astype(f32))
    y_a, ssm_h = ssd_chunked(xa, dt, a_neg, b_a, c_a, past['ssm_h'].astype(f32))
    y_a = (y_a + lp['ssd_d'].astype(f32)[:, None] * xa).reshape(bsz, L, GROUP_WIDTH)
    y_a = rmsnorm(y_a * jax.nn.silu(z_a.astype(f32)), lp['ssd_norm'])

    fq, fk, fv, ff = jnp.split(pb, [GROUP_WIDTH, 2 * GROUP_WIDTH, 3 * GROUP_WIDTH], axis=-1)
    fq = rmsnorm(fq.reshape(bsz, L, FOX_HEADS, HEAD_DIM), lp['fox_q_norm'])
    fk = rmsnorm(fk.reshape(bsz, L, FOX_HEADS, HEAD_DIM), lp['fox_k_norm'])
    fv = fv.reshape(bsz, L, FOX_HEADS, HEAD_DIM)
    logf = jax.nn.log_sigmoid(ff.astype(f32) + lp['fox_b_f'].astype(f32))
    if past['fox'] is None:
        fk_all, fv_all, lf_all = fk, fv, logf
    else:
        pk, pv, plf = past['fox']
        fk_all = jnp.concatenate([pk.astype(fk.dtype), fk], axis=1)
        fv_all = jnp.concatenate([pv.astype(fv.dtype), fv], axis=1)
        lf_all = jnp.concatenate([plf.astype(f32), logf], axis=1)
    y_b = fox_attend(fq, fk_all, fv_all, lf_all).reshape(bsz, L, GROUP_WIDTH)

    qkv, beta_raw, a_raw, z_c = jnp.split(pc, [GDN_QKV, GDN_QKV + GDN_HEADS, GDN_QKV + 2 * GDN_HEADS], axis=-1)
    qkv, gdn_buf = causal_dwconv(qkv, past['gdn_buf'], lp['gdn_conv_w'])
    qkv = jax.nn.silu(qkv)
    gq, gk, gv = jnp.split(qkv, [GDN_HEADS * GDN_DK, 2 * GDN_HEADS * GDN_DK], axis=-1)
    gq = l2norm(gq.reshape(bsz, L, GDN_HEADS, GDN_DK)) * (GDN_DK ** -0.5)
    gk = l2norm(gk.reshape(bsz, L, GDN_HEADS, GDN_DK))
    gv = gv.reshape(bsz, L, GDN_HEADS, GDN_DV).astype(f32)
    beta = jax.nn.sigmoid(beta_raw.astype(f32))
    g_log = -jnp.exp(lp['gdn_a_log'].astype(f32)) * jax.nn.softplus(a_raw.astype(f32) + lp['gdn_dt_bias'].astype(f32))
    o_c, gdn_s = gated_delta_chunked(gq, gk, gv, beta, g_log, past['gdn_s'].astype(f32))
    o_c = rmsnorm(o_c, lp['gdn_norm']) * jax.nn.silu(z_c.reshape(bsz, L, GDN_HEADS, GDN_DV).astype(f32))
    y_c = o_c.reshape(bsz, L, GROUP_WIDTH)

    mq, mk, mv = jnp.split(pd, 3, axis=-1)
    mq = rmsnorm(mq.reshape(bsz, L, MOBA_HEADS, HEAD_DIM), lp['moba_q_norm'])
    mk = rmsnorm(mk.reshape(bsz, L, MOBA_HEADS, HEAD_DIM), lp['moba_k_norm'])
    mv = mv.reshape(bsz, L, MOBA_HEADS, HEAD_DIM)
    if past['moba'] is None:
        mk_all, mv_all = mk, mv
    else:
        pk, pv = past['moba']
        mk_all = jnp.concatenate([pk.astype(mk.dtype), mk], axis=1)
        mv_all = jnp.concatenate([pv.astype(mv.dtype), mv], axis=1)
    y_d = moba_attend(mq, mk_all, mv_all).reshape(bsz, L, GROUP_WIDTH)

    y = jnp.concatenate([y_a.astype(h.dtype), y_b.astype(h.dtype), y_c.astype(h.dtype), y_d.astype(h.dtype)], axis=-1)
    y = y @ lp['w_out']
    return y, (fk, fv, logf, mk, mv, ssm_h, ssm_buf, gdn_s, gdn_buf)


def swiglu(h, w_gate, w_up, w_down):
    return (jax.nn.silu(h @ w_gate) * (h @ w_up)) @ w_down


def moe_swiglu(h, router, w_gate, w_up, w_down):
    logits = (h @ router).astype(jnp.float32)
    top_v, top_i = lax.top_k(logits, TOP_K)
    wts = jax.nn.softmax(top_v, axis=-1)
    gates = jnp.sum(jax.nn.one_hot(top_i, N_EXPERTS, dtype=jnp.float32) * wts[..., None], axis=-2)
    out = jnp.zeros_like(h)
    for e in range(N_EXPERTS):
        out = out + (gates[..., e:e + 1] * swiglu(h, w_gate[e], w_up[e], w_down[e]).astype(jnp.float32)).astype(h.dtype)
    return out


def trunk_layer(x, mod, g_mix, g_ffn, lp, past, ffn):
    sh1, sc1, gt1, sh2, sc2, gt2 = [m[:, None, :] for m in jnp.split(mod.astype(x.dtype), 6, axis=-1)]
    h = rmsnorm(x, g_mix) * (1 + sc1) + sh1
    y, state = token_mixers(h, lp, past)
    x = x + gt1 * y
    h = rmsnorm(x, g_ffn) * (1 + sc2) + sh2
    x = x + gt2 * ffn(h)
    return x, state


def setup_inputs(seed: int = 0) -> dict:
    key = jax.random.key(seed)
    ks = iter(jax.random.split(key, 64))
    f32 = jnp.float32
    n_pages = PAST_LEN // PAGE_SIZE
    n_used = DEC_BATCH * n_pages
    n_pool = n_used + max(1, n_used // 4)

    def nrm(shape, scale):
        return scale * jax.random.normal(next(ks), shape, f32)

    def gain(shape):
        return 1.0 + nrm(shape, 0.02)

    def dt_bias(shape):
        dt = jnp.exp(jax.random.uniform(next(ks), shape, f32, math.log(1e-3), math.log(1e-1)))
        return dt + jnp.log(-jnp.expm1(-dt))

    def a_log(shape):
        return jnp.log(jax.random.uniform(next(ks), shape, f32, 1.0, 16.0))

    d = D_MODEL
    inputs = {}
    inputs['x_prompt'] = nrm((BATCH, SEQ, d), 1.0)
    inputs['x_sample'] = nrm((DEC_BATCH, DEC_SEQ, d), 1.0)
    inputs['cache_fox_k'] = nrm((DEPTH, n_pool, PAGE_SIZE, FOX_HEADS, HEAD_DIM), 1.0)
    inputs['cache_fox_v'] = nrm((DEPTH, n_pool, PAGE_SIZE, FOX_HEADS, HEAD_DIM), 1.0)
    inputs['cache_fox_logf'] = jax.nn.log_sigmoid(4.0 + nrm((DEPTH, n_pool, PAGE_SIZE, FOX_HEADS), 1.0))
    inputs['cache_moba_k'] = nrm((DEPTH, n_pool, PAGE_SIZE, MOBA_HEADS, HEAD_DIM), 1.0)
    inputs['cache_moba_v'] = nrm((DEPTH, n_pool, PAGE_SIZE, MOBA_HEADS, HEAD_DIM), 1.0)
    inputs['state_ssm'] = nrm((DEPTH, DEC_BATCH, SSD_HEADS, HEAD_DIM, SSD_STATE), 0.1)
    inputs['state_ssm_conv'] = nrm((DEPTH, DEC_BATCH, CONV_W - 1, SSD_XBC), 1.0)
    inputs['state_gdn'] = nrm((DEPTH, DEC_BATCH, GDN_HEADS, GDN_DK, GDN_DV), 0.1)
    inputs['state_gdn_conv'] = nrm((DEPTH, DEC_BATCH, CONV_W - 1, GDN_QKV), 1.0)
    inputs['page_table'] = jax.random.permutation(next(ks), n_pool)[:n_used].reshape(DEC_BATCH, n_pages).astype(jnp.int32)
    inputs['c_prompt'] = nrm((BATCH, d), 1.0)
    inputs['c_sample'] = nrm((DEC_BATCH, d), 1.0)
    inputs['w_ada'] = nrm((DEPTH, d, 6 * d), 0.5 * d ** -0.5)
    inputs['b_ada'] = nrm((DEPTH, 6 * d), 0.02)
    inputs['norm_mix'] = gain((DEPTH, d))
    inputs['norm_ffn'] = gain((DEPTH, d))
    inputs['w_in'] = nrm((DEPTH, d, IN_COLS), d ** -0.5)
    inputs['w_out'] = nrm((DEPTH, d, d), d ** -0.5)
    inputs['ssd_conv_w'] = nrm((DEPTH, CONV_W, SSD_XBC), CONV_W ** -0.5)
    inputs['ssd_conv_b'] = nrm((DEPTH, SSD_XBC), 0.02)
    inputs['ssd_dt_bias'] = dt_bias((DEPTH, SSD_HEADS))
    inputs['ssd_a_log'] = a_log((DEPTH, SSD_HEADS))
    inputs['ssd_d'] = 1.0 + nrm((DEPTH, SSD_HEADS), 0.1)
    inputs['ssd_norm'] = gain((DEPTH, GROUP_WIDTH))
    inputs['fox_b_f'] = 4.0 + nrm((DEPTH, FOX_HEADS), 0.5)
    inputs['fox_q_norm'] = gain((DEPTH, HEAD_DIM))
    inputs['fox_k_norm'] = gain((DEPTH, HEAD_DIM))
    inputs['gdn_conv_w'] = nrm((DEPTH, CONV_W, GDN_QKV), CONV_W ** -0.5)
    inputs['gdn_a_log'] = a_log((DEPTH, GDN_HEADS))
    inputs['gdn_dt_bias'] = dt_bias((DEPTH, GDN_HEADS))
    inputs['gdn_norm'] = gain((DEPTH, GDN_DV))
    inputs['moba_q_norm'] = gain((DEPTH, HEAD_DIM))
    inputs['moba_k_norm'] = gain((DEPTH, HEAD_DIM))
    inputs['ffn_w_gate'] = nrm((N_DENSE, d, D_FF), d ** -0.5)
    inputs['ffn_w_up'] = nrm((N_DENSE, d, D_FF), d ** -0.5)
    inputs['ffn_w_down'] = nrm((N_DENSE, D_FF, d), D_FF ** -0.5)
    inputs['moe_router'] = nrm((N_MOE, d, N_EXPERTS), d ** -0.5)
    inputs['moe_w_gate'] = nrm((N_MOE, N_EXPERTS, d, D_FF_EXPERT), d ** -0.5)
    inputs['moe_w_up'] = nrm((N_MOE, N_EXPERTS, d, D_FF_EXPERT), d ** -0.5)
    inputs['moe_w_down'] = nrm((N_MOE, N_EXPERTS, D_FF_EXPERT, d), D_FF_EXPERT ** -0.5)
    return inputs


def reference(x_prompt, x_sample, cache_fox_k, cache_fox_v, cache_fox_logf, cache_moba_k, cache_moba_v,
              state_ssm, state_ssm_conv, state_gdn, state_gdn_conv, page_table, c_prompt, c_sample,
              w_ada, b_ada, norm_mix, norm_ffn, w_in, w_out,
              ssd_conv_w, ssd_conv_b, ssd_dt_bias, ssd_a_log, ssd_d, ssd_norm,
              fox_b_f, fox_q_norm, fox_k_norm,
              gdn_conv_w, gdn_a_log, gdn_dt_bias, gdn_norm,
              moba_q_norm, moba_k_norm,
              ffn_w_gate, ffn_w_up, ffn_w_down,
              moe_router, moe_w_gate, moe_w_up, moe_w_down):
    bp = x_prompt.shape[0]
    past_len = page_table.shape[1] * PAGE_SIZE

    def paged_rows(cache):
        rows = cache[page_table]
        return rows.reshape((rows.shape[0], past_len) + rows.shape[3:])

    cp = jax.nn.silu(c_prompt)
    cs = jax.nn.silu(c_sample)
    xp, xs = x_prompt, x_sample
    st_prompt, st_sample = [], []
    for l in range(DEPTH):
        lp = dict(w_in=w_in[l], w_out=w_out[l],
                  ssd_conv_w=ssd_conv_w[l], ssd_conv_b=ssd_conv_b[l], ssd_dt_bias=ssd_dt_bias[l],
                  ssd_a_log=ssd_a_log[l], ssd_d=ssd_d[l], ssd_norm=ssd_norm[l],
                  fox_b_f=fox_b_f[l], fox_q_norm=fox_q_norm[l], fox_k_norm=fox_k_norm[l],
                  gdn_conv_w=gdn_conv_w[l], gdn_a_log=gdn_a_log[l], gdn_dt_bias=gdn_dt_bias[l],
                  gdn_norm=gdn_norm[l], moba_q_norm=moba_q_norm[l], moba_k_norm=moba_k_norm[l])
        i = l // 2
        if l % 2 == 0:
            ffn = lambda t, i=i: swiglu(t, ffn_w_gate[i], ffn_w_up[i], ffn_w_down[i])
        else:
            ffn = lambda t, i=i: moe_swiglu(t, moe_router[i], moe_w_gate[i], moe_w_up[i], moe_w_down[i])
        past_p = dict(fox=None, moba=None,
                      ssm_h=jnp.zeros((bp, SSD_HEADS, HEAD_DIM, SSD_STATE), jnp.float32),
                      ssm_buf=jnp.zeros((bp, CONV_W - 1, SSD_XBC), x_prompt.dtype),
                      gdn_s=jnp.zeros((bp, GDN_HEADS, GDN_DK, GDN_DV), jnp.float32),
                      gdn_buf=jnp.zeros((bp, CONV_W - 1, GDN_QKV), x_prompt.dtype))
        past_s = dict(fox=(paged_rows(cache_fox_k[l]), paged_rows(cache_fox_v[l]), paged_rows(cache_fox_logf[l])),
                      moba=(paged_rows(cache_moba_k[l]), paged_rows(cache_moba_v[l])),
                      ssm_h=state_ssm[l], ssm_buf=state_ssm_conv[l],
                      gdn_s=state_gdn[l], gdn_buf=state_gdn_conv[l])
        xp, sp = trunk_layer(xp, cp @ w_ada[l] + b_ada[l], norm_mix[l], norm_ffn[l], lp, past_p, ffn)
        xs, ss = trunk_layer(xs, cs @ w_ada[l] + b_ada[l], norm_mix[l], norm_ffn[l], lp, past_s, ffn)
        st_prompt.append(sp)
        st_sample.append(ss)

    def stack(states, j):
        return jnp.stack([s[j] for s in states], axis=0)

    fox_k_p, fox_v_p, fox_logf_p = stack(st_prompt, 0), stack(st_prompt, 1), stack(st_prompt, 2)
    moba_k_p, moba_v_p = stack(st_prompt, 3), stack(st_prompt, 4)
    ssm_p, ssm_conv_p = stack(st_prompt, 5), stack(st_prompt, 6)
    gdn_p, gdn_conv_p = stack(st_prompt, 7), stack(st_prompt, 8)
    fox_k_s, fox_v_s, fox_logf_s = stack(st_sample, 0), stack(st_sample, 1), stack(st_sample, 2)
    moba_k_s, moba_v_s = stack(st_sample, 3), stack(st_sample, 4)
    ssm_s, ssm_conv_s = stack(st_sample, 5), stack(st_sample, 6)
    gdn_s, gdn_conv_s = stack(st_sample, 7), stack(st_sample, 8)
    return (xp, xs,
            fox_k_p, fox_v_p, fox_logf_p, moba_k_p, moba_v_p, ssm_p, ssm_conv_p, gdn_p, gdn_conv_p,
            fox_k_s, fox_v_s, fox_logf_s, moba_k_s, moba_v_s, ssm_s, ssm_conv_s, gdn_s, gdn_conv_s)
```

```python
import functools

import jax
import jax.numpy as jnp
from jax import lax
from jax.experimental import pallas as pl
from jax.experimental.pallas import tpu as pltpu

F32 = jnp.float32
BF16 = jnp.bfloat16
EPS = 1e-6
NEG = -1e30

D_MODEL = 1024
GROUP = 256
HEADS = 4
HD = 64
PAGE = 128
MOBA_BLOCK = 256
MOBA_TOPK = 3
CONV_W = 4
SSD_CHUNK = 128
GDN_CHUNK = 64
LANES = 128

XBC_W, QKV_W = 512, 768
COL_XBC = 0
COL_ZA = 2
COL_FQ, COL_FK, COL_FV = 3, 4, 5
COL_QKV = 2
COL_ZC = 9
COL_MQ, COL_MK, COL_MV = 10, 11, 12
COL_SMALL = 26
PROJ_COLS = 3456
L_DT, L_FF, L_BETA, L_GA = 0, 4, 8, 12


def _dot(a, b):
    return jnp.dot(a, b, preferred_element_type=F32)


def _dg(a, b, ca, cb):
    return lax.dot_general(a, b, (((ca,), (cb,)), ((), ())), preferred_element_type=F32)


def _bf(a):
    return a.astype(BF16)


def _hi_lo(a):
    hi = _bf(a)
    return hi, _bf(a - hi.astype(F32))


def _split3(a):
    a1 = _bf(a)
    r = a - a1.astype(F32)
    a2 = _bf(r)
    return a1, a2, _bf(r - a2.astype(F32))


def _mm(a, b):
    return _dot(_bf(a), _bf(b))


def _nt(a, b):
    return _dg(_bf(a), _bf(b), 1, 1)


def _tn(a, b):
    return _dg(_bf(a), _bf(b), 0, 0)


def _x3(f, a, b):
    a1, a2 = _hi_lo(a)
    b1, b2 = _hi_lo(b)
    return f(a1, b1) + (f(a1, b2) + f(a2, b1))


def _mm3(a, b):
    return _x3(_dot, a, b)


def _nt3(a, b):
    return _x3(lambda x, y: _dg(x, y, 1, 1), a, b)


def _tn3(a, b):
    return _x3(lambda x, y: _dg(x, y, 0, 0), a, b)


def _mm01(m01, x):
    x1, x2, x3 = _split3(x)
    return _dot(m01, x1) + (_dot(m01, x2) + _dot(m01, x3))


def _mmx01(x, m01):
    x1, x2, x3 = _split3(x)
    return _dot(x1, m01) + (_dot(x2, m01) + _dot(x3, m01))


def _tr(x):
    n = x.shape[1]
    eye = _bf(lax.broadcasted_iota(jnp.int32, (n, n), 0) == lax.broadcasted_iota(jnp.int32, (n, n), 1))
    x1, x2, x3 = _split3(x)
    return _dg(eye, x1, 1, 1) + (_dg(eye, x2, 1, 1) + _dg(eye, x3, 1, 1))


def _iota(shape, dim):
    return lax.broadcasted_iota(jnp.int32, shape, dim)


def _silu(x):
    return x * jax.nn.sigmoid(x)


def _softplus(x):
    return jnp.maximum(x, 0.0) + jnp.log1p(jnp.exp(-jnp.abs(x)))


def _log_sigmoid(x):
    return -_softplus(-x)


def _head_mat():
    r = lax.shift_right_logical(_iota((GROUP, GROUP), 0), 6)
    c = lax.shift_right_logical(_iota((GROUP, GROUP), 1), 6)
    return _bf(r == c)


def _headsum(x, hm):
    x1, x2 = _hi_lo(x)
    return _dot(x1, hm) + _dot(x2, hm)


def _expand_heads(x):
    r = x.shape[0]
    grp = lax.shift_right_logical(_iota((r, GROUP), 1), 6)
    out = jnp.broadcast_to(x[:, 3:4], (r, GROUP))
    for h in (2, 1, 0):
        out = jnp.where(grp == h, jnp.broadcast_to(x[:, h:h + 1], (r, GROUP)), out)
    return out


def _normmod(x, g, sc, sh):
    y = x * lax.rsqrt(jnp.mean(x * x, axis=-1, keepdims=True) + EPS) * g
    return y * (1.0 + sc) + sh


def _cparams(sem, vmem_mb=48):
    return pltpu.CompilerParams(dimension_semantics=sem, vmem_limit_bytes=vmem_mb * 1024 * 1024)


def _mod_spec(per_row, tm, rows_per_batch, chunk):
    if per_row:
        return pl.BlockSpec((1, tm, D_MODEL), lambda i, *_: (0, i, chunk))
    return pl.BlockSpec((1, 1, D_MODEL), lambda i, *_: ((i * tm) // rows_per_batch, 0, chunk))


def _ada_kernel(c_ref, w_ref, b_ref, o_ref):
    o_ref[0] = _mm(_silu(c_ref[...]), w_ref[0]) + b_ref[0]


def _ada_mod(c_all, w_ada, b_ada):
    depth, d, n = w_ada.shape
    r = c_all.shape[0]
    tn = 1536
    return pl.pallas_call(
        _ada_kernel, grid=(depth, n // tn),
        in_specs=[pl.BlockSpec((r, d), lambda l, j: (0, 0)),
                  pl.BlockSpec((1, d, tn), lambda l, j: (l, 0, j)),
                  pl.BlockSpec((1, 1, tn), lambda l, j: (l, 0, j))],
        out_specs=pl.BlockSpec((1, r, tn), lambda l, j: (l, 0, j)),
        out_shape=jax.ShapeDtypeStruct((depth, r, n), F32),
        compiler_params=_cparams(("parallel", "parallel")), name="ada_mod",
    )(c_all, w_ada, b_ada.reshape(depth, 1, n))


def _inproj_kernel(x_ref, g_ref, sc_ref, sh_ref, w_ref, o_ref, h_scr):
    @pl.when(pl.program_id(1) == 0)
    def _():
        h_scr[...] = _bf(_normmod(x_ref[...], g_ref[...], sc_ref[0], sh_ref[0]))

    o_ref[...] = _dot(h_scr[...], w_ref[...])


def _in_proj(x2d, g, mod, w_bf, *, tm, per_row, rows_per_batch):
    t, d = x2d.shape
    n = w_bf.shape[1]
    tn = n // 3
    return pl.pallas_call(
        _inproj_kernel, grid=(t // tm, n // tn),
        in_specs=[pl.BlockSpec((tm, d), lambda i, j: (i, 0)),
                  pl.BlockSpec((1, d), lambda i, j: (0, 0)),
                  _mod_spec(per_row, tm, rows_per_batch, 1),
                  _mod_spec(per_row, tm, rows_per_batch, 0),
                  pl.BlockSpec((d, tn), lambda i, j: (0, j))],
        out_specs=pl.BlockSpec((tm, tn), lambda i, j: (i, j)),
        out_shape=jax.ShapeDtypeStruct((t, n), F32),
        scratch_shapes=[pltpu.VMEM((tm, d), BF16)],
        compiler_params=_cparams(("parallel", "arbitrary")), name="in_proj",
    )(x2d, g, mod, mod, w_bf)


def _prep_kernel(fq_ref, fk_ref, fv_ref, mq_ref, mk_ref, mv_ref, sm_ref,
                 gfq_ref, gfk_ref, gmq_ref, gmk_ref, bf_ref, *rest, seq):
    if seq:
        (fqn_ref, fkn_ref, fvo_ref, mqn_ref, mkn_ref, mvo_ref, lf_ref,
         cum_ref, cumt_ref, kmean_ref, carry_scr) = rest
    else:
        fqn_ref, fkn_ref, fvo_ref, mqn_ref, mkn_ref, mvo_ref, lf_ref = rest
    hm = _head_mat()

    def hnorm(x, g):
        return x * lax.rsqrt(_headsum(x * x, hm) * (1.0 / HD) + EPS) * g

    fqn_ref[0] = hnorm(fq_ref[0], gfq_ref[...])
    fkn_ref[0] = hnorm(fk_ref[0], gfk_ref[...])
    fvo_ref[0] = fv_ref[0]
    mqn_ref[0] = hnorm(mq_ref[0], gmq_ref[...])
    mkn = hnorm(mk_ref[0], gmk_ref[...])
    mkn_ref[0] = mkn
    mvo_ref[0] = mv_ref[0]
    lf = _log_sigmoid(sm_ref[0] + bf_ref[...])
    lf_ref[0] = lf[:, L_FF:L_FF + HEADS]
    if seq:
        @pl.when(pl.program_id(1) == 0)
        def _():
            carry_scr[...] = jnp.zeros_like(carry_scr)

        tm = lf.shape[0]
        tril = _bf(_iota((tm, tm), 0) >= _iota((tm, tm), 1))
        cum = _mm01(tril, lf) + carry_scr[...]
        carry_scr[...] = cum[tm - 1:tm, :]
        cum_ref[0] = cum
        cumt_ref[0] = _tr(cum)[0:8, :]
        kmean_ref[0, 0] = jnp.mean(mkn, axis=0, keepdims=True)


def _prep(proj3, gfq, gfk, gmq, gmk, bf_full, *, tm, seq):
    nb, ln, _ = proj3.shape

    def col(c):
        return pl.BlockSpec((1, tm, GROUP), lambda b, i: (b, i, c))

    row = pl.BlockSpec((1, GROUP), lambda b, i: (0, 0))
    out_blk = pl.BlockSpec((1, tm, GROUP), lambda b, i: (b, i, 0))
    big = jax.ShapeDtypeStruct((nb, ln, GROUP), F32)
    out_specs = [out_blk] * 6 + [pl.BlockSpec((1, tm, HEADS), lambda b, i: (b, i, 0))]
    out_shape = [big] * 6 + [jax.ShapeDtypeStruct((nb, ln, HEADS), F32)]
    scratch = []
    if seq:
        out_specs += [pl.BlockSpec((1, tm, LANES), lambda b, i: (b, i, 0)),
                      pl.BlockSpec((1, 8, tm), lambda b, i: (b, 0, i)),
                      pl.BlockSpec((1, 1, 1, GROUP), lambda b, i: (b, i, 0, 0))]
        out_shape += [jax.ShapeDtypeStruct((nb, ln, LANES), F32),
                      jax.ShapeDtypeStruct((nb, 8, ln), F32),
                      jax.ShapeDtypeStruct((nb, ln // tm, 1, GROUP), F32)]
        scratch = [pltpu.VMEM((1, LANES), F32)]
    return pl.pallas_call(
        functools.partial(_prep_kernel, seq=seq), grid=(nb, ln // tm),
        in_specs=[col(COL_FQ), col(COL_FK), col(COL_FV), col(COL_MQ), col(COL_MK), col(COL_MV),
                  pl.BlockSpec((1, tm, LANES), lambda b, i: (b, i, COL_SMALL)),
                  row, row, row, row, pl.BlockSpec((1, LANES), lambda b, i: (0, 0))],
        out_specs=out_specs, out_shape=out_shape, scratch_shapes=scratch,
        compiler_params=_cparams(("parallel", "arbitrary")), name="attn_prep",
    )(proj3, proj3, proj3, proj3, proj3, proj3, proj3, gfq, gfk, gmq, gmk, bf_full)


def _softmax_step(s, vh, h, m_scr, l_scr, acc_scr):
    sl = slice(h * HD, (h + 1) * HD)
    m_old = m_scr[h]
    m_new = jnp.maximum(m_old, jnp.max(s, axis=1, keepdims=True))
    p = jnp.exp(s - m_new)
    alpha = jnp.exp(m_old - m_new)
    l_scr[h] = alpha * l_scr[h] + jnp.sum(p, axis=1, keepdims=True)
    acc_scr[:, sl] = alpha * acc_scr[:, sl] + _mm(p, vh)
    m_scr[h] = m_new


def _softmax_init(m_scr, l_scr, acc_scr):
    m_scr[...] = jnp.full_like(m_scr, NEG)
    l_scr[...] = jnp.zeros_like(l_scr)
    acc_scr[...] = jnp.zeros_like(acc_scr)


def _softmax_finish(o_ref, l_scr, acc_scr):
    for h in range(HEADS):
        sl = slice(h * HD, (h + 1) * HD)
        o_ref[0, :, sl] = acc_scr[:, sl] / l_scr[h]


def _fox_kernel(q_ref, k_ref, v_ref, cq_ref, ck_ref, o_ref, m_scr, l_scr, acc_scr, *, t, scale):
    qi, ki = pl.program_id(1), pl.program_id(2)

    @pl.when(ki == 0)
    def _():
        _softmax_init(m_scr, l_scr, acc_scr)

    @pl.when(ki <= qi)
    def _():
        q, k, v, cq, ck = q_ref[0], k_ref[0], v_ref[0], cq_ref[0], ck_ref[0]
        causal = (_iota((t, t), 1) + ki * t) <= (_iota((t, t), 0) + qi * t)
        for h in range(HEADS):
            sl = slice(h * HD, (h + 1) * HD)
            s = _nt(q[:, sl], k[:, sl]) * scale + (cq[:, L_FF + h:L_FF + h + 1] - ck[L_FF + h:L_FF + h + 1, :])
            _softmax_step(jnp.where(causal, s, NEG), v[:, sl], h, m_scr, l_scr, acc_scr)

    @pl.when(ki == qi)
    def _():
        _softmax_finish(o_ref, l_scr, acc_scr)


def _fox_prompt(q, k, v, cum, cumt, *, t):
    b, ln, _ = q.shape
    n = ln // t
    kv = pl.BlockSpec((1, t, GROUP), lambda bb, qi, ki: (bb, jnp.minimum(ki, qi), 0))
    qs = pl.BlockSpec((1, t, GROUP), lambda bb, qi, ki: (bb, qi, 0))
    return pl.pallas_call(
        functools.partial(_fox_kernel, t=t, scale=HD ** -0.5), grid=(b, n, n),
        in_specs=[qs, kv, kv,
                  pl.BlockSpec((1, t, LANES), lambda bb, qi, ki: (bb, qi, 0)),
                  pl.BlockSpec((1, 8, t), lambda bb, qi, ki: (bb, 0, jnp.minimum(ki, qi)))],
        out_specs=qs, out_shape=jax.ShapeDtypeStruct((b, ln, GROUP), F32),
        scratch_shapes=[pltpu.VMEM((HEADS, t, 1), F32), pltpu.VMEM((HEADS, t, 1), F32), pltpu.VMEM((t, GROUP), F32)],
        compiler_params=_cparams(("parallel", "parallel", "arbitrary")), name="fox_prompt",
    )(q, k, v, cum, cumt)


def _moba_kernel(q_ref, k_ref, v_ref, km_ref, o_ref, m_scr, l_scr, acc_scr, sel_scr, *, scale):
    t = MOBA_BLOCK
    qi, j = pl.program_id(1), pl.program_id(2)
    lane = _iota((t, LANES), 1)
    rel = (_iota((t, t), 0) - _iota((t, t), 1)).astype(F32)

    @pl.when(j == 0)
    def _():
        _softmax_init(m_scr, l_scr, acc_scr)
        q, k, v, km = q_ref[0], k_ref[0], v_ref[0], km_ref[0]
        lane_f = lane.astype(F32)
        for h in range(HEADS):
            sl = slice(h * HD, (h + 1) * HD)
            gate = jnp.where(lane < qi, _nt3(q[:, sl], km[:, sl]), NEG)
            sel = jnp.zeros((t, LANES), F32)
            for r in range(MOBA_TOPK):
                mx = jnp.max(gate, axis=1, keepdims=True)
                idx = jnp.min(jnp.where(gate == mx, lane_f, float(LANES)), axis=1, keepdims=True)
                hit = lane_f == idx
                sel = jnp.where(hit, jnp.maximum(sel, jnp.where(qi > r, 1.0, 0.0)), sel)
                gate = jnp.where(hit, NEG, gate)
            sel_scr[h] = sel
            s = _nt(q[:, sl], k[:, sl]) * scale - (2.0 ** (-2 * (h + 1))) * rel
            _softmax_step(jnp.where(rel >= 0.0, s, NEG), v[:, sl], h, m_scr, l_scr, acc_scr)

    @pl.when(jnp.logical_and(j >= 1, j <= qi))
    def _():
        q, k, v = q_ref[0], k_ref[0], v_ref[0]
        n = j - 1
        dist = rel + ((qi - n) * t).astype(F32)
        for h in range(HEADS):
            sl = slice(h * HD, (h + 1) * HD)
            picked = jnp.sum(jnp.where(lane == n, sel_scr[h], 0.0), axis=1, keepdims=True)
            s = _nt(q[:, sl], k[:, sl]) * scale - (2.0 ** (-2 * (h + 1))) * dist
            _softmax_step(jnp.where(picked > 0.5, s, NEG), v[:, sl], h, m_scr, l_scr, acc_scr)

    @pl.when(j == qi)
    def _():
        _softmax_finish(o_ref, l_scr, acc_scr)


def _moba_prompt(q, k, v, kmean_pad):
    b, ln, _ = q.shape
    t = MOBA_BLOCK
    n = ln // t

    def kv_idx(bb, qi, j):
        return (bb, jnp.where(j == 0, qi, jnp.minimum(j - 1, jnp.maximum(qi - 1, 0))), 0)

    qs = pl.BlockSpec((1, t, GROUP), lambda bb, qi, j: (bb, qi, 0))
    kv = pl.BlockSpec((1, t, GROUP), kv_idx)
    return pl.pallas_call(
        functools.partial(_moba_kernel, scale=HD ** -0.5), grid=(b, n, n),
        in_specs=[qs, kv, kv, pl.BlockSpec((1, LANES, GROUP), lambda bb, qi, j: (bb, 0, 0))],
        out_specs=qs, out_shape=jax.ShapeDtypeStruct((b, ln, GROUP), F32),
        scratch_shapes=[pltpu.VMEM((HEADS, t, 1), F32), pltpu.VMEM((HEADS, t, 1), F32), pltpu.VMEM((t, GROUP), F32),
                        pltpu.VMEM((HEADS, t, LANES), F32)],
        compiler_params=_cparams(("parallel", "parallel", "arbitrary")), name="moba_prompt",
    )(q, k, v, kmean_pad)


def _pad_lanes(x):
    r = x.shape[0]
    lane = _iota((r, LANES), 1)
    out = jnp.zeros((r, LANES), F32)
    for h in range(HEADS):
        out = jnp.where(lane == h, jnp.broadcast_to(x[:, h:h + 1], (r, LANES)), out)
    return out


def _query_rows(q):
    grp = lax.shift_right_logical(_iota((LANES, GROUP), 1), 6)
    return jnp.where(grp == _iota((LANES, GROUP), 0), jnp.broadcast_to(q, (LANES, GROUP)), 0.0)


def _head_dots(a, b):
    ind = _bf(lax.shift_right_logical(_iota((GROUP, LANES), 0), 6) == _iota((GROUP, LANES), 1))
    return _mmx01(jnp.broadcast_to(a * b, (8, GROUP)), ind)[0:1, :]


def _fox_dec_kernel(pt_ref, q_ref, kn_ref, vn_ref, lfn_ref, kc_ref, vc_ref, lfc_ref, o_ref,
                    qm_scr, m_scr, l_scr, c_scr, acc_scr, *, scale):
    del pt_ref
    j, nj = pl.program_id(1), pl.num_programs(1)

    @pl.when(j == 0)
    def _():
        qm_scr[...] = _bf(_query_rows(q_ref[0]))
        m_scr[...] = jnp.full_like(m_scr, NEG)
        l_scr[...] = jnp.zeros_like(l_scr)
        acc_scr[...] = jnp.zeros_like(acc_scr)
        c_scr[...] = _pad_lanes(lfn_ref[0])

    lf = _pad_lanes(lfc_ref[0])
    upper = _bf(_iota((PAGE, PAGE), 0) < _iota((PAGE, PAGE), 1))
    s = _dg(_bf(kc_ref[0]), qm_scr[...], 1, 1) * scale + (_mm01(upper, lf) + c_scr[...])
    c_scr[...] = c_scr[...] + jnp.sum(lf, axis=0, keepdims=True)
    m_old = m_scr[...]
    m_new = jnp.maximum(m_old, jnp.max(s, axis=0, keepdims=True))
    p = jnp.exp(s - m_new)
    alpha = jnp.exp(m_old - m_new)
    l_scr[...] = alpha * l_scr[...] + jnp.sum(p, axis=0, keepdims=True)
    acc_scr[...] = acc_scr[...] * _expand_heads(alpha) + _expand_heads(p) * vc_ref[0]
    m_scr[...] = m_new

    @pl.when(j == nj - 1)
    def _():
        s_new = _head_dots(q_ref[0], kn_ref[0]) * scale
        m_f = jnp.maximum(m_scr[...], s_new)
        a = jnp.exp(m_scr[...] - m_f)
        bnew = jnp.exp(s_new - m_f)
        l_f = l_scr[...] * a + bnew
        acc = jnp.sum(acc_scr[...], axis=0, keepdims=True)
        o_ref[0] = (acc * _expand_heads(a) + _expand_heads(bnew) * vn_ref[0]) / _expand_heads(l_f)


def _fox_decode(pt_flat, q, kn, vn, lfn, kc, vc, lfc, *, n_pages):
    bd = q.shape[0]
    row = pl.BlockSpec((1, 1, GROUP), lambda b, j, pt: (b, 0, 0))

    def page(b, j, pt):
        return (pt[b * n_pages + (n_pages - 1 - j)], 0, 0)

    return pl.pallas_call(
        functools.partial(_fox_dec_kernel, scale=HD ** -0.5),
        grid_spec=pltpu.PrefetchScalarGridSpec(
            num_scalar_prefetch=1, grid=(bd, n_pages),
            in_specs=[row, row, row, pl.BlockSpec((1, 1, HEADS), lambda b, j, pt: (b, 0, 0)),
                      pl.BlockSpec((1, PAGE, GROUP), page), pl.BlockSpec((1, PAGE, GROUP), page),
                      pl.BlockSpec((1, PAGE, HEADS), page)],
            out_specs=row,
            scratch_shapes=[pltpu.VMEM((LANES, GROUP), BF16), pltpu.VMEM((1, LANES), F32), pltpu.VMEM((1, LANES), F32),
                            pltpu.VMEM((1, LANES), F32), pltpu.VMEM((PAGE, GROUP), F32)]),
        out_shape=jax.ShapeDtypeStruct((bd, 1, GROUP), F32),
        compiler_params=_cparams(("parallel", "arbitrary")), name="fox_decode",
    )(pt_flat, q, kn, vn, lfn, kc, vc, lfc)


def _moba_dec_kernel(pt_ref, q_ref, kn_ref, vn_ref, kc_ref, vc_ref, o_ref,
                     qh_scr, ql_scr, m_scr, l_scr, g_scr, accs_scr, mb_scr, lb_scr, gb_scr, accb_scr,
                     *, scale, past_len, n_past):
    del pt_ref
    j, nj = pl.program_id(1), pl.num_programs(1)
    ppb = MOBA_BLOCK // PAGE
    lane1 = _iota((1, LANES), 1)
    slopes = jnp.zeros((1, LANES), F32)
    for h in range(HEADS):
        slopes = jnp.where(lane1 == h, 2.0 ** (-2 * (h + 1)), slopes)

    @pl.when(j == 0)
    def _():
        qh, ql = _hi_lo(_query_rows(q_ref[0]))
        qh_scr[...] = qh
        ql_scr[...] = ql
        m_scr[...] = jnp.full_like(m_scr, NEG)
        l_scr[...] = jnp.zeros_like(l_scr)
        g_scr[...] = jnp.full_like(g_scr, NEG)
        accs_scr[...] = jnp.zeros_like(accs_scr)

    @pl.when(j % ppb == 0)
    def _():
        mb_scr[...] = jnp.full_like(mb_scr, NEG)
        lb_scr[...] = jnp.zeros_like(lb_scr)
        gb_scr[...] = jnp.zeros_like(gb_scr)
        accb_scr[...] = jnp.zeros_like(accb_scr)

    kh, kl = _hi_lo(kc_ref[0])
    raw = _dg(kh, qh_scr[...], 1, 1) + (_dg(kh, ql_scr[...], 1, 1) + _dg(kl, qh_scr[...], 1, 1))
    gb_scr[...] = gb_scr[...] + jnp.sum(raw, axis=0, keepdims=True)
    dist = (past_len - j * PAGE - _iota((PAGE, 1), 0)).astype(F32)
    s = raw * scale - slopes * dist
    m_old = mb_scr[...]
    m_new = jnp.maximum(m_old, jnp.max(s, axis=0, keepdims=True))
    p = jnp.exp(s - m_new)
    alpha = jnp.exp(m_old - m_new)
    lb_scr[...] = alpha * lb_scr[...] + jnp.sum(p, axis=0, keepdims=True)
    accb_scr[...] = accb_scr[...] * _expand_heads(alpha) + _expand_heads(p) * vc_ref[0]
    mb_scr[...] = m_new

    @pl.when(j % ppb == ppb - 1)
    def _():
        here = _iota((8, LANES), 0) == j // ppb
        m_scr[...] = jnp.where(here, mb_scr[...], m_scr[...])
        l_scr[...] = jnp.where(here, lb_scr[...], l_scr[...])
        g_scr[...] = jnp.where(here, gb_scr[...] * (1.0 / MOBA_BLOCK), g_scr[...])
        here2 = _iota((8, GROUP), 0) == j // ppb
        accs_scr[...] = jnp.where(here2, jnp.sum(accb_scr[...], axis=0, keepdims=True), accs_scr[...])

    @pl.when(j == nj - 1)
    def _():
        rows_f = _iota((8, LANES), 0).astype(F32)
        gate = g_scr[...]
        sel = jnp.zeros((8, LANES), F32)
        for _ in range(min(MOBA_TOPK, n_past)):
            mx = jnp.max(gate, axis=0, keepdims=True)
            idx = jnp.min(jnp.where(gate == mx, rows_f, 8.0), axis=0, keepdims=True)
            hit = rows_f == idx
            sel = jnp.where(hit, 1.0, sel)
            gate = jnp.where(hit, NEG, gate)
        s_own = _head_dots(q_ref[0], kn_ref[0]) * scale
        m_sel = jnp.where(sel > 0.5, m_scr[...], NEG)
        m_f = jnp.maximum(jnp.max(m_sel, axis=0, keepdims=True), s_own)
        w = jnp.where(sel > 0.5, jnp.exp(m_sel - m_f), 0.0)
        bnew = jnp.exp(s_own - m_f)
        l_f = jnp.sum(w * l_scr[...], axis=0, keepdims=True) + bnew
        acc = jnp.sum(_expand_heads(w) * accs_scr[...], axis=0, keepdims=True)
        o_ref[0] = (acc + _expand_heads(bnew) * vn_ref[0]) / _expand_heads(l_f)


def _moba_decode(pt_flat, q, kn, vn, kc, vc, *, n_pages):
    bd = q.shape[0]
    past_len = n_pages * PAGE
    assert past_len % MOBA_BLOCK == 0 and past_len // MOBA_BLOCK <= 8
    row = pl.BlockSpec((1, 1, GROUP), lambda b, j, pt: (b, 0, 0))

    def page(b, j, pt):
        return (pt[b * n_pages + j], 0, 0)

    return pl.pallas_call(
        functools.partial(_moba_dec_kernel, scale=HD ** -0.5, past_len=past_len, n_past=past_len // MOBA_BLOCK),
        grid_spec=pltpu.PrefetchScalarGridSpec(
            num_scalar_prefetch=1, grid=(bd, n_pages),
            in_specs=[row, row, row, pl.BlockSpec((1, PAGE, GROUP), page), pl.BlockSpec((1, PAGE, GROUP), page)],
            out_specs=row,
            scratch_shapes=[pltpu.VMEM((LANES, GROUP), BF16), pltpu.VMEM((LANES, GROUP), BF16),
                            pltpu.VMEM((8, LANES), F32), pltpu.VMEM((8, LANES), F32), pltpu.VMEM((8, LANES), F32),
                            pltpu.VMEM((8, GROUP), F32),
                            pltpu.VMEM((1, LANES), F32), pltpu.VMEM((1, LANES), F32), pltpu.VMEM((1, LANES), F32),
                            pltpu.VMEM((PAGE, GROUP), F32)]),
        out_shape=jax.ShapeDtypeStruct((bd, 1, GROUP), F32),
        compiler_params=_cparams(("parallel", "arbitrary")), name="moba_decode",
    )(pt_flat, q, kn, vn, kc, vc)


def _chunk_conv(u, cw, ext_scr, c_len):
    ext_scr[8:8 + c_len, :] = u
    out = (cw[3:4, :] * u + cw[2:3, :] * ext_scr[7:7 + c_len, :]
           + cw[1:2, :] * ext_scr[6:6 + c_len, :] + cw[0:1, :] * ext_scr[5:5 + c_len, :])
    ext_scr[0:8, :] = ext_scr[c_len:c_len + 8, :]
    return out


def _ssd_kernel(xbc_ref, z_ref, sm_ref, cw_ref, cb_ref, dtb_ref, alog_ref, d_ref, ng_ref,
                y_ref, hs_ref, cs_ref, ext_scr, h_scr, y_scr, *, c_len):
    c, nc = pl.program_id(1), pl.num_programs(1)

    @pl.when(c == 0)
    def _():
        ext_scr[0:8, :] = jnp.zeros((8, XBC_W), F32)
        h_scr[...] = jnp.zeros_like(h_scr)

    u = xbc_ref[0]
    act = _silu(_chunk_conv(u, cw_ref[...], ext_scr, c_len) + cb_ref[...])

    @pl.when(c == nc - 1)
    def _():
        cs_ref[0] = u[c_len - (CONV_W - 1):c_len, :]

    xa, bm, cm = act[:, 0:GROUP], act[:, GROUP:GROUP + 2 * HD], act[:, GROUP + 2 * HD:]
    dt = _softplus(sm_ref[0] + dtb_ref[...])
    a_neg = -jnp.exp(alog_ref[...])
    row, colm = _iota((c_len, c_len), 0), _iota((c_len, c_len), 1)
    causal = row >= colm
    gam = _mm01(_bf(causal), dt * a_neg)
    gam_t, dt_t = _tr(gam), _tr(dt)
    glast = gam[c_len - 1:c_len, :]
    eg = jnp.exp(gam)
    wst = jnp.exp(glast - gam) * dt
    elast = jnp.exp(glast)
    dvec = d_ref[...]
    for g in range(2):
        cg, bg = cm[:, g * HD:(g + 1) * HD], bm[:, g * HD:(g + 1) * HD]
        cb = _nt(cg, bg)
        for h in (2 * g, 2 * g + 1):
            sl = slice(h * HD, (h + 1) * HD)
            hl = slice(L_DT + h, L_DT + h + 1)
            dec = jnp.exp(jnp.where(causal, gam[:, hl] - gam_t[hl, :], NEG))
            xh = xa[:, sl]
            hprev = h_scr[h]
            y_scr[:, sl] = (_mm(cb * dec * dt_t[hl, :], xh) + _nt(cg, hprev) * eg[:, hl] + dvec[:, hl] * xh)
            h_scr[h] = elast[:, hl] * hprev + _tn3(xh * wst[:, hl], bg)
    yg = y_scr[...] * _silu(z_ref[0])
    y_ref[0] = yg * lax.rsqrt(jnp.mean(yg * yg, axis=-1, keepdims=True) + EPS) * ng_ref[...]

    @pl.when(c == nc - 1)
    def _():
        hs_ref[0] = h_scr[...]


def _ssd_prompt(proj3, cw, cb, dtb, alog, dvec, ng):
    b, ln, _ = proj3.shape
    c_len = SSD_CHUNK
    vec = lambda w: pl.BlockSpec((1, w), lambda bb, c: (0, 0))
    return pl.pallas_call(
        functools.partial(_ssd_kernel, c_len=c_len), grid=(b, ln // c_len),
        in_specs=[pl.BlockSpec((1, c_len, XBC_W), lambda bb, c: (bb, c, COL_XBC)),
                  pl.BlockSpec((1, c_len, GROUP), lambda bb, c: (bb, c, COL_ZA)),
                  pl.BlockSpec((1, c_len, LANES), lambda bb, c: (bb, c, COL_SMALL)),
                  pl.BlockSpec((CONV_W, XBC_W), lambda bb, c: (0, 0)), vec(XBC_W), vec(LANES), vec(LANES), vec(LANES),
                  vec(GROUP)],
        out_specs=[pl.BlockSpec((1, c_len, GROUP), lambda bb, c: (bb, c, 0)),
                   pl.BlockSpec((1, HEADS, HD, HD), lambda bb, c: (bb, 0, 0, 0)),
                   pl.BlockSpec((1, CONV_W - 1, XBC_W), lambda bb, c: (bb, 0, 0))],
        out_shape=[jax.ShapeDtypeStruct((b, ln, GROUP), F32), jax.ShapeDtypeStruct((b, HEADS, HD, HD), F32),
                   jax.ShapeDtypeStruct((b, CONV_W - 1, XBC_W), F32)],
        scratch_shapes=[pltpu.VMEM((c_len + 8, XBC_W), F32), pltpu.VMEM((HEADS, HD, HD), F32),
                        pltpu.VMEM((c_len, GROUP), F32)],
        compiler_params=_cparams(("parallel", "arbitrary")), name="ssd_prompt",
    )(proj3, proj3, proj3, cw, cb, dtb, alog, dvec, ng)


def _gdn_kernel(qkv_ref, z_ref, sm_ref, cw_ref, alog_ref, dtb_ref, ng_ref,
                y_ref, ss_ref, cs_ref, ext_scr, s_scr, o_scr, rhs_scr, *, c_len):
    c, nc = pl.program_id(1), pl.num_programs(1)

    @pl.when(c == 0)
    def _():
        ext_scr[0:8, :] = jnp.zeros((8, QKV_W), F32)
        s_scr[...] = jnp.zeros_like(s_scr)

    u = qkv_ref[0]
    act = _silu(_chunk_conv(u, cw_ref[...], ext_scr, c_len))

    @pl.when(c == nc - 1)
    def _():
        cs_ref[0] = u[c_len - (CONV_W - 1):c_len, :]

    hm = _head_mat()
    q, k, v = act[:, 0:GROUP], act[:, GROUP:2 * GROUP], act[:, 2 * GROUP:]
    qn = q * lax.rsqrt(_headsum(q * q, hm) + EPS) * (HD ** -0.5)
    kn = k * lax.rsqrt(_headsum(k * k, hm) + EPS)
    sm = sm_ref[0]
    beta = jax.nn.sigmoid(sm)
    gl = -jnp.exp(alog_ref[...]) * _softplus(sm + dtb_ref[...])
    row, colm = _iota((c_len, c_len), 0), _iota((c_len, c_len), 1)
    incl = row >= colm
    strict = row > colm
    eye = (row == colm).astype(F32)
    gam = _mm01(_bf(incl), gl)
    gam_t = _tr(gam)
    glast = gam[c_len - 1:c_len, :]
    eg, etail, elast = jnp.exp(gam), jnp.exp(glast - gam), jnp.exp(glast)
    n_fac = c_len.bit_length() - 2
    for h in range(HEADS):
        sl = slice(h * HD, (h + 1) * HD)
        gcol = slice(L_GA + h, L_GA + h + 1)
        b = beta[:, L_BETA + h:L_BETA + h + 1]
        dec = jnp.exp(jnp.where(incl, gam[:, gcol] - gam_t[gcol, :], NEG))
        kh, qh, vh = kn[:, sl], qn[:, sl], v[:, sl]
        nmat = b * _nt3(kh, kh) * jnp.where(strict, dec, 0.0)
        pinv = eye - nmat
        mpow = _mm3(nmat, nmat)
        for i in range(n_fac):
            pinv = pinv + _mm3(pinv, mpow)
            if i + 1 < n_fac:
                mpow = _mm3(mpow, mpow)
        rhs_scr[:, 0:HD] = b * vh
        rhs_scr[:, HD:2 * HD] = (b * eg[:, gcol]) * kh
        sol = _mm3(pinv, rhs_scr[...])
        s_old = s_scr[h]
        w = sol[:, 0:HD] - _mm3(sol[:, HD:2 * HD], s_old)
        o_scr[:, sl] = _mm(qh * eg[:, gcol], s_old) + _mm(_nt(qh, kh) * dec, w)
        s_scr[h] = elast[:, gcol] * s_old + _tn3(kh * etail[:, gcol], w)
    o = o_scr[...]
    on = o * lax.rsqrt(_headsum(o * o, hm) * (1.0 / HD) + EPS) * ng_ref[...]
    y_ref[0] = on * _silu(z_ref[0])

    @pl.when(c == nc - 1)
    def _():
        ss_ref[0] = s_scr[...]


def _gdn_prompt(proj3, cw, alog, dtb, ng):
    b, ln, _ = proj3.shape
    c_len = GDN_CHUNK
    vec = lambda w: pl.BlockSpec((1, w), lambda bb, c: (0, 0))
    return pl.pallas_call(
        functools.partial(_gdn_kernel, c_len=c_len), grid=(b, ln // c_len),
        in_specs=[pl.BlockSpec((1, c_len, QKV_W), lambda bb, c: (bb, c, COL_QKV)),
                  pl.BlockSpec((1, c_len, GROUP), lambda bb, c: (bb, c, COL_ZC)),
                  pl.BlockSpec((1, c_len, LANES), lambda bb, c: (bb, c, COL_SMALL)),
                  pl.BlockSpec((CONV_W, QKV_W), lambda bb, c: (0, 0)), vec(LANES), vec(LANES), vec(GROUP)],
        out_specs=[pl.BlockSpec((1, c_len, GROUP), lambda bb, c: (bb, c, 0)),
                   pl.BlockSpec((1, HEADS, HD, HD), lambda bb, c: (bb, 0, 0, 0)),
                   pl.BlockSpec((1, CONV_W - 1, QKV_W), lambda bb, c: (bb, 0, 0))],
        out_shape=[jax.ShapeDtypeStruct((b, ln, GROUP), F32), jax.ShapeDtypeStruct((b, HEADS, HD, HD), F32),
                   jax.ShapeDtypeStruct((b, CONV_W - 1, QKV_W), F32)],
        scratch_shapes=[pltpu.VMEM((c_len + 8, QKV_W), F32), pltpu.VMEM((HEADS, HD, HD), F32),
                        pltpu.VMEM((c_len, GROUP), F32), pltpu.VMEM((c_len, 2 * HD), F32)],
        compiler_params=_cparams(("parallel", "arbitrary")), name="gdn_prompt",
    )(proj3, proj3, proj3, cw, alog, dtb, ng)


def _step_conv(u, buf_ref, cw, width):
    return (cw[3:4, :] * u + cw[2:3, :] * buf_ref[:, 2 * width:3 * width]
            + cw[1:2, :] * buf_ref[:, width:2 * width] + cw[0:1, :] * buf_ref[:, 0:width])


def _ssd_step_kernel(xbc_ref, z_ref, sm_ref, buf_ref, st_ref, cw_ref, cb_ref, dtb_ref, alog_ref, d_ref, ng_ref,
                     rep_ref, tile_ref, rept_ref, y_ref, so_ref, bo_ref, y_scr):
    u = xbc_ref[...]
    act = _silu(_step_conv(u, buf_ref, cw_ref[...], XBC_W) + cb_ref[...])
    bo_ref[:, 0:2 * XBC_W] = buf_ref[:, XBC_W:3 * XBC_W]
    bo_ref[:, 2 * XBC_W:3 * XBC_W] = u
    xa, bm, cm = act[:, 0:GROUP], act[:, GROUP:GROUP + 2 * HD], act[:, GROUP + 2 * HD:]
    dt = _softplus(sm_ref[...] + dtb_ref[...])
    dec = jnp.exp(dt * (-jnp.exp(alog_ref[...])))
    dvec = d_ref[...]
    rep, tile, rept = rep_ref[...], tile_ref[...], rept_ref[...]
    for h in range(HEADS):
        g = h // 2
        sl = slice(h * HD, (h + 1) * HD)
        fl = slice(h * HD * HD, (h + 1) * HD * HD)
        hl = slice(L_DT + h, L_DT + h + 1)
        xh = xa[:, sl]
        s_new = dec[:, hl] * st_ref[:, fl] + (_mmx01(dt[:, hl] * xh, rep)
                                               * _mmx01(bm[:, g * HD:(g + 1) * HD], tile))
        so_ref[:, fl] = s_new
        y_scr[:, sl] = _mmx01(s_new * _mmx01(cm[:, g * HD:(g + 1) * HD], tile), rept) + dvec[:, hl] * xh
    yg = y_scr[...] * _silu(z_ref[...])
    y_ref[...] = yg * lax.rsqrt(jnp.mean(yg * yg, axis=-1, keepdims=True) + EPS) * ng_ref[...]


def _gdn_step_kernel(qkv_ref, z_ref, sm_ref, buf_ref, st_ref, cw_ref, alog_ref, dtb_ref, ng_ref,
                     rep_ref, tile_ref, tilet_ref, y_ref, so_ref, bo_ref, o_scr):
    u = qkv_ref[...]
    act = _silu(_step_conv(u, buf_ref, cw_ref[...], QKV_W))
    bo_ref[:, 0:2 * QKV_W] = buf_ref[:, QKV_W:3 * QKV_W]
    bo_ref[:, 2 * QKV_W:3 * QKV_W] = u
    hm = _head_mat()
    q, k, v = act[:, 0:GROUP], act[:, GROUP:2 * GROUP], act[:, 2 * GROUP:]
    qn = q * lax.rsqrt(_headsum(q * q, hm) + EPS) * (HD ** -0.5)
    kn = k * lax.rsqrt(_headsum(k * k, hm) + EPS)
    qk = _headsum(qn * kn, hm)
    sm = sm_ref[...]
    beta = jax.nn.sigmoid(sm)
    eg = jnp.exp(-jnp.exp(alog_ref[...]) * _softplus(sm + dtb_ref[...]))
    rep, tile, tilet = rep_ref[...], tile_ref[...], tilet_ref[...]
    for h in range(HEADS):
        sl = slice(h * HD, (h + 1) * HD)
        fl = slice(h * HD * HD, (h + 1) * HD * HD)
        e = eg[:, L_GA + h:L_GA + h + 1]
        s_old = st_ref[:, fl]
        krep = _mmx01(kn[:, sl], rep)
        ks = _mmx01(s_old * krep, tilet)
        qs = _mmx01(s_old * _mmx01(qn[:, sl], rep), tilet)
        w = beta[:, L_BETA + h:L_BETA + h + 1] * (v[:, sl] - e * ks)
        o_scr[:, sl] = e * qs + qk[:, sl] * w
        so_ref[:, fl] = e * s_old + krep * _mmx01(w, tile)
    o = o_scr[...]
    on = o * lax.rsqrt(_headsum(o * o, hm) * (1.0 / HD) + EPS) * ng_ref[...]
    y_ref[...] = on * _silu(z_ref[...])


def _spread_mats():
    eye = jnp.eye(HD, dtype=BF16)
    rep = jnp.repeat(eye, HD, axis=1)
    tile = jnp.tile(eye, (1, HD))
    return rep, tile, rep.T, tile.T


def _ssd_step(proj, buf, state, cw, cb, dtb, alog, dvec, ng, *, tb):
    bd = proj.shape[0]
    rep, tile, rept, _ = _spread_mats()
    flat = HEADS * HD * HD
    full = lambda a: pl.BlockSpec(a.shape, lambda i: (0,) * a.ndim)
    return pl.pallas_call(
        _ssd_step_kernel, grid=(bd // tb,),
        in_specs=[pl.BlockSpec((tb, XBC_W), lambda i: (i, COL_XBC)), pl.BlockSpec((tb, GROUP), lambda i: (i, COL_ZA)),
                  pl.BlockSpec((tb, LANES), lambda i: (i, COL_SMALL)),
                  pl.BlockSpec((tb, 3 * XBC_W), lambda i: (i, 0)), pl.BlockSpec((tb, flat), lambda i: (i, 0)),
                  full(cw), full(cb), full(dtb), full(alog), full(dvec), full(ng), full(rep), full(tile), full(rept)],
        out_specs=[pl.BlockSpec((tb, GROUP), lambda i: (i, 0)), pl.BlockSpec((tb, flat), lambda i: (i, 0)),
                   pl.BlockSpec((tb, 3 * XBC_W), lambda i: (i, 0))],
        out_shape=[jax.ShapeDtypeStruct((bd, GROUP), F32), jax.ShapeDtypeStruct((bd, flat), F32),
                   jax.ShapeDtypeStruct((bd, 3 * XBC_W), F32)],
        scratch_shapes=[pltpu.VMEM((tb, GROUP), F32)],
        compiler_params=_cparams(("parallel",)), name="ssd_step",
    )(proj, proj, proj, buf, state, cw, cb, dtb, alog, dvec, ng, rep, tile, rept)


def _gdn_step(proj, buf, state, cw, alog, dtb, ng, *, tb):
    bd = proj.shape[0]
    rep, tile, _, tilet = _spread_mats()
    flat = HEADS * HD * HD
    full = lambda a: pl.BlockSpec(a.shape, lambda i: (0,) * a.ndim)
    return pl.pallas_call(
        _gdn_step_kernel, grid=(bd // tb,),
        in_specs=[pl.BlockSpec((tb, QKV_W), lambda i: (i, COL_QKV)), pl.BlockSpec((tb, GROUP), lambda i: (i, COL_ZC)),
                  pl.BlockSpec((tb, LANES), lambda i: (i, COL_SMALL)),
                  pl.BlockSpec((tb, 3 * QKV_W), lambda i: (i, 0)), pl.BlockSpec((tb, flat), lambda i: (i, 0)),
                  full(cw), full(alog), full(dtb), full(ng), full(rep), full(tile), full(tilet)],
        out_specs=[pl.BlockSpec((tb, GROUP), lambda i: (i, 0)), pl.BlockSpec((tb, flat), lambda i: (i, 0)),
                   pl.BlockSpec((tb, 3 * QKV_W), lambda i: (i, 0))],
        out_shape=[jax.ShapeDtypeStruct((bd, GROUP), F32), jax.ShapeDtypeStruct((bd, flat), F32),
                   jax.ShapeDtypeStruct((bd, 3 * QKV_W), F32)],
        scratch_shapes=[pltpu.VMEM((tb, GROUP), F32)],
        compiler_params=_cparams(("parallel",)), name="gdn_step",
    )(proj, proj, proj, buf, state, cw, alog, dtb, ng, rep, tile, tilet)


def _outproj_kernel(ya_ref, yb_ref, yc_ref, yd_ref, w_ref, x_ref, gt_ref, o_ref):
    acc = _dot(_bf(ya_ref[...]), w_ref[0:GROUP, :])
    acc += _dot(_bf(yb_ref[...]), w_ref[GROUP:2 * GROUP, :])
    acc += _dot(_bf(yc_ref[...]), w_ref[2 * GROUP:3 * GROUP, :])
    acc += _dot(_bf(yd_ref[...]), w_ref[3 * GROUP:4 * GROUP, :])
    o_ref[...] = x_ref[...] + gt_ref[0] * acc


def _out_proj(ya, yb, yc, yd, w_bf, x2d, mod, *, tm, per_row, rows_per_batch):
    t, d = x2d.shape
    yspec = pl.BlockSpec((tm, GROUP), lambda i: (i, 0))
    return pl.pallas_call(
        _outproj_kernel, grid=(t // tm,),
        in_specs=[yspec, yspec, yspec, yspec, pl.BlockSpec((d, d), lambda i: (0, 0)),
                  pl.BlockSpec((tm, d), lambda i: (i, 0)), _mod_spec(per_row, tm, rows_per_batch, 2)],
        out_specs=pl.BlockSpec((tm, d), lambda i: (i, 0)),
        out_shape=jax.ShapeDtypeStruct((t, d), F32),
        compiler_params=_cparams(("parallel",)), name="out_proj",
    )(ya, yb, yc, yd, w_bf, x2d, mod)


def _ffn_kernel(x_ref, g_ref, sc_ref, sh_ref, gt_ref, wg_ref, wu_ref, wd_ref, o_ref, h_scr, acc_scr):
    f, nf = pl.program_id(1), pl.num_programs(1)

    @pl.when(f == 0)
    def _():
        h_scr[...] = _bf(_normmod(x_ref[...], g_ref[...], sc_ref[0], sh_ref[0]))
        acc_scr[...] = jnp.zeros_like(acc_scr)

    h = h_scr[...]
    a = _silu(_dot(h, wg_ref[...])) * _dot(h, wu_ref[...])
    acc_scr[...] += _dot(_bf(a), wd_ref[...])

    @pl.when(f == nf - 1)
    def _():
        o_ref[...] = x_ref[...] + gt_ref[0] * acc_scr[...]


def _ffn_dense(x2d, g, mod, wg, wu, wd, *, tm, tf, per_row, rows_per_batch):
    t, d = x2d.shape
    ff = wg.shape[1]
    ms = functools.partial(_mod_spec, per_row, tm, rows_per_batch)
    return pl.pallas_call(
        _ffn_kernel, grid=(t // tm, ff // tf),
        in_specs=[pl.BlockSpec((tm, d), lambda i, f: (i, 0)), pl.BlockSpec((1, d), lambda i, f: (0, 0)),
                  ms(4), ms(3), ms(5),
                  pl.BlockSpec((d, tf), lambda i, f: (0, f)), pl.BlockSpec((d, tf), lambda i, f: (0, f)),
                  pl.BlockSpec((tf, d), lambda i, f: (f, 0))],
        out_specs=pl.BlockSpec((tm, d), lambda i, f: (i, 0)),
        out_shape=jax.ShapeDtypeStruct((t, d), F32),
        scratch_shapes=[pltpu.VMEM((tm, d), BF16), pltpu.VMEM((tm, d), F32)],
        compiler_params=_cparams(("parallel", "arbitrary")), name="ffn_dense",
    )(x2d, g, mod, mod, mod, wg, wu, wd)


def _moe_kernel(x_ref, g_ref, sc_ref, sh_ref, gt_ref, r_ref, wg_ref, wu_ref, wd_ref, o_ref,
                h_scr, acc_scr, gate_scr, *, n_experts):
    e, f = pl.program_id(1), pl.program_id(2)
    ne, nf = pl.num_programs(1), pl.num_programs(2)
    tm = x_ref.shape[0]
    lane = _iota((tm, LANES), 1)

    @pl.when(jnp.logical_and(e == 0, f == 0))
    def _():
        hf = _normmod(x_ref[...], g_ref[...], sc_ref[0], sh_ref[0])
        h_scr[...] = _bf(hf)
        acc_scr[...] = jnp.zeros_like(acc_scr)
        lane_f = lane.astype(F32)
        logits = jnp.where(lane < n_experts, _mm3(hf, r_ref[...]), NEG)
        v1 = jnp.max(logits, axis=1, keepdims=True)
        i1 = jnp.min(jnp.where(logits == v1, lane_f, float(LANES)), axis=1, keepdims=True)
        rest = jnp.where(lane_f == i1, NEG, logits)
        v2 = jnp.max(rest, axis=1, keepdims=True)
        i2 = jnp.min(jnp.where(rest == v2, lane_f, float(LANES)), axis=1, keepdims=True)
        e2 = jnp.exp(v2 - v1)
        w1 = 1.0 / (1.0 + e2)
        gate_scr[...] = jnp.where(lane_f == i1, w1, jnp.where(lane_f == i2, e2 * w1, 0.0))

    h = h_scr[...]
    ge = jnp.sum(jnp.where(lane == e, gate_scr[...], 0.0), axis=1, keepdims=True)
    a = _silu(_dot(h, wg_ref[0])) * _dot(h, wu_ref[0]) * ge
    acc_scr[...] += _dot(_bf(a), wd_ref[0])

    @pl.when(jnp.logical_and(e == ne - 1, f == nf - 1))
    def _():
        o_ref[...] = x_ref[...] + gt_ref[0] * acc_scr[...]


def _ffn_moe(x2d, g, mod, router_pad, wg, wu, wd, *, tm, tf, per_row, rows_per_batch):
    t, d = x2d.shape
    ne, _, ff = wg.shape
    ms = functools.partial(_mod_spec, per_row, tm, rows_per_batch)
    return pl.pallas_call(
        functools.partial(_moe_kernel, n_experts=ne), grid=(t // tm, ne, ff // tf),
        in_specs=[pl.BlockSpec((tm, d), lambda i, e, f: (i, 0)), pl.BlockSpec((1, d), lambda i, e, f: (0, 0)),
                  ms(4), ms(3), ms(5), pl.BlockSpec((d, LANES), lambda i, e, f: (0, 0)),
                  pl.BlockSpec((1, d, tf), lambda i, e, f: (e, 0, f)), pl.BlockSpec((1, d, tf), lambda i, e, f: (e, 0, f)),
                  pl.BlockSpec((1, tf, d), lambda i, e, f: (e, f, 0))],
        out_specs=pl.BlockSpec((tm, d), lambda i, e, f: (i, 0)),
        out_shape=jax.ShapeDtypeStruct((t, d), F32),
        scratch_shapes=[pltpu.VMEM((tm, d), BF16), pltpu.VMEM((tm, d), F32), pltpu.VMEM((tm, LANES), F32)],
        compiler_params=_cparams(("parallel", "arbitrary", "arbitrary")), name="ffn_moe",
    )(x2d, g, mod, mod, mod, router_pad, wg, wu, wd)


def _lane_row(vals, offset):
    return jnp.zeros((1, LANES), F32).at[0, offset:offset + vals.shape[0]].set(vals.astype(F32))


def _reorder_w_in(w):
    pad = jnp.zeros((w.shape[0], LANES - 16), w.dtype)
    return jnp.concatenate([w[:, 256:768], w[:, 0:256], w[:, 772:1540], w[:, 1544:2312], w[:, 2320:2576],
                            w[:, 2576:3344], w[:, 768:772], w[:, 1540:1544], w[:, 2312:2320], pad], axis=1)


def _tile_heads(g):
    return jnp.tile(g.astype(F32), HEADS).reshape(1, GROUP)


def kernel(x_prompt, x_sample, cache_fox_k, cache_fox_v, cache_fox_logf, cache_moba_k, cache_moba_v, state_ssm, state_ssm_conv, state_gdn, state_gdn_conv, page_table, c_prompt, c_sample, w_ada, b_ada, norm_mix, norm_ffn, w_in, w_out, ssd_conv_w, ssd_conv_b, ssd_dt_bias, ssd_a_log, ssd_d, ssd_norm, fox_b_f, fox_q_norm, fox_k_norm, gdn_conv_w, gdn_a_log, gdn_dt_bias, gdn_norm, moba_q_norm, moba_k_norm, ffn_w_gate, ffn_w_up, ffn_w_down, moe_router, moe_w_gate, moe_w_up, moe_w_down):
    bp, ln, d = x_prompt.shape
    bd = x_sample.shape[0]
    depth = w_in.shape[0]
    n_pages = page_table.shape[1]
    n_pool = cache_fox_k.shape[1]
    assert x_sample.shape[1] == 1 and d == D_MODEL and ln % MOBA_BLOCK == 0
    tp = bp * ln
    tm_p = 512 if ln % 512 == 0 else 256
    tm_s = bd
    t_attn = 256

    mod = _ada_mod(jnp.concatenate([c_prompt, c_sample], axis=0), w_ada, b_ada)
    pt_flat = page_table.reshape(-1).astype(jnp.int32)
    xp = x_prompt.reshape(tp, d)
    xs = x_sample.reshape(bd, d)
    outs_p, outs_s = [], []
    for l in range(depth):
        mod_p = mod[l, :bp].reshape(bp, 1, 6 * d)
        mod_s = mod[l, bp:].reshape(1, bd, 6 * d)
        kw_p = dict(tm=tm_p, per_row=False, rows_per_batch=ln)
        kw_s = dict(tm=tm_s, per_row=True, rows_per_batch=1)
        g_mix = norm_mix[l].reshape(1, d)
        g_ffn = norm_ffn[l].reshape(1, d)
        w_in_bf = _bf(_reorder_w_in(w_in[l]))
        w_out_bf = _bf(w_out[l])
        gfq, gfk = _tile_heads(fox_q_norm[l]), _tile_heads(fox_k_norm[l])
        gmq, gmk = _tile_heads(moba_q_norm[l]), _tile_heads(moba_k_norm[l])
        bf_full = _lane_row(fox_b_f[l], L_FF)
        ssd_par = (ssd_conv_w[l], ssd_conv_b[l].reshape(1, XBC_W), _lane_row(ssd_dt_bias[l], L_DT),
                   _lane_row(ssd_a_log[l], L_DT), _lane_row(ssd_d[l], L_DT), ssd_norm[l].reshape(1, GROUP))
        gdn_par = (gdn_conv_w[l], _lane_row(gdn_a_log[l], L_GA), _lane_row(gdn_dt_bias[l], L_GA),
                   _tile_heads(gdn_norm[l]))

        proj = _in_proj(xp, g_mix, mod_p, w_in_bf, **kw_p).reshape(bp, ln, PROJ_COLS)
        fqn, fkn, fvo, mqn, mkn, mvo, lf, cum, cumt, kmean = _prep(proj, gfq, gfk, gmq, gmk, bf_full, tm=MOBA_BLOCK, seq=True)
        y_a, ssm_h, ssm_buf = _ssd_prompt(proj, *ssd_par)
        y_c, gdn_s, gdn_buf = _gdn_prompt(proj, *gdn_par)
        y_b = _fox_prompt(fqn, fkn, fvo, cum, cumt, t=t_attn)
        kmean_pad = jnp.pad(kmean.reshape(bp, ln // MOBA_BLOCK, GROUP), ((0, 0), (0, LANES - ln // MOBA_BLOCK), (0, 0)))
        y_d = _moba_prompt(mqn, mkn, mvo, kmean_pad)
        flat = lambda a: a.reshape(tp, GROUP)
        xp = _out_proj(flat(y_a), flat(y_b), flat(y_c), flat(y_d), w_out_bf, xp, mod_p, **kw_p)
        heads = lambda a: a.reshape(bp, ln, HEADS, HD)
        outs_p.append((heads(fkn), heads(fvo), lf, heads(mkn), heads(mvo), ssm_h, ssm_buf, gdn_s, gdn_buf))

        proj_s = _in_proj(xs, g_mix, mod_s, w_in_bf, **kw_s)
        sq, sk, sv, smq, smk, smv, slf = _prep(proj_s.reshape(1, bd, PROJ_COLS), gfq, gfk, gmq, gmk, bf_full, tm=bd, seq=False)
        rows = lambda a: a.reshape(bd, 1, GROUP)
        ys_a, ssm_new, ssm_buf_new = _ssd_step(proj_s, state_ssm_conv[l].reshape(bd, 3 * XBC_W),
                                               state_ssm[l].reshape(bd, HEADS * HD * HD), *ssd_par, tb=min(32, bd))
        ys_c, gdn_new, gdn_buf_new = _gdn_step(proj_s, state_gdn_conv[l].reshape(bd, 3 * QKV_W),
                                               state_gdn[l].reshape(bd, HEADS * HD * HD), *gdn_par, tb=min(32, bd))
        ys_b = _fox_decode(pt_flat, rows(sq), rows(sk), rows(sv), slf.reshape(bd, 1, HEADS),
                           cache_fox_k[l].reshape(n_pool, PAGE, GROUP), cache_fox_v[l].reshape(n_pool, PAGE, GROUP),
                           cache_fox_logf[l], n_pages=n_pages)
        ys_d = _moba_decode(pt_flat, rows(smq), rows(smk), rows(smv),
                            cache_moba_k[l].reshape(n_pool, PAGE, GROUP), cache_moba_v[l].reshape(n_pool, PAGE, GROUP),
                            n_pages=n_pages)
        xs = _out_proj(ys_a, ys_b.reshape(bd, GROUP), ys_c, ys_d.reshape(bd, GROUP), w_out_bf, xs, mod_s, **kw_s)
        heads_s = lambda a: a.reshape(bd, 1, HEADS, HD)
        outs_s.append((heads_s(sk), heads_s(sv), slf.reshape(bd, 1, HEADS), heads_s(smk), heads_s(smv),
                       ssm_new.reshape(bd, HEADS, HD, HD), ssm_buf_new.reshape(bd, CONV_W - 1, XBC_W),
                       gdn_new.reshape(bd, HEADS, HD, HD), gdn_buf_new.reshape(bd, CONV_W - 1, QKV_W)))

        i = l // 2
        if l % 2 == 0:
            wg, wu, wd = _bf(ffn_w_gate[i]), _bf(ffn_w_up[i]), _bf(ffn_w_down[i])
            tf = wg.shape[1] // 2
            xp = _ffn_dense(xp, g_ffn, mod_p, wg, wu, wd, tf=tf, **kw_p)
            xs = _ffn_dense(xs, g_ffn, mod_s, wg, wu, wd, tf=tf, **kw_s)
        else:
            wg, wu, wd = _bf(moe_w_gate[i]), _bf(moe_w_up[i]), _bf(moe_w_down[i])
            ne = wg.shape[0]
            router_pad = jnp.pad(moe_router[i], ((0, 0), (0, LANES - ne)))
            tf = wg.shape[2] // 2
            xp = _ffn_moe(xp, g_ffn, mod_p, router_pad, wg, wu, wd, tf=tf, **kw_p)
            xs = _ffn_moe(xs, g_ffn, mod_s, router_pad, wg, wu, wd, tf=tf, **kw_s)

    stack = lambda outs, j: jnp.stack([o[j] for o in outs], axis=0)
    return ((xp.reshape(bp, ln, d), xs.reshape(bd, 1, d))
            + tuple(stack(outs_p, j) for j in range(9)) + tuple(stack(outs_s, j) for j in range(9)))
```

```python
import functools

import jax
import jax.numpy as jnp
from jax import lax
from jax.experimental import pallas as pl
from jax.experimental.pallas import tpu as pltpu

F32 = jnp.float32
BF16 = jnp.bfloat16
EPS = 1e-6
NEG = -1e30

D_MODEL = 1024
GROUP = 256
HEADS = 4
HD = 64
PAGE = 128
MOBA_BLOCK = 256
MOBA_TOPK = 3
CONV_W = 4
SSD_CHUNK = 128
GDN_CHUNK = 64
GDN_TOK = 256
LANES = 128

XBC_W, QKV_W = 512, 768
COL_XBC = 0
COL_ZA = 2
COL_FQ, COL_FK, COL_FV = 3, 4, 5
COL_QKV = 2
COL_ZC = 9
COL_MQ, COL_MK, COL_MV = 10, 11, 12
COL_SMALL = 26
PROJ_COLS = 3456
L_DT, L_FF, L_BETA, L_GA = 0, 4, 8, 12


def _dot(a, b):
    return jnp.dot(a, b, preferred_element_type=F32)


def _dg(a, b, ca, cb):
    return lax.dot_general(a, b, (((ca,), (cb,)), ((), ())), preferred_element_type=F32)


def _bf(a):
    return a.astype(BF16)


def _hi_lo(a):
    hi = _bf(a)
    return hi, _bf(a - hi.astype(F32))


def _split3(a):
    a1 = _bf(a)
    r = a - a1.astype(F32)
    a2 = _bf(r)
    return a1, a2, _bf(r - a2.astype(F32))


def _mm(a, b):
    return _dot(_bf(a), _bf(b))


def _nt(a, b):
    return _dg(_bf(a), _bf(b), 1, 1)


def _tn(a, b):
    return _dg(_bf(a), _bf(b), 0, 0)


def _x3(f, a, b):
    a1, a2 = _hi_lo(a)
    b1, b2 = _hi_lo(b)
    return f(a1, b1) + (f(a1, b2) + f(a2, b1))


def _mm3(a, b):
    return _x3(_dot, a, b)


def _nt3(a, b):
    return _x3(lambda x, y: _dg(x, y, 1, 1), a, b)


def _tn3(a, b):
    return _x3(lambda x, y: _dg(x, y, 0, 0), a, b)


def _mm01(m01, x):
    x1, x2, x3 = _split3(x)
    return _dot(m01, x1) + (_dot(m01, x2) + _dot(m01, x3))


def _mmx01(x, m01):
    x1, x2, x3 = _split3(x)
    return _dot(x1, m01) + (_dot(x2, m01) + _dot(x3, m01))


def _tr(x):
    n = x.shape[1]
    eye = _bf(lax.broadcasted_iota(jnp.int32, (n, n), 0) == lax.broadcasted_iota(jnp.int32, (n, n), 1))
    x1, x2, x3 = _split3(x)
    return _dg(eye, x1, 1, 1) + (_dg(eye, x2, 1, 1) + _dg(eye, x3, 1, 1))


def _iota(shape, dim):
    return lax.broadcasted_iota(jnp.int32, shape, dim)


def _silu(x):
    return x * jax.nn.sigmoid(x)


def _softplus(x):
    return jnp.maximum(x, 0.0) + jnp.log1p(jnp.exp(-jnp.abs(x)))


def _log_sigmoid(x):
    return -_softplus(-x)


def _head_mat():
    r = lax.shift_right_logical(_iota((GROUP, GROUP), 0), 6)
    c = lax.shift_right_logical(_iota((GROUP, GROUP), 1), 6)
    return _bf(r == c)


def _headsum(x, hm):
    x1, x2 = _hi_lo(x)
    return _dot(x1, hm) + _dot(x2, hm)


def _normmod(x, g, sc, sh):
    y = x * lax.rsqrt(jnp.mean(x * x, axis=-1, keepdims=True) + EPS) * g
    return y * (1.0 + sc) + sh


def _cparams(sem, vmem_mb=48):
    return pltpu.CompilerParams(dimension_semantics=sem, vmem_limit_bytes=vmem_mb * 1024 * 1024)


def _mod_spec(per_row, tm, rows_per_batch, chunk):
    if per_row:
        return pl.BlockSpec((1, tm, D_MODEL), lambda i, *_: (0, i, chunk))
    return pl.BlockSpec((1, 1, D_MODEL), lambda i, *_: ((i * tm) // rows_per_batch, 0, chunk))


def _ada_kernel(c_ref, w_ref, b_ref, o_ref):
    o_ref[0] = _mm(_silu(c_ref[...]), w_ref[0]) + b_ref[0]


def _ada_mod(c_all, w_ada, b_ada):
    depth, d, n = w_ada.shape
    r = c_all.shape[0]
    tn = 1536
    return pl.pallas_call(
        _ada_kernel, grid=(depth, n // tn),
        in_specs=[pl.BlockSpec((r, d), lambda l, j: (0, 0)),
                  pl.BlockSpec((1, d, tn), lambda l, j: (l, 0, j)),
                  pl.BlockSpec((1, 1, tn), lambda l, j: (l, 0, j))],
        out_specs=pl.BlockSpec((1, r, tn), lambda l, j: (l, 0, j)),
        out_shape=jax.ShapeDtypeStruct((depth, r, n), F32),
        compiler_params=_cparams(("parallel", "parallel")), name="ada_mod",
    )(c_all, w_ada, b_ada.reshape(depth, 1, n))


def _inproj_kernel(x_ref, g_ref, sc_ref, sh_ref, w_ref, o_ref, h_scr):
    @pl.when(pl.program_id(1) == 0)
    def _():
        h_scr[...] = _bf(_normmod(x_ref[...], g_ref[...], sc_ref[0], sh_ref[0]))

    o_ref[...] = _dot(h_scr[...], w_ref[...])


def _in_proj(x2d, g, mod, w_bf, *, tm, per_row, rows_per_batch):
    t, d = x2d.shape
    n = w_bf.shape[1]
    tn = n // 3
    return pl.pallas_call(
        _inproj_kernel, grid=(t // tm, n // tn),
        in_specs=[pl.BlockSpec((tm, d), lambda i, j: (i, 0)),
                  pl.BlockSpec((1, d), lambda i, j: (0, 0)),
                  _mod_spec(per_row, tm, rows_per_batch, 1),
                  _mod_spec(per_row, tm, rows_per_batch, 0),
                  pl.BlockSpec((d, tn), lambda i, j: (0, j))],
        out_specs=pl.BlockSpec((tm, tn), lambda i, j: (i, j)),
        out_shape=jax.ShapeDtypeStruct((t, n), F32),
        scratch_shapes=[pltpu.VMEM((tm, d), BF16)],
        compiler_params=_cparams(("parallel", "arbitrary")), name="in_proj",
    )(x2d, g, mod, mod, w_bf)


def _prep_kernel(fq_ref, fk_ref, fv_ref, mq_ref, mk_ref, mv_ref, sm_ref, gfq_ref, gfk_ref, gmq_ref, gmk_ref, bf_ref,
                 fqn_ref, fkt_ref, fvt_ref, mqn_ref, mkt_ref, mvt_ref, lft_ref, cum_ref, cumt_ref, kmean_ref, carry_scr):
    hm = _head_mat()

    def hnorm(x, g):
        return x * lax.rsqrt(_headsum(x * x, hm) * (1.0 / HD) + EPS) * g

    def put_t(ref, x):
        xt = _tr(x)
        for h in range(HEADS):
            ref[0, h] = xt[h * HD:(h + 1) * HD, :]

    fqn_ref[0] = hnorm(fq_ref[0], gfq_ref[...])
    put_t(fkt_ref, hnorm(fk_ref[0], gfk_ref[...]))
    put_t(fvt_ref, fv_ref[0])
    mqn_ref[0] = hnorm(mq_ref[0], gmq_ref[...])
    mkn = hnorm(mk_ref[0], gmk_ref[...])
    put_t(mkt_ref, mkn)
    put_t(mvt_ref, mv_ref[0])
    kmean_ref[0, 0] = jnp.mean(mkn, axis=0, keepdims=True)
    lf = _log_sigmoid(sm_ref[0] + bf_ref[...])
    lft_ref[0] = _tr(lf)[L_FF:L_FF + HEADS, :]

    @pl.when(pl.program_id(1) == 0)
    def _():
        carry_scr[...] = jnp.zeros_like(carry_scr)

    tm = lf.shape[0]
    tril = _bf(_iota((tm, tm), 0) >= _iota((tm, tm), 1))
    cum = _mm01(tril, lf) + carry_scr[...]
    carry_scr[...] = cum[tm - 1:tm, :]
    cum_ref[0] = cum
    cumt_ref[0] = _tr(cum)[0:8, :]


def _prep(proj3, gfq, gfk, gmq, gmk, bf_full, *, tm):
    nb, ln, _ = proj3.shape

    def col(c):
        return pl.BlockSpec((1, tm, GROUP), lambda b, i: (b, i, c))

    row = pl.BlockSpec((1, GROUP), lambda b, i: (0, 0))
    rows_blk = pl.BlockSpec((1, tm, GROUP), lambda b, i: (b, i, 0))
    t_blk = pl.BlockSpec((1, HEADS, HD, tm), lambda b, i: (b, 0, 0, i))
    rows_shape = jax.ShapeDtypeStruct((nb, ln, GROUP), F32)
    t_shape = jax.ShapeDtypeStruct((nb, HEADS, HD, ln), F32)
    return pl.pallas_call(
        _prep_kernel, grid=(nb, ln // tm),
        in_specs=[col(COL_FQ), col(COL_FK), col(COL_FV), col(COL_MQ), col(COL_MK), col(COL_MV),
                  pl.BlockSpec((1, tm, LANES), lambda b, i: (b, i, COL_SMALL)),
                  row, row, row, row, pl.BlockSpec((1, LANES), lambda b, i: (0, 0))],
        out_specs=[rows_blk, t_blk, t_blk, rows_blk, t_blk, t_blk,
                   pl.BlockSpec((1, HEADS, tm), lambda b, i: (b, 0, i)),
                   pl.BlockSpec((1, tm, LANES), lambda b, i: (b, i, 0)),
                   pl.BlockSpec((1, 8, tm), lambda b, i: (b, 0, i)),
                   pl.BlockSpec((1, 1, 1, GROUP), lambda b, i: (b, i, 0, 0))],
        out_shape=[rows_shape, t_shape, t_shape, rows_shape, t_shape, t_shape,
                   jax.ShapeDtypeStruct((nb, HEADS, ln), F32), jax.ShapeDtypeStruct((nb, ln, LANES), F32),
                   jax.ShapeDtypeStruct((nb, 8, ln), F32), jax.ShapeDtypeStruct((nb, ln // tm, 1, GROUP), F32)],
        scratch_shapes=[pltpu.VMEM((1, LANES), F32)],
        compiler_params=_cparams(("parallel", "arbitrary")), name="attn_prep",
    )(proj3, proj3, proj3, proj3, proj3, proj3, proj3, gfq, gfk, gmq, gmk, bf_full)


def _attn_update(s, vt, h, m_scr, l_scr, acc_scr):
    rep = s.shape[1] // LANES
    m_old = m_scr[h]
    m_new = jnp.maximum(m_old, jnp.max(s, axis=1, keepdims=True))
    p = jnp.exp(s - jnp.concatenate([m_new] * rep, axis=1))
    alpha = jnp.exp(m_old - m_new)
    l_scr[h] = alpha * l_scr[h] + jnp.sum(p, axis=1, keepdims=True)
    acc_scr[h] = alpha[:, 0:HD] * acc_scr[h] + _nt(p, vt)
    m_scr[h] = m_new


def _attn_init(m_scr, l_scr, acc_scr):
    m_scr[...] = jnp.full_like(m_scr, NEG)
    l_scr[...] = jnp.zeros_like(l_scr)
    acc_scr[...] = jnp.zeros_like(acc_scr)


def _attn_finish(o_ref, l_scr, acc_scr):
    for h in range(HEADS):
        o_ref[0, :, h * HD:(h + 1) * HD] = acc_scr[h] / l_scr[h][:, 0:HD]


def _attn_scratch(t):
    return [pltpu.VMEM((HEADS, t, LANES), F32), pltpu.VMEM((HEADS, t, LANES), F32), pltpu.VMEM((HEADS, t, HD), F32)]


def _fox_kernel(q_ref, kt_ref, vt_ref, cq_ref, ck_ref, o_ref, m_scr, l_scr, acc_scr, cqb_scr, *, t, scale):
    qi, ki = pl.program_id(1), pl.program_id(2)
    rep = t // LANES

    @pl.when(ki == 0)
    def _():
        _attn_init(m_scr, l_scr, acc_scr)
        cq = cq_ref[0]
        for h in range(HEADS):
            cqb_scr[h] = jnp.broadcast_to(cq[:, L_FF + h:L_FF + h + 1], (t, LANES))

    def step(diagonal):
        q, ck = q_ref[0], ck_ref[0]
        for h in range(HEADS):
            s = _dot(_bf(q[:, h * HD:(h + 1) * HD]), _bf(kt_ref[0, h])) * scale
            s = s + (jnp.concatenate([cqb_scr[h]] * rep, axis=1) - ck[L_FF + h:L_FF + h + 1, :])
            if diagonal:
                s = jnp.where(_iota((t, t), 1) <= _iota((t, t), 0), s, NEG)
            _attn_update(s, vt_ref[0, h], h, m_scr, l_scr, acc_scr)

    @pl.when(ki < qi)
    def _():
        step(False)

    @pl.when(ki == qi)
    def _():
        step(True)
        _attn_finish(o_ref, l_scr, acc_scr)


def _fox_prompt(q, kt, vt, cum, cumt, *, t):
    b, ln, _ = q.shape
    n = ln // t
    kv = pl.BlockSpec((1, HEADS, HD, t), lambda bb, qi, ki: (bb, 0, 0, jnp.minimum(ki, qi)))
    qs = pl.BlockSpec((1, t, GROUP), lambda bb, qi, ki: (bb, qi, 0))
    return pl.pallas_call(
        functools.partial(_fox_kernel, t=t, scale=HD ** -0.5), grid=(b, n, n),
        in_specs=[qs, kv, kv,
                  pl.BlockSpec((1, t, LANES), lambda bb, qi, ki: (bb, qi, 0)),
                  pl.BlockSpec((1, 8, t), lambda bb, qi, ki: (bb, 0, jnp.minimum(ki, qi)))],
        out_specs=qs, out_shape=jax.ShapeDtypeStruct((b, ln, GROUP), F32),
        scratch_shapes=_attn_scratch(t) + [pltpu.VMEM((HEADS, t, LANES), F32)],
        compiler_params=_cparams(("parallel", "parallel", "arbitrary")), name="fox_prompt",
    )(q, kt, vt, cum, cumt)


def _moba_kernel(q_ref, kt_ref, vt_ref, km_ref, o_ref, m_scr, l_scr, acc_scr, sel_scr, *, scale):
    t = MOBA_BLOCK
    qi, j = pl.program_id(1), pl.program_id(2)
    lane = _iota((t, LANES), 1)

    def rel():
        return (_iota((t, t), 0) - _iota((t, t), 1)).astype(F32)

    @pl.when(j == 0)
    def _():
        _attn_init(m_scr, l_scr, acc_scr)
        q, km = q_ref[0], km_ref[0]
        lane_f = lane.astype(F32)
        r = rel()
        for h in range(HEADS):
            sl = slice(h * HD, (h + 1) * HD)
            gate = jnp.where(lane < qi, _nt3(q[:, sl], km[:, sl]), NEG)
            sel = jnp.zeros((t, LANES), F32)
            for k in range(MOBA_TOPK):
                mx = jnp.max(gate, axis=1, keepdims=True)
                idx = jnp.min(jnp.where(gate == mx, lane_f, float(LANES)), axis=1, keepdims=True)
                hit = lane_f == idx
                sel = jnp.where(hit, jnp.maximum(sel, jnp.where(qi > k, 1.0, 0.0)), sel)
                gate = jnp.where(hit, NEG, gate)
            sel_scr[h] = sel
            s = _dot(_bf(q[:, sl]), _bf(kt_ref[0, h])) * scale - (2.0 ** (-2 * (h + 1))) * r
            _attn_update(jnp.where(r >= 0.0, s, NEG), vt_ref[0, h], h, m_scr, l_scr, acc_scr)

    @pl.when(jnp.logical_and(j >= 1, j <= qi))
    def _():
        q = q_ref[0]
        n = j - 1
        dist = rel() + ((qi - n) * t).astype(F32)
        for h in range(HEADS):
            sl = slice(h * HD, (h + 1) * HD)
            picked = jnp.sum(jnp.where(lane == n, sel_scr[h], 0.0), axis=1, keepdims=True)
            s = _dot(_bf(q[:, sl]), _bf(kt_ref[0, h])) * scale - (2.0 ** (-2 * (h + 1))) * dist
            _attn_update(jnp.where(picked > 0.5, s, NEG), vt_ref[0, h], h, m_scr, l_scr, acc_scr)

    @pl.when(j == qi)
    def _():
        _attn_finish(o_ref, l_scr, acc_scr)


def _moba_prompt(q, kt, vt, kmean_pad):
    b, ln, _ = q.shape
    t = MOBA_BLOCK
    n = ln // t

    def kv_idx(bb, qi, j):
        return (bb, 0, 0, jnp.where(j == 0, qi, jnp.minimum(j - 1, jnp.maximum(qi - 1, 0))))

    qs = pl.BlockSpec((1, t, GROUP), lambda bb, qi, j: (bb, qi, 0))
    kv = pl.BlockSpec((1, HEADS, HD, t), kv_idx)
    return pl.pallas_call(
        functools.partial(_moba_kernel, scale=HD ** -0.5), grid=(b, n, n),
        in_specs=[qs, kv, kv, pl.BlockSpec((1, LANES, GROUP), lambda bb, qi, j: (bb, 0, 0))],
        out_specs=qs, out_shape=jax.ShapeDtypeStruct((b, ln, GROUP), F32),
        scratch_shapes=_attn_scratch(t) + [pltpu.VMEM((HEADS, t, LANES), F32)],
        compiler_params=_cparams(("parallel", "parallel", "arbitrary")), name="moba_prompt",
    )(q, kt, vt, kmean_pad)


def _query_cols(q):
    return _tr(jnp.broadcast_to(q, (LANES, GROUP)))


def _head_dots(a, b):
    ind = _bf(lax.shift_right_logical(_iota((GROUP, LANES), 0), 6) == _iota((GROUP, LANES), 1))
    return _mmx01(jnp.broadcast_to(a * b, (8, GROUP)), ind)[0:1, :]


def _lane_sums(acc):
    a1, a2, a3 = _split3(acc)
    ones = jnp.ones((8, LANES), BF16)
    return (_dg(ones, a1, 1, 1) + (_dg(ones, a2, 1, 1) + _dg(ones, a3, 1, 1)))[0:1, :]


def _decode_head(srows, s_self, v_refs, h):
    mrow = srows[0]
    for r in srows[1:]:
        mrow = jnp.maximum(mrow, r)
    m = jnp.maximum(jnp.max(mrow, axis=1, keepdims=True), s_self)
    acc = jnp.zeros((HD, LANES), F32)
    lrow = jnp.zeros((1, LANES), F32)
    for j, r in enumerate(srows):
        p = jnp.exp(r - m)
        lrow = lrow + p
        acc = acc + p * v_refs[j][0, 0, h]
    w_self = jnp.exp(s_self - m)
    return acc, w_self, jnp.sum(lrow, axis=1, keepdims=True) + w_self


def _fox_dec_kernel(pt_ref, q_ref, kn_ref, vn_ref, lfn_ref, *rest, n_pages, scale):
    del pt_ref
    n = n_pages
    k_refs, v_refs, lf_refs = rest[:n], rest[n:2 * n], rest[2 * n:3 * n]
    o_ref, qb_scr, lf_scr, acc_scr = rest[3 * n:]
    q = q_ref[0]
    qb_scr[...] = _query_cols(q)
    s_new = _head_dots(q, kn_ref[0]) * scale
    lfn = lfn_ref[0]
    for j in range(n):
        row = lf_refs[j][0, 0]
        for h in range(HEADS):
            lf_scr[h * n + j:h * n + j + 1, :] = row[:, h * PAGE:(h + 1) * PAGE]
    lf_all = lf_scr[...]
    later = _bf(_iota((PAGE, PAGE), 0) > _iota((PAGE, PAGE), 1))
    suffix = _mmx01(lf_all, later)
    page_sum = jnp.sum(lf_all, axis=1, keepdims=True)
    grp = lax.shift_right_logical(_iota((1, GROUP), 1), 6)
    w_row = jnp.zeros((1, GROUP), F32)
    l_row = jnp.ones((1, GROUP), F32)
    for h in range(HEADS):
        carry = lfn[:, h:h + 1]
        offs = [None] * n
        for j in reversed(range(n)):
            offs[j] = carry
            carry = carry + page_sum[h * n + j:h * n + j + 1, :]
        qh = qb_scr[h * HD:(h + 1) * HD, :]
        srows = [jnp.sum(qh * k_refs[j][0, 0, h], axis=0, keepdims=True) * scale
                 + (suffix[h * n + j:h * n + j + 1, :] + offs[j]) for j in range(n)]
        acc, w_self, l = _decode_head(srows, s_new[:, h:h + 1], v_refs, h)
        acc_scr[h * HD:(h + 1) * HD, :] = acc
        w_row = jnp.where(grp == h, w_self, w_row)
        l_row = jnp.where(grp == h, l, l_row)
    o_ref[0] = (_lane_sums(acc_scr[...]) + w_row * vn_ref[0]) / l_row


def _fox_decode(pt_flat, layer, q, kn, proj_s3, lfn, kc, vc, lfc, *, n_pages):
    bd = q.shape[0]
    row = pl.BlockSpec((1, 1, GROUP), lambda b, pt: (b, 0, 0))

    def page(j, tail):
        return lambda b, pt: (layer, pt[b * n_pages + j]) + tail

    kv_specs = [pl.BlockSpec((1, 1, HEADS, HD, PAGE), page(j, (0, 0, 0))) for j in range(n_pages)]
    lf_specs = [pl.BlockSpec((1, 1, 1, HEADS * PAGE), page(j, (0, 0))) for j in range(n_pages)]
    return pl.pallas_call(
        functools.partial(_fox_dec_kernel, n_pages=n_pages, scale=HD ** -0.5),
        grid_spec=pltpu.PrefetchScalarGridSpec(
            num_scalar_prefetch=1, grid=(bd,),
            in_specs=[row, row, pl.BlockSpec((1, 1, GROUP), lambda b, pt: (b, 0, COL_FV)),
                      pl.BlockSpec((1, 1, HEADS), lambda b, pt: (b, 0, 0))] + kv_specs + kv_specs + lf_specs,
            out_specs=row,
            scratch_shapes=[pltpu.VMEM((GROUP, LANES), F32), pltpu.VMEM((HEADS * n_pages, PAGE), F32),
                            pltpu.VMEM((GROUP, LANES), F32)]),
        out_shape=jax.ShapeDtypeStruct((bd, 1, GROUP), F32),
        compiler_params=_cparams(("parallel",)), name="fox_decode",
    )(pt_flat, q, kn, proj_s3, lfn, *([kc] * n_pages), *([vc] * n_pages), *([lfc] * n_pages))


def _moba_dec_kernel(pt_ref, q_ref, kn_ref, vn_ref, *rest, n_pages, scale):
    del pt_ref
    n = n_pages
    ppb = MOBA_BLOCK // PAGE
    n_past = n // ppb
    past_len = n * PAGE
    k_refs, v_refs = rest[:n], rest[n:2 * n]
    o_ref, qb_scr, acc_scr = rest[2 * n:]
    q = q_ref[0]
    qb_scr[...] = _query_cols(q)
    s_own = _head_dots(q, kn_ref[0]) * scale
    rows8, lanes8 = _iota((8, LANES), 0), _iota((8, LANES), 1)
    raws = []
    gates = jnp.full((8, LANES), NEG, F32)
    for h in range(HEADS):
        qh = qb_scr[h * HD:(h + 1) * HD, :]
        raw_h = [jnp.sum(qh * k_refs[j][0, 0, h], axis=0, keepdims=True) for j in range(n)]
        raws.append(raw_h)
        for blk in range(n_past):
            tot = raw_h[ppb * blk]
            for j in range(ppb * blk + 1, ppb * (blk + 1)):
                tot = tot + raw_h[j]
            gate = jnp.sum(tot, axis=1, keepdims=True) * (1.0 / MOBA_BLOCK)
            gates = jnp.where(jnp.logical_and(rows8 == h, lanes8 == blk), gate, gates)
    lanes_f = lanes8.astype(F32)
    sel = jnp.zeros((8, LANES), F32)
    for _ in range(min(MOBA_TOPK, n_past)):
        mx = jnp.max(gates, axis=1, keepdims=True)
        idx = jnp.min(jnp.where(gates == mx, lanes_f, float(LANES)), axis=1, keepdims=True)
        hit = lanes_f == idx
        sel = jnp.where(hit, 1.0, sel)
        gates = jnp.where(hit, NEG, gates)
    lane_f = _iota((1, LANES), 1).astype(F32)
    grp = lax.shift_right_logical(_iota((1, GROUP), 1), 6)
    w_row = jnp.zeros((1, GROUP), F32)
    l_row = jnp.ones((1, GROUP), F32)
    for h in range(HEADS):
        slope = 2.0 ** (-2 * (h + 1))
        srows = []
        for j in range(n):
            keep = sel[h:h + 1, j // ppb:j // ppb + 1]
            s = raws[h][j] * scale - slope * (float(past_len - j * PAGE) - lane_f)
            srows.append(s * keep + (1.0 - keep) * NEG)
        acc, w_self, l = _decode_head(srows, s_own[:, h:h + 1], v_refs, h)
        acc_scr[h * HD:(h + 1) * HD, :] = acc
        w_row = jnp.where(grp == h, w_self, w_row)
        l_row = jnp.where(grp == h, l, l_row)
    o_ref[0] = (_lane_sums(acc_scr[...]) + w_row * vn_ref[0]) / l_row


def _moba_decode(pt_flat, layer, q, kn, proj_s3, kc, vc, *, n_pages):
    bd = q.shape[0]
    assert (n_pages * PAGE) % MOBA_BLOCK == 0 and n_pages * PAGE // MOBA_BLOCK <= LANES
    row = pl.BlockSpec((1, 1, GROUP), lambda b, pt: (b, 0, 0))

    def page(j):
        return lambda b, pt: (layer, pt[b * n_pages + j], 0, 0, 0)

    kv_specs = [pl.BlockSpec((1, 1, HEADS, HD, PAGE), page(j)) for j in range(n_pages)]
    return pl.pallas_call(
        functools.partial(_moba_dec_kernel, n_pages=n_pages, scale=HD ** -0.5),
        grid_spec=pltpu.PrefetchScalarGridSpec(
            num_scalar_prefetch=1, grid=(bd,),
            in_specs=[row, row, pl.BlockSpec((1, 1, GROUP), lambda b, pt: (b, 0, COL_MV))] + kv_specs + kv_specs,
            out_specs=row,
            scratch_shapes=[pltpu.VMEM((GROUP, LANES), F32), pltpu.VMEM((GROUP, LANES), F32)]),
        out_shape=jax.ShapeDtypeStruct((bd, 1, GROUP), F32),
        compiler_params=_cparams(("parallel",)), name="moba_decode",
    )(pt_flat, q, kn, proj_s3, *([kc] * n_pages), *([vc] * n_pages))


def _chunk_conv(u, cw, ext_scr, c_len):
    ext_scr[8:8 + c_len, :] = u
    out = (cw[3:4, :] * u + cw[2:3, :] * ext_scr[7:7 + c_len, :]
           + cw[1:2, :] * ext_scr[6:6 + c_len, :] + cw[0:1, :] * ext_scr[5:5 + c_len, :])
    ext_scr[0:8, :] = ext_scr[c_len:c_len + 8, :]
    return out


def _ssd_kernel(xbc_ref, z_ref, sm_ref, cw_ref, cb_ref, dtb_ref, alog_ref, d_ref, ng_ref,
                y_ref, hs_ref, cs_ref, ext_scr, h_scr, y_scr, *, c_len):
    c, nc = pl.program_id(1), pl.num_programs(1)

    @pl.when(c == 0)
    def _():
        ext_scr[0:8, :] = jnp.zeros((8, XBC_W), F32)
        h_scr[...] = jnp.zeros_like(h_scr)

    u = xbc_ref[0]
    act = _silu(_chunk_conv(u, cw_ref[...], ext_scr, c_len) + cb_ref[...])

    @pl.when(c == nc - 1)
    def _():
        cs_ref[0] = u[c_len - (CONV_W - 1):c_len, :]

    xa, bm, cm = act[:, 0:GROUP], act[:, GROUP:GROUP + 2 * HD], act[:, GROUP + 2 * HD:]
    dt = _softplus(sm_ref[0] + dtb_ref[...])
    a_neg = -jnp.exp(alog_ref[...])
    row, colm = _iota((c_len, c_len), 0), _iota((c_len, c_len), 1)
    causal = row >= colm
    gam = _mm01(_bf(causal), dt * a_neg)
    gam_t, dt_t = _tr(gam), _tr(dt)
    glast = gam[c_len - 1:c_len, :]
    eg = jnp.exp(gam)
    wst = jnp.exp(glast - gam) * dt
    elast = jnp.exp(glast)
    dvec = d_ref[...]
    for g in range(2):
        cg, bg = cm[:, g * HD:(g + 1) * HD], bm[:, g * HD:(g + 1) * HD]
        cb = _nt(cg, bg)
        for h in (2 * g, 2 * g + 1):
            sl = slice(h * HD, (h + 1) * HD)
            hl = slice(L_DT + h, L_DT + h + 1)
            dec = jnp.exp(jnp.where(causal, gam[:, hl] - gam_t[hl, :], NEG))
            xh = xa[:, sl]
            hprev = h_scr[h]
            y_scr[:, sl] = (_mm(cb * dec * dt_t[hl, :], xh) + _nt(cg, hprev) * eg[:, hl] + dvec[:, hl] * xh)
            h_scr[h] = elast[:, hl] * hprev + _tn3(xh * wst[:, hl], bg)
    yg = y_scr[...] * _silu(z_ref[0])
    y_ref[0] = yg * lax.rsqrt(jnp.mean(yg * yg, axis=-1, keepdims=True) + EPS) * ng_ref[...]

    @pl.when(c == nc - 1)
    def _():
        hs_ref[0] = h_scr[...]


def _ssd_prompt(proj3, cw, cb, dtb, alog, dvec, ng):
    b, ln, _ = proj3.shape
    c_len = SSD_CHUNK
    vec = lambda w: pl.BlockSpec((1, w), lambda bb, c: (0, 0))
    return pl.pallas_call(
        functools.partial(_ssd_kernel, c_len=c_len), grid=(b, ln // c_len),
        in_specs=[pl.BlockSpec((1, c_len, XBC_W), lambda bb, c: (bb, c, COL_XBC)),
                  pl.BlockSpec((1, c_len, GROUP), lambda bb, c: (bb, c, COL_ZA)),
                  pl.BlockSpec((1, c_len, LANES), lambda bb, c: (bb, c, COL_SMALL)),
                  pl.BlockSpec((CONV_W, XBC_W), lambda bb, c: (0, 0)), vec(XBC_W), vec(LANES), vec(LANES), vec(LANES),
                  vec(GROUP)],
        out_specs=[pl.BlockSpec((1, c_len, GROUP), lambda bb, c: (bb, c, 0)),
                   pl.BlockSpec((1, HEADS, HD, HD), lambda bb, c: (bb, 0, 0, 0)),
                   pl.BlockSpec((1, CONV_W - 1, XBC_W), lambda bb, c: (bb, 0, 0))],
        out_shape=[jax.ShapeDtypeStruct((b, ln, GROUP), F32), jax.ShapeDtypeStruct((b, HEADS, HD, HD), F32),
                   jax.ShapeDtypeStruct((b, CONV_W - 1, XBC_W), F32)],
        scratch_shapes=[pltpu.VMEM((c_len + 8, XBC_W), F32), pltpu.VMEM((HEADS, HD, HD), F32),
                        pltpu.VMEM((c_len, GROUP), F32)],
        compiler_params=_cparams(("parallel", "arbitrary")), name="ssd_prompt",
    )(proj3, proj3, proj3, cw, cb, dtb, alog, dvec, ng)


def _gdn_kernel(qkv_ref, z_ref, sm_ref, cw_ref, alog_ref, dtb_ref, ng_ref,
                y_ref, ss_ref, cs_ref, ext_scr, s_scr, *, tok, c_len):
    c, nc = pl.program_id(1), pl.num_programs(1)

    @pl.when(c == 0)
    def _():
        ext_scr[0:8, :] = jnp.zeros((8, QKV_W), F32)
        s_scr[...] = jnp.zeros_like(s_scr)

    u = qkv_ref[0]
    act = _silu(_chunk_conv(u, cw_ref[...], ext_scr, tok))

    @pl.when(c == nc - 1)
    def _():
        cs_ref[0] = u[tok - (CONV_W - 1):tok, :]

    hm = _head_mat()
    q, k, v = act[:, 0:GROUP], act[:, GROUP:2 * GROUP], act[:, 2 * GROUP:]
    qn = q * lax.rsqrt(_headsum(q * q, hm) + EPS) * (HD ** -0.5)
    kn = k * lax.rsqrt(_headsum(k * k, hm) + EPS)
    sm = sm_ref[0]
    beta = jax.nn.sigmoid(sm)
    gl = -jnp.exp(alog_ref[...]) * _softplus(sm + dtb_ref[...])
    shift = c_len.bit_length() - 1
    ri, ci = _iota((tok, tok), 0), _iota((tok, tok), 1)
    same_chunk = lax.shift_right_logical(ri, shift) == lax.shift_right_logical(ci, shift)
    gam = _mm01(_bf(jnp.logical_and(ri >= ci, same_chunk)), gl)
    gam_t = _tr(gam)
    eg = jnp.exp(gam)
    row, colm = _iota((c_len, c_len), 0), _iota((c_len, c_len), 1)
    incl = row >= colm
    strict = row > colm
    eye = (row == colm).astype(F32)
    z, ng = z_ref[0], ng_ref[...]
    units = [(i, h) for i in range(tok // c_len) for h in range(HEADS)]
    n_fac = c_len.bit_length() - 2

    dec, kh, qh, bcol, pinv, mpow = {}, {}, {}, {}, {}, {}
    for un in units:
        i, h = un
        r = slice(i * c_len, (i + 1) * c_len)
        g = L_GA + h
        dec[un] = jnp.exp(jnp.where(incl, gam[r, g:g + 1] - gam_t[g:g + 1, r], NEG))
        kh[un], qh[un] = kn[r, h * HD:(h + 1) * HD], qn[r, h * HD:(h + 1) * HD]
        bcol[un] = beta[r, L_BETA + h:L_BETA + h + 1]
        nmat = bcol[un] * _nt3(kh[un], kh[un]) * jnp.where(strict, dec[un], 0.0)
        pinv[un] = eye - nmat
        mpow[un] = _mm3(nmat, nmat)
    for lvl in range(n_fac):
        for un in units:
            pinv[un] = pinv[un] + _mm3(pinv[un], mpow[un])
        if lvl + 1 < n_fac:
            for un in units:
                mpow[un] = _mm3(mpow[un], mpow[un])
    uu, wk, qkd = {}, {}, {}
    for un in units:
        i, h = un
        r = slice(i * c_len, (i + 1) * c_len)
        g = L_GA + h
        uu[un] = _mm3(pinv[un], bcol[un] * v[r, h * HD:(h + 1) * HD])
        wk[un] = _mm3(pinv[un], (bcol[un] * eg[r, g:g + 1]) * kh[un])
        qkd[un] = _nt(qh[un], kh[un]) * dec[un]

    for h in range(HEADS):
        sl = slice(h * HD, (h + 1) * HD)
        g = L_GA + h
        s_run = s_scr[h]
        for i in range(tok // c_len):
            un = (i, h)
            r = slice(i * c_len, (i + 1) * c_len)
            glast = gam[(i + 1) * c_len - 1:(i + 1) * c_len, g:g + 1]
            w = uu[un] - _mm3(wk[un], s_run)
            o = _mm(qh[un] * eg[r, g:g + 1], s_run) + _mm(qkd[un], w)
            s_run = jnp.exp(glast) * s_run + _tn3(kh[un] * jnp.exp(glast - gam[r, g:g + 1]), w)
            on = o * lax.rsqrt(jnp.mean(o * o, axis=-1, keepdims=True) + EPS) * ng[:, sl]
            y_ref[0, r, sl] = on * _silu(z[r, sl])
        s_scr[h] = s_run

    @pl.when(c == nc - 1)
    def _():
        ss_ref[0] = s_scr[...]


def _gdn_prompt(proj3, cw, alog, dtb, ng):
    b, ln, _ = proj3.shape
    tok = GDN_TOK
    vec = lambda w: pl.BlockSpec((1, w), lambda bb, c: (0, 0))
    return pl.pallas_call(
        functools.partial(_gdn_kernel, tok=tok, c_len=GDN_CHUNK), grid=(b, ln // tok),
        in_specs=[pl.BlockSpec((1, tok, QKV_W), lambda bb, c: (bb, c, COL_QKV)),
                  pl.BlockSpec((1, tok, GROUP), lambda bb, c: (bb, c, COL_ZC)),
                  pl.BlockSpec((1, tok, LANES), lambda bb, c: (bb, c, COL_SMALL)),
                  pl.BlockSpec((CONV_W, QKV_W), lambda bb, c: (0, 0)), vec(LANES), vec(LANES), vec(GROUP)],
        out_specs=[pl.BlockSpec((1, tok, GROUP), lambda bb, c: (bb, c, 0)),
                   pl.BlockSpec((1, HEADS, HD, HD), lambda bb, c: (bb, 0, 0, 0)),
                   pl.BlockSpec((1, CONV_W - 1, QKV_W), lambda bb, c: (bb, 0, 0))],
        out_shape=[jax.ShapeDtypeStruct((b, ln, GROUP), F32), jax.ShapeDtypeStruct((b, HEADS, HD, HD), F32),
                   jax.ShapeDtypeStruct((b, CONV_W - 1, QKV_W), F32)],
        scratch_shapes=[pltpu.VMEM((tok + 8, QKV_W), F32), pltpu.VMEM((HEADS, HD, HD), F32)],
        compiler_params=_cparams(("parallel", "arbitrary")), name="gdn_prompt",
    )(proj3, proj3, proj3, cw, alog, dtb, ng)


def _sprep_kernel(fq_ref, fk_ref, fv_ref, mq_ref, mk_ref, mv_ref, xbc_ref, qkv_ref, za_ref, zc_ref, sm_ref,
                  sbuf_ref, gbuf_ref, gfq_ref, gfk_ref, gmq_ref, gmk_ref, bf_ref, scw_ref, scb_ref, gcw_ref,
                  fqn_ref, fkn_ref, mqn_ref, mkn_ref, lf_ref, fkt_ref, fvt_ref, mkt_ref, mvt_ref, lft_ref,
                  xbct_ref, zat_ref, smt_ref, sbo_ref, qkvt_ref, zct_ref, gbo_ref):
    hm = _head_mat()

    def hnorm(x, g):
        return x * lax.rsqrt(_headsum(x * x, hm) * (1.0 / HD) + EPS) * g

    fqn_ref[...] = hnorm(fq_ref[...], gfq_ref[...])
    fkn = hnorm(fk_ref[...], gfk_ref[...])
    fkn_ref[...] = fkn
    fkt_ref[...] = _tr(fkn)
    fvt_ref[...] = _tr(fv_ref[...])
    mqn_ref[...] = hnorm(mq_ref[...], gmq_ref[...])
    mkn = hnorm(mk_ref[...], gmk_ref[...])
    mkn_ref[...] = mkn
    mkt_ref[...] = _tr(mkn)
    mvt_ref[...] = _tr(mv_ref[...])
    sm = sm_ref[...]
    lf = _log_sigmoid(sm + bf_ref[...])
    lf_ref[...] = lf[:, L_FF:L_FF + HEADS]
    lft_ref[...] = _tr(lf)[L_FF:L_FF + HEADS, :]
    smt_ref[...] = _tr(sm)
    zat_ref[...] = _tr(za_ref[...])
    zct_ref[...] = _tr(zc_ref[...])

    def conv(u, buf_ref, cw):
        return cw[3:4, :] * u + cw[2:3, :] * buf_ref[2] + cw[1:2, :] * buf_ref[1] + cw[0:1, :] * buf_ref[0]

    def roll_buf(out_ref, buf_ref, u):
        out_ref[0] = buf_ref[1]
        out_ref[1] = buf_ref[2]
        out_ref[2] = u

    u = xbc_ref[...]
    xbct_ref[...] = _tr(_silu(conv(u, sbuf_ref, scw_ref[...]) + scb_ref[...]))
    roll_buf(sbo_ref, sbuf_ref, u)
    ug = qkv_ref[...]
    act = _silu(conv(ug, gbuf_ref, gcw_ref[...]))
    roll_buf(gbo_ref, gbuf_ref, ug)
    q, k = act[:, 0:GROUP], act[:, GROUP:2 * GROUP]
    qkvt_ref[0:GROUP, :] = _tr(q * lax.rsqrt(_headsum(q * q, hm) + EPS) * (HD ** -0.5))
    qkvt_ref[GROUP:2 * GROUP, :] = _tr(k * lax.rsqrt(_headsum(k * k, hm) + EPS))
    qkvt_ref[2 * GROUP:, :] = _tr(act[:, 2 * GROUP:])


def _sprep(proj_s, sbuf, gbuf, gfq, gfk, gmq, gmk, bf_full, scw, scb, gcw):
    bd = proj_s.shape[0]
    col = lambda c, w=GROUP: pl.BlockSpec((bd, w), lambda i: (0, c))
    full = lambda a: pl.BlockSpec(a.shape, lambda i: (0,) * a.ndim)
    sds = lambda *s: jax.ShapeDtypeStruct(s, F32)
    out_shape = [sds(bd, GROUP)] * 4 + [sds(bd, HEADS)] + [sds(GROUP, bd)] * 4 + [sds(HEADS, bd),
                 sds(XBC_W, bd), sds(GROUP, bd), sds(LANES, bd), sds(CONV_W - 1, bd, XBC_W),
                 sds(QKV_W, bd), sds(GROUP, bd), sds(CONV_W - 1, bd, QKV_W)]
    return pl.pallas_call(
        _sprep_kernel, grid=(1,),
        in_specs=[col(COL_FQ), col(COL_FK), col(COL_FV), col(COL_MQ), col(COL_MK), col(COL_MV),
                  col(COL_XBC, XBC_W), col(COL_QKV, QKV_W), col(COL_ZA), col(COL_ZC), col(COL_SMALL, LANES),
                  full(sbuf), full(gbuf), full(gfq), full(gfk), full(gmq), full(gmk), full(bf_full),
                  full(scw), full(scb), full(gcw)],
        out_specs=[pl.BlockSpec(s.shape, lambda i, n=len(s.shape): (0,) * n) for s in out_shape],
        out_shape=out_shape,
        compiler_params=_cparams(("arbitrary",)), name="sample_prep",
    )(proj_s, proj_s, proj_s, proj_s, proj_s, proj_s, proj_s, proj_s, proj_s, proj_s, proj_s,
      sbuf, gbuf, gfq, gfk, gmq, gmk, bf_full, scw, scb, gcw)


def _ssd_step_kernel(x_ref, b_ref, c_ref, z_ref, dtr_ref, dtb_ref, alog_ref, d_ref, st_ref, y_ref, so_ref, y_scr):
    dt = _softplus(dtr_ref[0] + dtb_ref[0])
    dec = jnp.exp(dt * (-jnp.exp(alog_ref[0])))
    xt, bt, ct = x_ref[...], b_ref[...], c_ref[...]
    xdt = xt * dt
    for p in range(HD):
        s_new = dec * st_ref[0, 0, p] + xdt[p:p + 1, :] * bt
        so_ref[0, p] = s_new
        y_scr[p:p + 1, :] = jnp.sum(s_new * ct, axis=0, keepdims=True)
    y_ref[...] = (y_scr[...] + d_ref[0] * xt) * _silu(z_ref[...])


def _ssd_step(layer, xbct, zat, smt3, state_t, dtb, alog, dvec):
    bd = xbct.shape[1]
    tile = lambda f: pl.BlockSpec((HD, bd), f)
    par = pl.BlockSpec((1, 1, bd), lambda h: (h, 0, 0))
    return pl.pallas_call(
        _ssd_step_kernel, grid=(HEADS,),
        in_specs=[tile(lambda h: (h, 0)), tile(lambda h: (HEADS + h // 2, 0)), tile(lambda h: (HEADS + 2 + h // 2, 0)),
                  tile(lambda h: (h, 0)), pl.BlockSpec((1, 1, bd), lambda h: (L_DT + h, 0, 0)), par, par, par,
                  pl.BlockSpec((1, 1, HD, HD, bd), lambda h: (layer, h, 0, 0, 0))],
        out_specs=[tile(lambda h: (h, 0)), pl.BlockSpec((1, HD, HD, bd), lambda h: (h, 0, 0, 0))],
        out_shape=[jax.ShapeDtypeStruct((GROUP, bd), F32), jax.ShapeDtypeStruct((HEADS, HD, HD, bd), F32)],
        scratch_shapes=[pltpu.VMEM((HD, bd), F32)],
        compiler_params=_cparams(("parallel",)), name="ssd_step",
    )(xbct, xbct, xbct, zat, smt3, dtb, alog, dvec, state_t)


def _gdn_step_kernel(q_ref, k_ref, v_ref, z_ref, br_ref, ar_ref, alog_ref, dtb_ref, ng_ref, st_ref, y_ref, so_ref):
    eg = jnp.exp(-jnp.exp(alog_ref[0]) * _softplus(ar_ref[0] + dtb_ref[0]))
    beta = jax.nn.sigmoid(br_ref[0])
    qt, kt, vt = q_ref[...], k_ref[...], v_ref[...]
    ks = jnp.zeros_like(vt)
    qs = jnp.zeros_like(vt)
    for dk in range(HD):
        s_old = st_ref[0, 0, dk]
        ks = ks + kt[dk:dk + 1, :] * s_old
        qs = qs + qt[dk:dk + 1, :] * s_old
    w = beta * (vt - eg * ks)
    o = eg * qs + jnp.sum(qt * kt, axis=0, keepdims=True) * w
    for dk in range(HD):
        so_ref[0, dk] = eg * st_ref[0, 0, dk] + kt[dk:dk + 1, :] * w
    on = o * lax.rsqrt(jnp.mean(o * o, axis=0, keepdims=True) + EPS) * ng_ref[...]
    y_ref[...] = on * _silu(z_ref[...])


def _gdn_step(layer, qkvt, zct, smt3, state_t, alog, dtb, ng_b):
    bd = qkvt.shape[1]
    tile = lambda f: pl.BlockSpec((HD, bd), f)
    par = pl.BlockSpec((1, 1, bd), lambda h: (h, 0, 0))
    return pl.pallas_call(
        _gdn_step_kernel, grid=(HEADS,),
        in_specs=[tile(lambda h: (h, 0)), tile(lambda h: (HEADS + h, 0)), tile(lambda h: (2 * HEADS + h, 0)),
                  tile(lambda h: (h, 0)), pl.BlockSpec((1, 1, bd), lambda h: (L_BETA + h, 0, 0)),
                  pl.BlockSpec((1, 1, bd), lambda h: (L_GA + h, 0, 0)), par, par, tile(lambda h: (0, 0)),
                  pl.BlockSpec((1, 1, HD, HD, bd), lambda h: (layer, h, 0, 0, 0))],
        out_specs=[tile(lambda h: (h, 0)), pl.BlockSpec((1, HD, HD, bd), lambda h: (h, 0, 0, 0))],
        out_shape=[jax.ShapeDtypeStruct((GROUP, bd), F32), jax.ShapeDtypeStruct((HEADS, HD, HD, bd), F32)],
        compiler_params=_cparams(("parallel",)), name="gdn_step",
    )(qkvt, qkvt, qkvt, zct, smt3, smt3, alog, dtb, ng_b, state_t)


def _outproj_kernel(ya_ref, yb_ref, yc_ref, yd_ref, w_ref, x_ref, gt_ref, o_ref):
    acc = _dot(_bf(ya_ref[...]), w_ref[0:GROUP, :])
    acc += _dot(_bf(yb_ref[...]), w_ref[GROUP:2 * GROUP, :])
    acc += _dot(_bf(yc_ref[...]), w_ref[2 * GROUP:3 * GROUP, :])
    acc += _dot(_bf(yd_ref[...]), w_ref[3 * GROUP:4 * GROUP, :])
    o_ref[...] = x_ref[...] + gt_ref[0] * acc


def _out_proj(ya, yb, yc, yd, w_bf, x2d, mod, *, tm, per_row, rows_per_batch):
    t, d = x2d.shape
    yspec = pl.BlockSpec((tm, GROUP), lambda i: (i, 0))
    return pl.pallas_call(
        _outproj_kernel, grid=(t // tm,),
        in_specs=[yspec, yspec, yspec, yspec, pl.BlockSpec((d, d), lambda i: (0, 0)),
                  pl.BlockSpec((tm, d), lambda i: (i, 0)), _mod_spec(per_row, tm, rows_per_batch, 2)],
        out_specs=pl.BlockSpec((tm, d), lambda i: (i, 0)),
        out_shape=jax.ShapeDtypeStruct((t, d), F32),
        compiler_params=_cparams(("parallel",)), name="out_proj",
    )(ya, yb, yc, yd, w_bf, x2d, mod)


def _outproj_s_kernel(yat_ref, ng_ref, yb_ref, yct_ref, yd_ref, w_ref, x_ref, gt_ref, o_ref):
    ya = yat_ref[...]
    ya = ya * lax.rsqrt(jnp.mean(ya * ya, axis=0, keepdims=True) + EPS) * ng_ref[...]
    acc = _dg(_bf(ya), w_ref[0:GROUP, :], 0, 0)
    acc += _dot(_bf(yb_ref[...]), w_ref[GROUP:2 * GROUP, :])
    acc += _dg(_bf(yct_ref[...]), w_ref[2 * GROUP:3 * GROUP, :], 0, 0)
    acc += _dot(_bf(yd_ref[...]), w_ref[3 * GROUP:4 * GROUP, :])
    o_ref[...] = x_ref[...] + gt_ref[0] * acc


def _out_proj_s(yat, ng_b, yb, yct, yd, w_bf, x2d, mod):
    bd, d = x2d.shape
    full = lambda a: pl.BlockSpec(a.shape, lambda i: (0,) * a.ndim)
    return pl.pallas_call(
        _outproj_s_kernel, grid=(1,),
        in_specs=[full(yat), full(ng_b), full(yb), full(yct), full(yd), full(w_bf), full(x2d),
                  _mod_spec(True, bd, 1, 2)],
        out_specs=pl.BlockSpec((bd, d), lambda i: (0, 0)),
        out_shape=jax.ShapeDtypeStruct((bd, d), F32),
        compiler_params=_cparams(("arbitrary",)), name="out_proj_sample",
    )(yat, ng_b, yb, yct, yd, w_bf, x2d, mod)


def _ffn_kernel(x_ref, g_ref, sc_ref, sh_ref, gt_ref, wg_ref, wu_ref, wd_ref, o_ref, h_scr, acc_scr):
    f, nf = pl.program_id(1), pl.num_programs(1)

    @pl.when(f == 0)
    def _():
        h_scr[...] = _bf(_normmod(x_ref[...], g_ref[...], sc_ref[0], sh_ref[0]))
        acc_scr[...] = jnp.zeros_like(acc_scr)

    h = h_scr[...]
    a = _silu(_dot(h, wg_ref[...])) * _dot(h, wu_ref[...])
    acc_scr[...] += _dot(_bf(a), wd_ref[...])

    @pl.when(f == nf - 1)
    def _():
        o_ref[...] = x_ref[...] + gt_ref[0] * acc_scr[...]


def _ffn_dense(x2d, g, mod, wg, wu, wd, *, tm, tf, per_row, rows_per_batch):
    t, d = x2d.shape
    ff = wg.shape[1]
    ms = functools.partial(_mod_spec, per_row, tm, rows_per_batch)
    return pl.pallas_call(
        _ffn_kernel, grid=(t // tm, ff // tf),
        in_specs=[pl.BlockSpec((tm, d), lambda i, f: (i, 0)), pl.BlockSpec((1, d), lambda i, f: (0, 0)),
                  ms(4), ms(3), ms(5),
                  pl.BlockSpec((d, tf), lambda i, f: (0, f)), pl.BlockSpec((d, tf), lambda i, f: (0, f)),
                  pl.BlockSpec((tf, d), lambda i, f: (f, 0))],
        out_specs=pl.BlockSpec((tm, d), lambda i, f: (i, 0)),
        out_shape=jax.ShapeDtypeStruct((t, d), F32),
        scratch_shapes=[pltpu.VMEM((tm, d), BF16), pltpu.VMEM((tm, d), F32)],
        compiler_params=_cparams(("parallel", "arbitrary")), name="ffn_dense",
    )(x2d, g, mod, mod, mod, wg, wu, wd)


def _moe_kernel(x_ref, g_ref, sc_ref, sh_ref, gt_ref, r_ref, wg_ref, wu_ref, wd_ref, o_ref,
                h_scr, acc_scr, gate_scr, *, n_experts):
    e, f = pl.program_id(1), pl.program_id(2)
    ne, nf = pl.num_programs(1), pl.num_programs(2)
    tm = x_ref.shape[0]
    lane = _iota((tm, LANES), 1)

    @pl.when(jnp.logical_and(e == 0, f == 0))
    def _():
        hf = _normmod(x_ref[...], g_ref[...], sc_ref[0], sh_ref[0])
        h_scr[...] = _bf(hf)
        acc_scr[...] = jnp.zeros_like(acc_scr)
        lane_f = lane.astype(F32)
        logits = jnp.where(lane < n_experts, _mm3(hf, r_ref[...]), NEG)
        v1 = jnp.max(logits, axis=1, keepdims=True)
        i1 = jnp.min(jnp.where(logits == v1, lane_f, float(LANES)), axis=1, keepdims=True)
        rest = jnp.where(lane_f == i1, NEG, logits)
        v2 = jnp.max(rest, axis=1, keepdims=True)
        i2 = jnp.min(jnp.where(rest == v2, lane_f, float(LANES)), axis=1, keepdims=True)
        e2 = jnp.exp(v2 - v1)
        w1 = 1.0 / (1.0 + e2)
        gate_scr[...] = jnp.where(lane_f == i1, w1, jnp.where(lane_f == i2, e2 * w1, 0.0))

    h = h_scr[...]
    ge = jnp.sum(jnp.where(lane == e, gate_scr[...], 0.0), axis=1, keepdims=True)
    a = _silu(_dot(h, wg_ref[0])) * _dot(h, wu_ref[0]) * ge
    acc_scr[...] += _dot(_bf(a), wd_ref[0])

    @pl.when(jnp.logical_and(e == ne - 1, f == nf - 1))
    def _():
        o_ref[...] = x_ref[...] + gt_ref[0] * acc_scr[...]


def _ffn_moe(x2d, g, mod, router_pad, wg, wu, wd, *, tm, tf, per_row, rows_per_batch):
    t, d = x2d.shape
    ne, _, ff = wg.shape
    ms = functools.partial(_mod_spec, per_row, tm, rows_per_batch)
    return pl.pallas_call(
        functools.partial(_moe_kernel, n_experts=ne), grid=(t // tm, ne, ff // tf),
        in_specs=[pl.BlockSpec((tm, d), lambda i, e, f: (i, 0)), pl.BlockSpec((1, d), lambda i, e, f: (0, 0)),
                  ms(4), ms(3), ms(5), pl.BlockSpec((d, LANES), lambda i, e, f: (0, 0)),
                  pl.BlockSpec((1, d, tf), lambda i, e, f: (e, 0, f)), pl.BlockSpec((1, d, tf), lambda i, e, f: (e, 0, f)),
                  pl.BlockSpec((1, tf, d), lambda i, e, f: (e, f, 0))],
        out_specs=pl.BlockSpec((tm, d), lambda i, e, f: (i, 0)),
        out_shape=jax.ShapeDtypeStruct((t, d), F32),
        scratch_shapes=[pltpu.VMEM((tm, d), BF16), pltpu.VMEM((tm, d), F32), pltpu.VMEM((tm, LANES), F32)],
        compiler_params=_cparams(("parallel", "arbitrary", "arbitrary")), name="ffn_moe",
    )(x2d, g, mod, mod, mod, router_pad, wg, wu, wd)


def _lane_row(vals, offset):
    return jnp.zeros((1, LANES), F32).at[0, offset:offset + vals.shape[0]].set(vals.astype(F32))


def _reorder_w_in(w):
    pad = jnp.zeros((w.shape[0], LANES - 16), w.dtype)
    return jnp.concatenate([w[:, 256:768], w[:, 0:256], w[:, 772:1540], w[:, 1544:2312], w[:, 2320:2576],
                            w[:, 2576:3344], w[:, 768:772], w[:, 1540:1544], w[:, 2312:2320], pad], axis=1)


def _tile_heads(g):
    return jnp.tile(g.astype(F32), HEADS).reshape(1, GROUP)


def _per_head_rows(vals, bd):
    return jnp.broadcast_to(vals.astype(F32).reshape(HEADS, 1, 1), (HEADS, 1, bd))


def kernel(x_prompt, x_sample, cache_fox_k, cache_fox_v, cache_fox_logf, cache_moba_k, cache_moba_v, state_ssm, state_ssm_conv, state_gdn, state_gdn_conv, page_table, c_prompt, c_sample, w_ada, b_ada, norm_mix, norm_ffn, w_in, w_out, ssd_conv_w, ssd_conv_b, ssd_dt_bias, ssd_a_log, ssd_d, ssd_norm, fox_b_f, fox_q_norm, fox_k_norm, gdn_conv_w, gdn_a_log, gdn_dt_bias, gdn_norm, moba_q_norm, moba_k_norm, ffn_w_gate, ffn_w_up, ffn_w_down, moe_router, moe_w_gate, moe_w_up, moe_w_down):
    bp, ln, d = x_prompt.shape
    bd = x_sample.shape[0]
    depth = w_in.shape[0]
    n_pages = page_table.shape[1]
    n_pool = cache_fox_k.shape[1]
    assert x_sample.shape[1] == 1 and d == D_MODEL and ln % MOBA_BLOCK == 0
    tp = bp * ln
    tm_p = 512 if ln % 512 == 0 else 256
    t_attn = 256

    mod = _ada_mod(jnp.concatenate([c_prompt, c_sample], axis=0), w_ada, b_ada)
    pt_flat = page_table.reshape(-1).astype(jnp.int32)
    page_t = lambda c: jnp.transpose(c, (0, 1, 3, 4, 2))
    fox_kc, fox_vc, moba_kc, moba_vc = page_t(cache_fox_k), page_t(cache_fox_v), page_t(cache_moba_k), page_t(cache_moba_v)
    fox_lfc = jnp.transpose(cache_fox_logf, (0, 1, 3, 2)).reshape(depth, n_pool, 1, HEADS * PAGE)
    ssm_t = jnp.transpose(state_ssm, (0, 2, 3, 4, 1))
    gdn_t = jnp.transpose(state_gdn, (0, 2, 3, 4, 1))
    xp = x_prompt.reshape(tp, d)
    xs = x_sample.reshape(bd, d)
    outs_p, outs_s = [], []
    for l in range(depth):
        mod_p = mod[l, :bp].reshape(bp, 1, 6 * d)
        mod_s = mod[l, bp:].reshape(1, bd, 6 * d)
        kw_p = dict(tm=tm_p, per_row=False, rows_per_batch=ln)
        kw_s = dict(tm=bd, per_row=True, rows_per_batch=1)
        g_mix = norm_mix[l].reshape(1, d)
        g_ffn = norm_ffn[l].reshape(1, d)
        w_in_bf = _bf(_reorder_w_in(w_in[l]))
        w_out_bf = _bf(w_out[l])
        gfq, gfk = _tile_heads(fox_q_norm[l]), _tile_heads(fox_k_norm[l])
        gmq, gmk = _tile_heads(moba_q_norm[l]), _tile_heads(moba_k_norm[l])
        bf_full = _lane_row(fox_b_f[l], L_FF)
        scw, scb, gcw = ssd_conv_w[l], ssd_conv_b[l].reshape(1, XBC_W), gdn_conv_w[l]

        proj = _in_proj(xp, g_mix, mod_p, w_in_bf, **kw_p).reshape(bp, ln, PROJ_COLS)
        fqn, fkt, fvt, mqn, mkt, mvt, lft, cum, cumt, kmean = _prep(proj, gfq, gfk, gmq, gmk, bf_full, tm=MOBA_BLOCK)
        y_a, ssm_h, ssm_buf = _ssd_prompt(proj, scw, scb, _lane_row(ssd_dt_bias[l], L_DT), _lane_row(ssd_a_log[l], L_DT),
                                          _lane_row(ssd_d[l], L_DT), ssd_norm[l].reshape(1, GROUP))
        y_c, gdn_s, gdn_buf = _gdn_prompt(proj, gcw, _lane_row(gdn_a_log[l], L_GA), _lane_row(gdn_dt_bias[l], L_GA),
                                          _tile_heads(gdn_norm[l]))
        y_b = _fox_prompt(fqn, fkt, fvt, cum, cumt, t=t_attn)
        kmean_pad = jnp.pad(kmean.reshape(bp, ln // MOBA_BLOCK, GROUP), ((0, 0), (0, LANES - ln // MOBA_BLOCK), (0, 0)))
        y_d = _moba_prompt(mqn, mkt, mvt, kmean_pad)
        flat = lambda a: a.reshape(tp, GROUP)
        xp = _out_proj(flat(y_a), flat(y_b), flat(y_c), flat(y_d), w_out_bf, xp, mod_p, **kw_p)
        outs_p.append((fkt, fvt, lft, mkt, mvt, ssm_h, ssm_buf, gdn_s, gdn_buf))

        proj_s = _in_proj(xs, g_mix, mod_s, w_in_bf, **kw_s)
        (sq, sk, smq, smk, slf, sfkt, sfvt, smkt, smvt, slft, xbct, zat, smt, sbo, qkvt, zct, gbo) = _sprep(
            proj_s, jnp.transpose(state_ssm_conv[l], (1, 0, 2)), jnp.transpose(state_gdn_conv[l], (1, 0, 2)),
            gfq, gfk, gmq, gmk, bf_full, scw, scb, gcw)
        smt3 = smt.reshape(LANES, 1, bd)
        yat, ssm_new = _ssd_step(l, xbct, zat, smt3, ssm_t, _per_head_rows(ssd_dt_bias[l], bd),
                                 _per_head_rows(ssd_a_log[l], bd), _per_head_rows(ssd_d[l], bd))
        yct, gdn_new = _gdn_step(l, qkvt, zct, smt3, gdn_t, _per_head_rows(gdn_a_log[l], bd),
                                 _per_head_rows(gdn_dt_bias[l], bd),
                                 jnp.broadcast_to(gdn_norm[l].astype(F32).reshape(HD, 1), (HD, bd)))
        rows = lambda a: a.reshape(bd, 1, a.shape[-1])
        ys_b = _fox_decode(pt_flat, l, rows(sq), rows(sk), rows(proj_s), rows(slf), fox_kc, fox_vc, fox_lfc, n_pages=n_pages)
        ys_d = _moba_decode(pt_flat, l, rows(smq), rows(smk), rows(proj_s), moba_kc, moba_vc, n_pages=n_pages)
        xs = _out_proj_s(yat, jnp.broadcast_to(ssd_norm[l].astype(F32).reshape(GROUP, 1), (GROUP, bd)),
                         ys_b.reshape(bd, GROUP), yct, ys_d.reshape(bd, GROUP), w_out_bf, xs, mod_s)
        outs_s.append((sfkt, sfvt, slft, smkt, smvt, ssm_new, sbo, gdn_new, gbo))

        i = l // 2
        if l % 2 == 0:
            wg, wu, wd = _bf(ffn_w_gate[i]), _bf(ffn_w_up[i]), _bf(ffn_w_down[i])
            tf = wg.shape[1] // 2
            xp = _ffn_dense(xp, g_ffn, mod_p, wg, wu, wd, tf=tf, **kw_p)
            xs = _ffn_dense(xs, g_ffn, mod_s, wg, wu, wd, tf=tf, **kw_s)
        else:
            wg, wu, wd = _bf(moe_w_gate[i]), _bf(moe_w_up[i]), _bf(moe_w_down[i])
            ne = wg.shape[0]
            router_pad = jnp.pad(moe_router[i], ((0, 0), (0, LANES - ne)))
            tf = wg.shape[2] // 2
            xp = _ffn_moe(xp, g_ffn, mod_p, router_pad, wg, wu, wd, tf=tf, **kw_p)
            xs = _ffn_moe(xs, g_ffn, mod_s, router_pad, wg, wu, wd, tf=tf, **kw_s)

    stack = lambda outs, j: jnp.stack([o[j] for o in outs], axis=0)
    kv_p = lambda j: jnp.transpose(stack(outs_p, j), (0, 1, 4, 2, 3))
    kv_s = lambda j: jnp.transpose(stack(outs_s, j).reshape(depth, HEADS, HD, bd), (0, 3, 1, 2)).reshape(depth, bd, 1, HEADS, HD)
    st_s = lambda j: jnp.transpose(stack(outs_s, j), (0, 4, 1, 2, 3))
    buf_s = lambda j: jnp.transpose(stack(outs_s, j), (0, 2, 1, 3))
    return (xp.reshape(bp, ln, d), xs.reshape(bd, 1, d),
            kv_p(0), kv_p(1), jnp.transpose(stack(outs_p, 2), (0, 1, 3, 2)), kv_p(3), kv_p(4),
            stack(outs_p, 5), stack(outs_p, 6), stack(outs_p, 7), stack(outs_p, 8),
            kv_s(0), kv_s(1), jnp.transpose(stack(outs_s, 2), (0, 2, 1)).reshape(depth, bd, 1, HEADS), kv_s(3), kv_s(4),
            st_s(5), buf_s(6), st_s(7), buf_s(8))
```

```python
import functools

import jax
import jax.numpy as jnp
from jax import lax
from jax.experimental import pallas as pl
from jax.experimental.pallas import tpu as pltpu

F32 = jnp.float32
BF16 = jnp.bfloat16
EPS = 1e-6
NEG = -1e30

D_MODEL = 1024
GROUP = 256
HEADS = 4
HD = 64
PAGE = 128
MOBA_BLOCK = 256
MOBA_TOPK = 3
CONV_W = 4
SSD_CHUNK = 128
GDN_CHUNK = 64
GDN_TOK = 256
LANES = 128

XBC_W, QKV_W = 512, 768
COL_XBC = 0
COL_ZA = 2
COL_FQ, COL_FK, COL_FV = 3, 4, 5
COL_QKV = 2
COL_ZC = 9
COL_MQ, COL_MK, COL_MV = 10, 11, 12
COL_SMALL = 26
PROJ_COLS = 3456
L_DT, L_FF, L_BETA, L_GA = 0, 4, 8, 12


def _dot(a, b):
    return jnp.dot(a, b, preferred_element_type=F32)


def _dg(a, b, ca, cb):
    return lax.dot_general(a, b, (((ca,), (cb,)), ((), ())), preferred_element_type=F32)


def _bf(a):
    return a.astype(BF16)


def _hi_lo(a):
    hi = _bf(a)
    return hi, _bf(a - hi.astype(F32))


def _split3(a):
    a1 = _bf(a)
    r = a - a1.astype(F32)
    a2 = _bf(r)
    return a1, a2, _bf(r - a2.astype(F32))


def _mm(a, b):
    return _dot(_bf(a), _bf(b))


def _nt(a, b):
    return _dg(_bf(a), _bf(b), 1, 1)


def _tn(a, b):
    return _dg(_bf(a), _bf(b), 0, 0)


def _x3(f, a, b):
    a1, a2 = _hi_lo(a)
    b1, b2 = _hi_lo(b)
    return f(a1, b1) + (f(a1, b2) + f(a2, b1))


def _mm3(a, b):
    return _x3(_dot, a, b)


def _nt3(a, b):
    return _x3(lambda x, y: _dg(x, y, 1, 1), a, b)


def _tn3(a, b):
    return _x3(lambda x, y: _dg(x, y, 0, 0), a, b)


def _hi_lo_f32(a):
    hi = _bf(a).astype(F32)
    return hi, a - hi


def _mm3c(a, b):
    a1, a2 = _hi_lo_f32(a)
    b1, b2 = _hi_lo_f32(b)
    return _dot(_bf(jnp.concatenate([a1, a2, a1], axis=1)), _bf(jnp.concatenate([b1, b1, b2], axis=0)))


def _nt3c(a, b):
    a1, a2 = _hi_lo_f32(a)
    b1, b2 = _hi_lo_f32(b)
    return _dg(_bf(jnp.concatenate([a1, a2, a1], axis=1)), _bf(jnp.concatenate([b1, b1, b2], axis=1)), 1, 1)


def _tn3c(a, b):
    a1, a2 = _hi_lo_f32(a)
    b1, b2 = _hi_lo_f32(b)
    return _dg(_bf(jnp.concatenate([a1, a2, a1], axis=0)), _bf(jnp.concatenate([b1, b1, b2], axis=0)), 0, 0)


def _mm01(m01, x):
    x1, x2, x3 = _split3(x)
    return _dot(m01, x1) + (_dot(m01, x2) + _dot(m01, x3))


def _mmx01(x, m01):
    x1, x2, x3 = _split3(x)
    return _dot(x1, m01) + (_dot(x2, m01) + _dot(x3, m01))


def _tr(x):
    n = x.shape[1]
    eye = _bf(lax.broadcasted_iota(jnp.int32, (n, n), 0) == lax.broadcasted_iota(jnp.int32, (n, n), 1))
    x1, x2, x3 = _split3(x)
    return _dg(eye, x1, 1, 1) + (_dg(eye, x2, 1, 1) + _dg(eye, x3, 1, 1))


def _iota(shape, dim):
    return lax.broadcasted_iota(jnp.int32, shape, dim)


def _silu(x):
    return x * jax.nn.sigmoid(x)


def _softplus(x):
    return jnp.maximum(x, 0.0) + jnp.log1p(jnp.exp(-jnp.abs(x)))


def _log_sigmoid(x):
    return -_softplus(-x)


def _head_mat():
    r = lax.shift_right_logical(_iota((GROUP, GROUP), 0), 6)
    c = lax.shift_right_logical(_iota((GROUP, GROUP), 1), 6)
    return _bf(r == c)


def _headsum(x, hm):
    x1, x2 = _hi_lo(x)
    return _dot(x1, hm) + _dot(x2, hm)


def _normmod(x, g, sc, sh):
    y = x * lax.rsqrt(jnp.mean(x * x, axis=-1, keepdims=True) + EPS) * g
    return y * (1.0 + sc) + sh


def _cparams(sem, vmem_mb=48):
    return pltpu.CompilerParams(dimension_semantics=sem, vmem_limit_bytes=vmem_mb * 1024 * 1024)


def _mod_spec(per_row, tm, rows_per_batch, chunk):
    if per_row:
        return pl.BlockSpec((1, tm, D_MODEL), lambda i, *_: (0, i, chunk))
    return pl.BlockSpec((1, 1, D_MODEL), lambda i, *_: ((i * tm) // rows_per_batch, 0, chunk))


def _ada_kernel(c_ref, w_ref, b_ref, o_ref):
    o_ref[0] = _mm3(_silu(c_ref[...]), w_ref[0]) + b_ref[0]


def _ada_mod(c_all, w_ada, b_ada):
    depth, d, n = w_ada.shape
    r = c_all.shape[0]
    tn = 1536
    return pl.pallas_call(
        _ada_kernel, grid=(depth, n // tn),
        in_specs=[pl.BlockSpec((r, d), lambda l, j: (0, 0)),
                  pl.BlockSpec((1, d, tn), lambda l, j: (l, 0, j)),
                  pl.BlockSpec((1, 1, tn), lambda l, j: (l, 0, j))],
        out_specs=pl.BlockSpec((1, r, tn), lambda l, j: (l, 0, j)),
        out_shape=jax.ShapeDtypeStruct((depth, r, n), F32),
        compiler_params=_cparams(("parallel", "parallel")), name="ada_mod",
    )(c_all, w_ada, b_ada.reshape(depth, 1, n))


def _inproj_kernel(x_ref, g_ref, sc_ref, sh_ref, w_ref, o_ref, *, precise):
    h = _normmod(x_ref[...], g_ref[...], sc_ref[0], sh_ref[0])
    o_ref[...] = _mm3(h, w_ref[...]) if precise else _dot(_bf(h), w_ref[...])


def _in_proj(x2d, g, mod, w, *, tm, tn, per_row, rows_per_batch, precise=False):
    t, d = x2d.shape
    n = w.shape[1]
    return pl.pallas_call(
        functools.partial(_inproj_kernel, precise=precise), grid=(t // tm, n // tn),
        in_specs=[pl.BlockSpec((tm, d), lambda i, j: (i, 0)),
                  pl.BlockSpec((1, d), lambda i, j: (0, 0)),
                  _mod_spec(per_row, tm, rows_per_batch, 1),
                  _mod_spec(per_row, tm, rows_per_batch, 0),
                  pl.BlockSpec((d, tn), lambda i, j: (0, j))],
        out_specs=pl.BlockSpec((tm, tn), lambda i, j: (i, j)),
        out_shape=jax.ShapeDtypeStruct((t, n), F32),
        compiler_params=_cparams(("parallel", "arbitrary")), name="in_proj",
    )(x2d, g, mod, mod, w)


def _prep_kernel(fq_ref, fk_ref, fv_ref, mq_ref, mk_ref, mv_ref, sm_ref, gfq_ref, gfk_ref, gmq_ref, gmk_ref, bf_ref,
                 fqn_ref, fkt_ref, fvt_ref, mqn_ref, mkt_ref, mvt_ref, lft_ref, cum_ref, cumt_ref, kmean_ref, carry_scr):
    hm = _head_mat()

    def hnorm(x, g):
        return x * lax.rsqrt(_headsum(x * x, hm) * (1.0 / HD) + EPS) * g

    def put_t(ref, x):
        xt = _tr(x)
        for h in range(HEADS):
            ref[0, h] = xt[h * HD:(h + 1) * HD, :]

    fqn_ref[0] = hnorm(fq_ref[0], gfq_ref[...])
    put_t(fkt_ref, hnorm(fk_ref[0], gfk_ref[...]))
    put_t(fvt_ref, fv_ref[0])
    mqn_ref[0] = hnorm(mq_ref[0], gmq_ref[...])
    mkn = hnorm(mk_ref[0], gmk_ref[...])
    put_t(mkt_ref, mkn)
    put_t(mvt_ref, mv_ref[0])
    kmean_ref[0, 0] = jnp.mean(mkn, axis=0, keepdims=True)
    lf = _log_sigmoid(sm_ref[0] + bf_ref[...])
    lft_ref[0] = _tr(lf)[L_FF:L_FF + HEADS, :]

    @pl.when(pl.program_id(1) == 0)
    def _():
        carry_scr[...] = jnp.zeros_like(carry_scr)

    tm = lf.shape[0]
    tril = _bf(_iota((tm, tm), 0) >= _iota((tm, tm), 1))
    cum = _mm01(tril, lf) + carry_scr[...]
    carry_scr[...] = cum[tm - 1:tm, :]
    cum_ref[0] = cum
    cumt_ref[0] = _tr(cum)[0:8, :]


def _prep(proj3, gfq, gfk, gmq, gmk, bf_full, *, tm):
    nb, ln, _ = proj3.shape

    def col(c):
        return pl.BlockSpec((1, tm, GROUP), lambda b, i: (b, i, c))

    row = pl.BlockSpec((1, GROUP), lambda b, i: (0, 0))
    rows_blk = pl.BlockSpec((1, tm, GROUP), lambda b, i: (b, i, 0))
    t_blk = pl.BlockSpec((1, HEADS, HD, tm), lambda b, i: (b, 0, 0, i))
    rows_shape = jax.ShapeDtypeStruct((nb, ln, GROUP), F32)
    t_shape = jax.ShapeDtypeStruct((nb, HEADS, HD, ln), F32)
    return pl.pallas_call(
        _prep_kernel, grid=(nb, ln // tm),
        in_specs=[col(COL_FQ), col(COL_FK), col(COL_FV), col(COL_MQ), col(COL_MK), col(COL_MV),
                  pl.BlockSpec((1, tm, LANES), lambda b, i: (b, i, COL_SMALL)),
                  row, row, row, row, pl.BlockSpec((1, LANES), lambda b, i: (0, 0))],
        out_specs=[rows_blk, t_blk, t_blk, rows_blk, t_blk, t_blk,
                   pl.BlockSpec((1, HEADS, tm), lambda b, i: (b, 0, i)),
                   pl.BlockSpec((1, tm, LANES), lambda b, i: (b, i, 0)),
                   pl.BlockSpec((1, 8, tm), lambda b, i: (b, 0, i)),
                   pl.BlockSpec((1, 1, 1, GROUP), lambda b, i: (b, i, 0, 0))],
        out_shape=[rows_shape, t_shape, t_shape, rows_shape, t_shape, t_shape,
                   jax.ShapeDtypeStruct((nb, HEADS, ln), F32), jax.ShapeDtypeStruct((nb, ln, LANES), F32),
                   jax.ShapeDtypeStruct((nb, 8, ln), F32), jax.ShapeDtypeStruct((nb, ln // tm, 1, GROUP), F32)],
        scratch_shapes=[pltpu.VMEM((1, LANES), F32)],
        compiler_params=_cparams(("parallel", "arbitrary")), name="attn_prep",
    )(proj3, proj3, proj3, proj3, proj3, proj3, proj3, gfq, gfk, gmq, gmk, bf_full)


def _attn_update(s, vt, h, m_scr, l_scr, acc_scr):
    rep = s.shape[1] // LANES
    m_old = m_scr[h]
    m_new = jnp.maximum(m_old, jnp.max(s, axis=1, keepdims=True))
    p = jnp.exp(s - jnp.concatenate([m_new] * rep, axis=1))
    alpha = jnp.exp(m_old - m_new)
    l_scr[h] = alpha * l_scr[h] + jnp.sum(p, axis=1, keepdims=True)
    acc_scr[h] = alpha[:, 0:HD] * acc_scr[h] + _nt(p, vt)
    m_scr[h] = m_new


def _attn_init(m_scr, l_scr, acc_scr):
    m_scr[...] = jnp.full_like(m_scr, NEG)
    l_scr[...] = jnp.zeros_like(l_scr)
    acc_scr[...] = jnp.zeros_like(acc_scr)


def _attn_finish(o_ref, l_scr, acc_scr):
    for h in range(HEADS):
        o_ref[0, :, h * HD:(h + 1) * HD] = acc_scr[h] / l_scr[h][:, 0:HD]


def _attn_scratch(t):
    return [pltpu.VMEM((HEADS, t, LANES), F32), pltpu.VMEM((HEADS, t, LANES), F32), pltpu.VMEM((HEADS, t, HD), F32)]


def _causal_pairs(n):
    pairs = [(qb, kb) for qb in range(n) for kb in range(qb + 1)]
    return jnp.asarray([p[0] for p in pairs], jnp.int32), jnp.asarray([p[1] for p in pairs], jnp.int32)


def _fox_kernel(qtab_ref, ktab_ref, q_ref, kt_ref, vt_ref, cq_ref, ck_ref, o_ref, m_scr, l_scr, acc_scr, cqb_scr,
                *, t, scale):
    step_id = pl.program_id(1)
    qi, ki = qtab_ref[step_id], ktab_ref[step_id]
    rep = t // LANES

    @pl.when(ki == 0)
    def _():
        _attn_init(m_scr, l_scr, acc_scr)
        cq = cq_ref[0]
        for h in range(HEADS):
            cqb_scr[h] = jnp.broadcast_to(cq[:, L_FF + h:L_FF + h + 1], (t, LANES))

    def step(diagonal):
        q, ck = q_ref[0], ck_ref[0]
        for h in range(HEADS):
            s = _dot(_bf(q[:, h * HD:(h + 1) * HD]), _bf(kt_ref[0, h])) * scale
            s = s + (jnp.concatenate([cqb_scr[h]] * rep, axis=1) - ck[L_FF + h:L_FF + h + 1, :])
            if diagonal:
                s = jnp.where(_iota((t, t), 1) <= _iota((t, t), 0), s, NEG)
            _attn_update(s, vt_ref[0, h], h, m_scr, l_scr, acc_scr)

    @pl.when(ki < qi)
    def _():
        step(False)

    @pl.when(ki == qi)
    def _():
        step(True)
        _attn_finish(o_ref, l_scr, acc_scr)


def _fox_prompt(q, kt, vt, cum, cumt, *, t):
    b, ln, _ = q.shape
    n = ln // t
    qtab, ktab = _causal_pairs(n)
    kv = pl.BlockSpec((1, HEADS, HD, t), lambda bb, s, qt, kt_: (bb, 0, 0, kt_[s]))
    qs = pl.BlockSpec((1, t, GROUP), lambda bb, s, qt, kt_: (bb, qt[s], 0))
    return pl.pallas_call(
        functools.partial(_fox_kernel, t=t, scale=HD ** -0.5),
        grid_spec=pltpu.PrefetchScalarGridSpec(
            num_scalar_prefetch=2, grid=(b, qtab.shape[0]),
            in_specs=[qs, kv, kv,
                      pl.BlockSpec((1, t, LANES), lambda bb, s, qt, kt_: (bb, qt[s], 0)),
                      pl.BlockSpec((1, 8, t), lambda bb, s, qt, kt_: (bb, 0, kt_[s]))],
            out_specs=qs,
            scratch_shapes=_attn_scratch(t) + [pltpu.VMEM((HEADS, t, LANES), F32)]),
        out_shape=jax.ShapeDtypeStruct((b, ln, GROUP), F32),
        compiler_params=_cparams(("parallel", "arbitrary")), name="fox_prompt",
    )(qtab, ktab, q, kt, vt, cum, cumt)


def _moba_kernel(qtab_ref, jtab_ref, q_ref, kt_ref, vt_ref, km_ref, o_ref, m_scr, l_scr, acc_scr, sel_scr, *, scale):
    t = MOBA_BLOCK
    step_id = pl.program_id(1)
    qi, j = qtab_ref[step_id], jtab_ref[step_id]
    lane = _iota((t, LANES), 1)

    def rel():
        return (_iota((t, t), 0) - _iota((t, t), 1)).astype(F32)

    @pl.when(j == 0)
    def _():
        _attn_init(m_scr, l_scr, acc_scr)
        q, km = q_ref[0], km_ref[0]
        lane_f = lane.astype(F32)
        r = rel()
        for h in range(HEADS):
            sl = slice(h * HD, (h + 1) * HD)
            gate = jnp.where(lane < qi, _nt3(q[:, sl], km[:, sl]), NEG)
            sel = jnp.zeros((t, LANES), F32)
            for k in range(MOBA_TOPK):
                mx = jnp.max(gate, axis=1, keepdims=True)
                idx = jnp.min(jnp.where(gate == mx, lane_f, float(LANES)), axis=1, keepdims=True)
                hit = lane_f == idx
                sel = jnp.where(hit, jnp.maximum(sel, jnp.where(qi > k, 1.0, 0.0)), sel)
                gate = jnp.where(hit, NEG, gate)
            sel_scr[h] = sel
            s = _dot(_bf(q[:, sl]), _bf(kt_ref[0, h])) * scale - (2.0 ** (-2 * (h + 1))) * r
            _attn_update(jnp.where(r >= 0.0, s, NEG), vt_ref[0, h], h, m_scr, l_scr, acc_scr)

    @pl.when(jnp.logical_and(j >= 1, j <= qi))
    def _():
        q = q_ref[0]
        n = j - 1
        dist = rel() + ((qi - n) * t).astype(F32)
        for h in range(HEADS):
            sl = slice(h * HD, (h + 1) * HD)
            picked = jnp.sum(jnp.where(lane == n, sel_scr[h], 0.0), axis=1, keepdims=True)
            s = _dot(_bf(q[:, sl]), _bf(kt_ref[0, h])) * scale - (2.0 ** (-2 * (h + 1))) * dist
            _attn_update(jnp.where(picked > 0.5, s, NEG), vt_ref[0, h], h, m_scr, l_scr, acc_scr)

    @pl.when(j == qi)
    def _():
        _attn_finish(o_ref, l_scr, acc_scr)


def _moba_prompt(q, kt, vt, kmean_pad):
    b, ln, _ = q.shape
    t = MOBA_BLOCK
    n = ln // t

    qtab, jtab = _causal_pairs(n)

    def kv_idx(bb, s, qt, jt):
        return (bb, 0, 0, jnp.where(jt[s] == 0, qt[s], jt[s] - 1))

    qs = pl.BlockSpec((1, t, GROUP), lambda bb, s, qt, jt: (bb, qt[s], 0))
    kv = pl.BlockSpec((1, HEADS, HD, t), kv_idx)
    return pl.pallas_call(
        functools.partial(_moba_kernel, scale=HD ** -0.5),
        grid_spec=pltpu.PrefetchScalarGridSpec(
            num_scalar_prefetch=2, grid=(b, qtab.shape[0]),
            in_specs=[qs, kv, kv, pl.BlockSpec((1, LANES, GROUP), lambda bb, s, qt, jt: (bb, 0, 0))],
            out_specs=qs,
            scratch_shapes=_attn_scratch(t) + [pltpu.VMEM((HEADS, t, LANES), F32)]),
        out_shape=jax.ShapeDtypeStruct((b, ln, GROUP), F32),
        compiler_params=_cparams(("parallel", "arbitrary")), name="moba_prompt",
    )(qtab, jtab, q, kt, vt, kmean_pad)


def _query_cols(q):
    return _tr(jnp.broadcast_to(q, (LANES, GROUP)))


def _head_dots(a, b):
    ind = _bf(lax.shift_right_logical(_iota((GROUP, LANES), 0), 6) == _iota((GROUP, LANES), 1))
    return _mmx01(jnp.broadcast_to(a * b, (8, GROUP)), ind)[0:1, :]


def _lane_sums(acc):
    a1, a2, a3 = _split3(acc)
    ones = jnp.ones((8, LANES), BF16)
    return (_dg(ones, a1, 1, 1) + (_dg(ones, a2, 1, 1) + _dg(ones, a3, 1, 1)))[0:1, :]


def _decode_head(srows, s_self, v_refs, h):
    mrow = srows[0]
    for r in srows[1:]:
        mrow = jnp.maximum(mrow, r)
    m = jnp.maximum(jnp.max(mrow, axis=1, keepdims=True), s_self)
    acc = jnp.zeros((HD, LANES), F32)
    lrow = jnp.zeros((1, LANES), F32)
    for j, r in enumerate(srows):
        p = jnp.exp(r - m)
        lrow = lrow + p
        acc = acc + p * v_refs[j][0, 0, h]
    w_self = jnp.exp(s_self - m)
    return acc, w_self, jnp.sum(lrow, axis=1, keepdims=True) + w_self


def _fox_dec_kernel(pt_ref, q_ref, kn_ref, vn_ref, lfn_ref, lfc_ref, *rest, n_pages, scale):
    n = n_pages
    k_refs, v_refs = rest[:n], rest[n:2 * n]
    o_ref, qb_scr, lf_scr, acc_scr = rest[2 * n:]
    q = q_ref[0]
    qb_scr[...] = _query_cols(q)
    s_new = _head_dots(q, kn_ref[0]) * scale
    lfn = lfn_ref[0]
    first = pl.program_id(0) * n
    for j in range(n):
        row = lfc_ref[pt_ref[first + j]]
        for h in range(HEADS):
            lf_scr[h * n + j:h * n + j + 1, :] = row[:, h * PAGE:(h + 1) * PAGE]
    lf_all = lf_scr[...]
    later = _bf(_iota((PAGE, PAGE), 0) > _iota((PAGE, PAGE), 1))
    suffix = _mmx01(lf_all, later)
    page_sum = jnp.sum(lf_all, axis=1, keepdims=True)
    grp = lax.shift_right_logical(_iota((1, GROUP), 1), 6)
    w_row = jnp.zeros((1, GROUP), F32)
    l_row = jnp.ones((1, GROUP), F32)
    for h in range(HEADS):
        carry = lfn[:, h:h + 1]
        offs = [None] * n
        for j in reversed(range(n)):
            offs[j] = carry
            carry = carry + page_sum[h * n + j:h * n + j + 1, :]
        qh = qb_scr[h * HD:(h + 1) * HD, :]
        srows = [jnp.sum(qh * k_refs[j][0, 0, h], axis=0, keepdims=True) * scale
                 + (suffix[h * n + j:h * n + j + 1, :] + offs[j]) for j in range(n)]
        acc, w_self, l = _decode_head(srows, s_new[:, h:h + 1], v_refs, h)
        acc_scr[h * HD:(h + 1) * HD, :] = acc
        w_row = jnp.where(grp == h, w_self, w_row)
        l_row = jnp.where(grp == h, l, l_row)
    o_ref[0] = (_lane_sums(acc_scr[...]) + w_row * vn_ref[0]) / l_row


def _fox_decode(pt_flat, layer, q, kn, proj_s3, lfn, kc, vc, lfc, *, n_pages):
    bd = q.shape[0]
    row = pl.BlockSpec((1, 1, GROUP), lambda b, pt: (b, 0, 0))

    def page(j):
        return lambda b, pt: (layer, pt[b * n_pages + j], 0, 0, 0)

    kv_specs = [pl.BlockSpec((1, 1, HEADS, HD, PAGE), page(j)) for j in range(n_pages)]
    return pl.pallas_call(
        functools.partial(_fox_dec_kernel, n_pages=n_pages, scale=HD ** -0.5),
        grid_spec=pltpu.PrefetchScalarGridSpec(
            num_scalar_prefetch=1, grid=(bd,),
            in_specs=[row, row, pl.BlockSpec((1, 1, GROUP), lambda b, pt: (b, 0, COL_FV)),
                      pl.BlockSpec((1, 1, HEADS), lambda b, pt: (b, 0, 0)),
                      pl.BlockSpec(lfc.shape, lambda b, pt: (0, 0, 0))] + kv_specs + kv_specs,
            out_specs=row,
            scratch_shapes=[pltpu.VMEM((GROUP, LANES), F32), pltpu.VMEM((HEADS * n_pages, PAGE), F32),
                            pltpu.VMEM((GROUP, LANES), F32)]),
        out_shape=jax.ShapeDtypeStruct((bd, 1, GROUP), F32),
        compiler_params=_cparams(("parallel",)), name="fox_decode",
    )(pt_flat, q, kn, proj_s3, lfn, lfc, *([kc] * n_pages), *([vc] * n_pages))


def _moba_dec_kernel(pt_ref, q_ref, kn_ref, vn_ref, *rest, n_pages, scale):
    del pt_ref
    n = n_pages
    ppb = MOBA_BLOCK // PAGE
    n_past = n // ppb
    past_len = n * PAGE
    k_refs, v_refs = rest[:n], rest[n:2 * n]
    o_ref, qb_scr, acc_scr = rest[2 * n:]
    q = q_ref[0]
    qb_scr[...] = _query_cols(q)
    s_own = _head_dots(q, kn_ref[0]) * scale
    rows8, lanes8 = _iota((8, LANES), 0), _iota((8, LANES), 1)
    raws = []
    gates = jnp.full((8, LANES), NEG, F32)
    for h in range(HEADS):
        qh = qb_scr[h * HD:(h + 1) * HD, :]
        raw_h = [jnp.sum(qh * k_refs[j][0, 0, h], axis=0, keepdims=True) for j in range(n)]
        raws.append(raw_h)
        for blk in range(n_past):
            tot = raw_h[ppb * blk]
            for j in range(ppb * blk + 1, ppb * (blk + 1)):
                tot = tot + raw_h[j]
            gate = jnp.sum(tot, axis=1, keepdims=True) * (1.0 / MOBA_BLOCK)
            gates = jnp.where(jnp.logical_and(rows8 == h, lanes8 == blk), gate, gates)
    lanes_f = lanes8.astype(F32)
    sel = jnp.zeros((8, LANES), F32)
    for _ in range(min(MOBA_TOPK, n_past)):
        mx = jnp.max(gates, axis=1, keepdims=True)
        idx = jnp.min(jnp.where(gates == mx, lanes_f, float(LANES)), axis=1, keepdims=True)
        hit = lanes_f == idx
        sel = jnp.where(hit, 1.0, sel)
        gates = jnp.where(hit, NEG, gates)
    lane_f = _iota((1, LANES), 1).astype(F32)
    grp = lax.shift_right_logical(_iota((1, GROUP), 1), 6)
    w_row = jnp.zeros((1, GROUP), F32)
    l_row = jnp.ones((1, GROUP), F32)
    for h in range(HEADS):
        slope = 2.0 ** (-2 * (h + 1))
        srows = []
        for j in range(n):
            keep = sel[h:h + 1, j // ppb:j // ppb + 1]
            s = raws[h][j] * scale - slope * (float(past_len - j * PAGE) - lane_f)
            srows.append(s * keep + (1.0 - keep) * NEG)
        acc, w_self, l = _decode_head(srows, s_own[:, h:h + 1], v_refs, h)
        acc_scr[h * HD:(h + 1) * HD, :] = acc
        w_row = jnp.where(grp == h, w_self, w_row)
        l_row = jnp.where(grp == h, l, l_row)
    o_ref[0] = (_lane_sums(acc_scr[...]) + w_row * vn_ref[0]) / l_row


def _moba_decode(pt_flat, layer, q, kn, proj_s3, kc, vc, *, n_pages):
    bd = q.shape[0]
    assert (n_pages * PAGE) % MOBA_BLOCK == 0 and n_pages * PAGE // MOBA_BLOCK <= LANES
    row = pl.BlockSpec((1, 1, GROUP), lambda b, pt: (b, 0, 0))

    def page(j):
        return lambda b, pt: (layer, pt[b * n_pages + j], 0, 0, 0)

    kv_specs = [pl.BlockSpec((1, 1, HEADS, HD, PAGE), page(j)) for j in range(n_pages)]
    return pl.pallas_call(
        functools.partial(_moba_dec_kernel, n_pages=n_pages, scale=HD ** -0.5),
        grid_spec=pltpu.PrefetchScalarGridSpec(
            num_scalar_prefetch=1, grid=(bd,),
            in_specs=[row, row, pl.BlockSpec((1, 1, GROUP), lambda b, pt: (b, 0, COL_MV))] + kv_specs + kv_specs,
            out_specs=row,
            scratch_shapes=[pltpu.VMEM((GROUP, LANES), F32), pltpu.VMEM((GROUP, LANES), F32)]),
        out_shape=jax.ShapeDtypeStruct((bd, 1, GROUP), F32),
        compiler_params=_cparams(("parallel",)), name="moba_decode",
    )(pt_flat, q, kn, proj_s3, *([kc] * n_pages), *([vc] * n_pages))


def _chunk_conv(u, cw, ext_scr, c_len):
    ext_scr[8:8 + c_len, :] = u
    out = (cw[3:4, :] * u + cw[2:3, :] * ext_scr[7:7 + c_len, :]
           + cw[1:2, :] * ext_scr[6:6 + c_len, :] + cw[0:1, :] * ext_scr[5:5 + c_len, :])
    ext_scr[0:8, :] = ext_scr[c_len:c_len + 8, :]
    return out


def _ssd_kernel(xbc_ref, z_ref, sm_ref, cw_ref, cb_ref, dtb_ref, alog_ref, d_ref, ng_ref,
                y_ref, hs_ref, cs_ref, ext_scr, h_scr, y_scr, *, c_len):
    c, nc = pl.program_id(1), pl.num_programs(1)

    @pl.when(c == 0)
    def _():
        ext_scr[0:8, :] = jnp.zeros((8, XBC_W), F32)
        h_scr[...] = jnp.zeros_like(h_scr)

    u = xbc_ref[0]
    act = _silu(_chunk_conv(u, cw_ref[...], ext_scr, c_len) + cb_ref[...])

    @pl.when(c == nc - 1)
    def _():
        cs_ref[0] = u[c_len - (CONV_W - 1):c_len, :]

    xa, bm, cm = act[:, 0:GROUP], act[:, GROUP:GROUP + 2 * HD], act[:, GROUP + 2 * HD:]
    dt = _softplus(sm_ref[0] + dtb_ref[...])
    a_neg = -jnp.exp(alog_ref[...])
    row, colm = _iota((c_len, c_len), 0), _iota((c_len, c_len), 1)
    causal = row >= colm
    gam = _mm01(_bf(causal), dt * a_neg)
    gam_t, dt_t = _tr(gam), _tr(dt)
    glast = gam[c_len - 1:c_len, :]
    eg = jnp.exp(gam)
    wst = jnp.exp(glast - gam) * dt
    elast = jnp.exp(glast)
    dvec = d_ref[...]
    for g in range(2):
        cg, bg = cm[:, g * HD:(g + 1) * HD], bm[:, g * HD:(g + 1) * HD]
        cb = _nt(cg, bg)
        for h in (2 * g, 2 * g + 1):
            sl = slice(h * HD, (h + 1) * HD)
            hl = slice(L_DT + h, L_DT + h + 1)
            dec = jnp.exp(jnp.where(causal, gam[:, hl] - gam_t[hl, :], NEG))
            xh = xa[:, sl]
            hprev = h_scr[h]
            y_scr[:, sl] = (_mm(cb * dec * dt_t[hl, :], xh) + _nt(cg, hprev) * eg[:, hl] + dvec[:, hl] * xh)
            h_scr[h] = elast[:, hl] * hprev + _tn3(xh * wst[:, hl], bg)
    yg = y_scr[...] * _silu(z_ref[0])
    y_ref[0] = yg * lax.rsqrt(jnp.mean(yg * yg, axis=-1, keepdims=True) + EPS) * ng_ref[...]

    @pl.when(c == nc - 1)
    def _():
        hs_ref[0] = h_scr[...]


def _ssd_prompt(proj3, cw, cb, dtb, alog, dvec, ng):
    b, ln, _ = proj3.shape
    c_len = SSD_CHUNK
    vec = lambda w: pl.BlockSpec((1, w), lambda bb, c: (0, 0))
    return pl.pallas_call(
        functools.partial(_ssd_kernel, c_len=c_len), grid=(b, ln // c_len),
        in_specs=[pl.BlockSpec((1, c_len, XBC_W), lambda bb, c: (bb, c, COL_XBC)),
                  pl.BlockSpec((1, c_len, GROUP), lambda bb, c: (bb, c, COL_ZA)),
                  pl.BlockSpec((1, c_len, LANES), lambda bb, c: (bb, c, COL_SMALL)),
                  pl.BlockSpec((CONV_W, XBC_W), lambda bb, c: (0, 0)), vec(XBC_W), vec(LANES), vec(LANES), vec(LANES),
                  vec(GROUP)],
        out_specs=[pl.BlockSpec((1, c_len, GROUP), lambda bb, c: (bb, c, 0)),
                   pl.BlockSpec((1, HEADS, HD, HD), lambda bb, c: (bb, 0, 0, 0)),
                   pl.BlockSpec((1, CONV_W - 1, XBC_W), lambda bb, c: (bb, 0, 0))],
        out_shape=[jax.ShapeDtypeStruct((b, ln, GROUP), F32), jax.ShapeDtypeStruct((b, HEADS, HD, HD), F32),
                   jax.ShapeDtypeStruct((b, CONV_W - 1, XBC_W), F32)],
        scratch_shapes=[pltpu.VMEM((c_len + 8, XBC_W), F32), pltpu.VMEM((HEADS, HD, HD), F32),
                        pltpu.VMEM((c_len, GROUP), F32)],
        compiler_params=_cparams(("parallel", "arbitrary")), name="ssd_prompt",
    )(proj3, proj3, proj3, cw, cb, dtb, alog, dvec, ng)


def _gdn_kernel(qkv_ref, z_ref, sm_ref, cw_ref, alog_ref, dtb_ref, ng_ref,
                y_ref, ss_ref, cs_ref, ext_scr, s_scr, *, tok, c_len):
    c, nc = pl.program_id(1), pl.num_programs(1)

    @pl.when(c == 0)
    def _():
        ext_scr[0:8, :] = jnp.zeros((8, QKV_W), F32)
        s_scr[...] = jnp.zeros_like(s_scr)

    u = qkv_ref[0]
    act = _silu(_chunk_conv(u, cw_ref[...], ext_scr, tok))

    @pl.when(c == nc - 1)
    def _():
        cs_ref[0] = u[tok - (CONV_W - 1):tok, :]

    hm = _head_mat()
    q, k, v = act[:, 0:GROUP], act[:, GROUP:2 * GROUP], act[:, 2 * GROUP:]
    qn = q * lax.rsqrt(_headsum(q * q, hm) + EPS) * (HD ** -0.5)
    kn = k * lax.rsqrt(_headsum(k * k, hm) + EPS)
    sm = sm_ref[0]
    beta = jax.nn.sigmoid(sm)
    gl = -jnp.exp(alog_ref[...]) * _softplus(sm + dtb_ref[...])
    shift = c_len.bit_length() - 1
    ri, ci = _iota((tok, tok), 0), _iota((tok, tok), 1)
    same_chunk = lax.shift_right_logical(ri, shift) == lax.shift_right_logical(ci, shift)
    gam = _mm01(_bf(jnp.logical_and(ri >= ci, same_chunk)), gl)
    gam_t = _tr(gam)
    eg = jnp.exp(gam)
    row, colm = _iota((c_len, c_len), 0), _iota((c_len, c_len), 1)
    incl = row >= colm
    strict = row > colm
    eye = (row == colm).astype(F32)
    z, ng = z_ref[0], ng_ref[...]
    units = [(i, h) for i in range(tok // c_len) for h in range(HEADS)]
    n_fac = c_len.bit_length() - 2

    dec, kh, qh, bcol, pinv, mpow = {}, {}, {}, {}, {}, {}
    for un in units:
        i, h = un
        r = slice(i * c_len, (i + 1) * c_len)
        g = L_GA + h
        dec[un] = jnp.exp(jnp.where(incl, gam[r, g:g + 1] - gam_t[g:g + 1, r], NEG))
        kh[un], qh[un] = kn[r, h * HD:(h + 1) * HD], qn[r, h * HD:(h + 1) * HD]
        bcol[un] = beta[r, L_BETA + h:L_BETA + h + 1]
        nmat = bcol[un] * _nt3c(kh[un], kh[un]) * jnp.where(strict, dec[un], 0.0)
        pinv[un] = eye - nmat
        mpow[un] = _mm3c(nmat, nmat)
    for lvl in range(n_fac):
        for un in units:
            if lvl + 1 < n_fac:
                both = _mm3c(jnp.concatenate([pinv[un], mpow[un]], axis=0), mpow[un])
                pinv[un], mpow[un] = pinv[un] + both[0:c_len], both[c_len:]
            else:
                pinv[un] = pinv[un] + _mm3c(pinv[un], mpow[un])
    uu, wk, qkd = {}, {}, {}
    for un in units:
        i, h = un
        r = slice(i * c_len, (i + 1) * c_len)
        g = L_GA + h
        rhs = jnp.concatenate([bcol[un] * v[r, h * HD:(h + 1) * HD], (bcol[un] * eg[r, g:g + 1]) * kh[un]], axis=1)
        sol = _mm3c(pinv[un], rhs)
        uu[un], wk[un] = sol[:, 0:HD], sol[:, HD:]
        qkd[un] = _nt(qh[un], kh[un]) * dec[un]

    for h in range(HEADS):
        sl = slice(h * HD, (h + 1) * HD)
        g = L_GA + h
        s_run = s_scr[h]
        for i in range(tok // c_len):
            un = (i, h)
            r = slice(i * c_len, (i + 1) * c_len)
            glast = gam[(i + 1) * c_len - 1:(i + 1) * c_len, g:g + 1]
            on_state = _mm3c(jnp.concatenate([wk[un], qh[un] * eg[r, g:g + 1]], axis=0), s_run)
            w = uu[un] - on_state[0:c_len]
            o = on_state[c_len:] + _mm(qkd[un], w)
            s_run = jnp.exp(glast) * s_run + _tn3c(kh[un] * jnp.exp(glast - gam[r, g:g + 1]), w)
            on = o * lax.rsqrt(jnp.mean(o * o, axis=-1, keepdims=True) + EPS) * ng[:, sl]
            y_ref[0, r, sl] = on * _silu(z[r, sl])
        s_scr[h] = s_run

    @pl.when(c == nc - 1)
    def _():
        ss_ref[0] = s_scr[...]


def _gdn_prompt(proj3, cw, alog, dtb, ng):
    b, ln, _ = proj3.shape
    tok = GDN_TOK
    vec = lambda w: pl.BlockSpec((1, w), lambda bb, c: (0, 0))
    return pl.pallas_call(
        functools.partial(_gdn_kernel, tok=tok, c_len=GDN_CHUNK), grid=(b, ln // tok),
        in_specs=[pl.BlockSpec((1, tok, QKV_W), lambda bb, c: (bb, c, COL_QKV)),
                  pl.BlockSpec((1, tok, GROUP), lambda bb, c: (bb, c, COL_ZC)),
                  pl.BlockSpec((1, tok, LANES), lambda bb, c: (bb, c, COL_SMALL)),
                  pl.BlockSpec((CONV_W, QKV_W), lambda bb, c: (0, 0)), vec(LANES), vec(LANES), vec(GROUP)],
        out_specs=[pl.BlockSpec((1, tok, GROUP), lambda bb, c: (bb, c, 0)),
                   pl.BlockSpec((1, HEADS, HD, HD), lambda bb, c: (bb, 0, 0, 0)),
                   pl.BlockSpec((1, CONV_W - 1, QKV_W), lambda bb, c: (bb, 0, 0))],
        out_shape=[jax.ShapeDtypeStruct((b, ln, GROUP), F32), jax.ShapeDtypeStruct((b, HEADS, HD, HD), F32),
                   jax.ShapeDtypeStruct((b, CONV_W - 1, QKV_W), F32)],
        scratch_shapes=[pltpu.VMEM((tok + 8, QKV_W), F32), pltpu.VMEM((HEADS, HD, HD), F32)],
        compiler_params=_cparams(("parallel", "arbitrary")), name="gdn_prompt",
    )(proj3, proj3, proj3, cw, alog, dtb, ng)


def _sprep_kernel(fq_ref, fk_ref, fv_ref, mq_ref, mk_ref, mv_ref, xbc_ref, qkv_ref, za_ref, zc_ref, sm_ref,
                  sbuf_ref, gbuf_ref, gfq_ref, gfk_ref, gmq_ref, gmk_ref, bf_ref, scw_ref, scb_ref, gcw_ref,
                  fqn_ref, fkn_ref, mqn_ref, mkn_ref, lf_ref, fkt_ref, fvt_ref, mkt_ref, mvt_ref, lft_ref,
                  xbct_ref, zat_ref, smt_ref, sbo_ref, qkvt_ref, zct_ref, gbo_ref):
    hm = _head_mat()

    def hnorm(x, g):
        return x * lax.rsqrt(_headsum(x * x, hm) * (1.0 / HD) + EPS) * g

    fqn_ref[...] = hnorm(fq_ref[...], gfq_ref[...])
    fkn = hnorm(fk_ref[...], gfk_ref[...])
    fkn_ref[...] = fkn
    fkt_ref[...] = _tr(fkn)
    fvt_ref[...] = _tr(fv_ref[...])
    mqn_ref[...] = hnorm(mq_ref[...], gmq_ref[...])
    mkn = hnorm(mk_ref[...], gmk_ref[...])
    mkn_ref[...] = mkn
    mkt_ref[...] = _tr(mkn)
    mvt_ref[...] = _tr(mv_ref[...])
    sm = sm_ref[...]
    lf = _log_sigmoid(sm + bf_ref[...])
    lf_ref[...] = lf[:, L_FF:L_FF + HEADS]
    lft_ref[...] = _tr(lf)[L_FF:L_FF + HEADS, :]
    smt_ref[...] = _tr(sm)
    zat_ref[...] = _tr(za_ref[...])
    zct_ref[...] = _tr(zc_ref[...])

    def conv(u, buf_ref, cw):
        return cw[3:4, :] * u + cw[2:3, :] * buf_ref[2] + cw[1:2, :] * buf_ref[1] + cw[0:1, :] * buf_ref[0]

    def roll_buf(out_ref, buf_ref, u):
        out_ref[0] = buf_ref[1]
        out_ref[1] = buf_ref[2]
        out_ref[2] = u

    u = xbc_ref[...]
    xbct_ref[...] = _tr(_silu(conv(u, sbuf_ref, scw_ref[...]) + scb_ref[...]))
    roll_buf(sbo_ref, sbuf_ref, u)
    ug = qkv_ref[...]
    act = _silu(conv(ug, gbuf_ref, gcw_ref[...]))
    roll_buf(gbo_ref, gbuf_ref, ug)
    q, k = act[:, 0:GROUP], act[:, GROUP:2 * GROUP]
    qkvt_ref[0:GROUP, :] = _tr(q * lax.rsqrt(_headsum(q * q, hm) + EPS) * (HD ** -0.5))
    qkvt_ref[GROUP:2 * GROUP, :] = _tr(k * lax.rsqrt(_headsum(k * k, hm) + EPS))
    qkvt_ref[2 * GROUP:, :] = _tr(act[:, 2 * GROUP:])


def _sprep(proj_s, sbuf, gbuf, gfq, gfk, gmq, gmk, bf_full, scw, scb, gcw):
    bd = proj_s.shape[0]
    col = lambda c, w=GROUP: pl.BlockSpec((bd, w), lambda i: (0, c))
    full = lambda a: pl.BlockSpec(a.shape, lambda i: (0,) * a.ndim)
    sds = lambda *s: jax.ShapeDtypeStruct(s, F32)
    out_shape = [sds(bd, GROUP)] * 4 + [sds(bd, HEADS)] + [sds(GROUP, bd)] * 4 + [sds(HEADS, bd),
                 sds(XBC_W, bd), sds(GROUP, bd), sds(LANES, bd), sds(CONV_W - 1, bd, XBC_W),
                 sds(QKV_W, bd), sds(GROUP, bd), sds(CONV_W - 1, bd, QKV_W)]
    return pl.pallas_call(
        _sprep_kernel, grid=(1,),
        in_specs=[col(COL_FQ), col(COL_FK), col(COL_FV), col(COL_MQ), col(COL_MK), col(COL_MV),
                  col(COL_XBC, XBC_W), col(COL_QKV, QKV_W), col(COL_ZA), col(COL_ZC), col(COL_SMALL, LANES),
                  full(sbuf), full(gbuf), full(gfq), full(gfk), full(gmq), full(gmk), full(bf_full),
                  full(scw), full(scb), full(gcw)],
        out_specs=[pl.BlockSpec(s.shape, lambda i, n=len(s.shape): (0,) * n) for s in out_shape],
        out_shape=out_shape,
        compiler_params=_cparams(("arbitrary",)), name="sample_prep",
    )(proj_s, proj_s, proj_s, proj_s, proj_s, proj_s, proj_s, proj_s, proj_s, proj_s, proj_s,
      sbuf, gbuf, gfq, gfk, gmq, gmk, bf_full, scw, scb, gcw)


def _ssd_step_kernel(x_ref, b_ref, c_ref, z_ref, dtr_ref, dtb_ref, alog_ref, d_ref, st_ref, y_ref, so_ref, y_scr):
    dt = _softplus(dtr_ref[0] + dtb_ref[0])
    dec = jnp.exp(dt * (-jnp.exp(alog_ref[0])))
    xt, bt, ct = x_ref[...], b_ref[...], c_ref[...]
    xdt = xt * dt
    for p in range(HD):
        s_new = dec * st_ref[0, 0, p] + xdt[p:p + 1, :] * bt
        so_ref[0, p] = s_new
        y_scr[p:p + 1, :] = jnp.sum(s_new * ct, axis=0, keepdims=True)
    y_ref[...] = (y_scr[...] + d_ref[0] * xt) * _silu(z_ref[...])


def _ssd_step(layer, xbct, zat, smt3, state_t, dtb, alog, dvec):
    bd = xbct.shape[1]
    tile = lambda f: pl.BlockSpec((HD, bd), f)
    par = pl.BlockSpec((1, 1, bd), lambda h: (h, 0, 0))
    return pl.pallas_call(
        _ssd_step_kernel, grid=(HEADS,),
        in_specs=[tile(lambda h: (h, 0)), tile(lambda h: (HEADS + h // 2, 0)), tile(lambda h: (HEADS + 2 + h // 2, 0)),
                  tile(lambda h: (h, 0)), pl.BlockSpec((1, 1, bd), lambda h: (L_DT + h, 0, 0)), par, par, par,
                  pl.BlockSpec((1, 1, HD, HD, bd), lambda h: (layer, h, 0, 0, 0))],
        out_specs=[tile(lambda h: (h, 0)), pl.BlockSpec((1, HD, HD, bd), lambda h: (h, 0, 0, 0))],
        out_shape=[jax.ShapeDtypeStruct((GROUP, bd), F32), jax.ShapeDtypeStruct((HEADS, HD, HD, bd), F32)],
        scratch_shapes=[pltpu.VMEM((HD, bd), F32)],
        compiler_params=_cparams(("parallel",)), name="ssd_step",
    )(xbct, xbct, xbct, zat, smt3, dtb, alog, dvec, state_t)


def _gdn_step_kernel(q_ref, k_ref, v_ref, z_ref, br_ref, ar_ref, alog_ref, dtb_ref, ng_ref, st_ref, y_ref, so_ref):
    eg = jnp.exp(-jnp.exp(alog_ref[0]) * _softplus(ar_ref[0] + dtb_ref[0]))
    beta = jax.nn.sigmoid(br_ref[0])
    qt, kt, vt = q_ref[...], k_ref[...], v_ref[...]
    ks = jnp.zeros_like(vt)
    qs = jnp.zeros_like(vt)
    for dk in range(HD):
        s_old = st_ref[0, 0, dk]
        ks = ks + kt[dk:dk + 1, :] * s_old
        qs = qs + qt[dk:dk + 1, :] * s_old
    w = beta * (vt - eg * ks)
    o = eg * qs + jnp.sum(qt * kt, axis=0, keepdims=True) * w
    for dk in range(HD):
        so_ref[0, dk] = eg * st_ref[0, 0, dk] + kt[dk:dk + 1, :] * w
    on = o * lax.rsqrt(jnp.mean(o * o, axis=0, keepdims=True) + EPS) * ng_ref[...]
    y_ref[...] = on * _silu(z_ref[...])


def _gdn_step(layer, qkvt, zct, smt3, state_t, alog, dtb, ng_b):
    bd = qkvt.shape[1]
    tile = lambda f: pl.BlockSpec((HD, bd), f)
    par = pl.BlockSpec((1, 1, bd), lambda h: (h, 0, 0))
    return pl.pallas_call(
        _gdn_step_kernel, grid=(HEADS,),
        in_specs=[tile(lambda h: (h, 0)), tile(lambda h: (HEADS + h, 0)), tile(lambda h: (2 * HEADS + h, 0)),
                  tile(lambda h: (h, 0)), pl.BlockSpec((1, 1, bd), lambda h: (L_BETA + h, 0, 0)),
                  pl.BlockSpec((1, 1, bd), lambda h: (L_GA + h, 0, 0)), par, par, tile(lambda h: (0, 0)),
                  pl.BlockSpec((1, 1, HD, HD, bd), lambda h: (layer, h, 0, 0, 0))],
        out_specs=[tile(lambda h: (h, 0)), pl.BlockSpec((1, HD, HD, bd), lambda h: (h, 0, 0, 0))],
        out_shape=[jax.ShapeDtypeStruct((GROUP, bd), F32), jax.ShapeDtypeStruct((HEADS, HD, HD, bd), F32)],
        compiler_params=_cparams(("parallel",)), name="gdn_step",
    )(qkvt, qkvt, qkvt, zct, smt3, smt3, alog, dtb, ng_b, state_t)


def _outproj_kernel(ya_ref, yb_ref, yc_ref, yd_ref, w_ref, x_ref, gt_ref, o_ref):
    acc = _dot(_bf(ya_ref[...]), w_ref[0:GROUP, :])
    acc += _dot(_bf(yb_ref[...]), w_ref[GROUP:2 * GROUP, :])
    acc += _dot(_bf(yc_ref[...]), w_ref[2 * GROUP:3 * GROUP, :])
    acc += _dot(_bf(yd_ref[...]), w_ref[3 * GROUP:4 * GROUP, :])
    o_ref[...] = x_ref[...] + gt_ref[0] * acc


def _out_proj(ya, yb, yc, yd, w_bf, x2d, mod, *, tm, per_row, rows_per_batch):
    t, d = x2d.shape
    yspec = pl.BlockSpec((tm, GROUP), lambda i: (i, 0))
    return pl.pallas_call(
        _outproj_kernel, grid=(t // tm,),
        in_specs=[yspec, yspec, yspec, yspec, pl.BlockSpec((d, d), lambda i: (0, 0)),
                  pl.BlockSpec((tm, d), lambda i: (i, 0)), _mod_spec(per_row, tm, rows_per_batch, 2)],
        out_specs=pl.BlockSpec((tm, d), lambda i: (i, 0)),
        out_shape=jax.ShapeDtypeStruct((t, d), F32),
        compiler_params=_cparams(("parallel",)), name="out_proj",
    )(ya, yb, yc, yd, w_bf, x2d, mod)


def _outproj_s_kernel(yat_ref, ng_ref, yb_ref, yct_ref, yd_ref, w_ref, x_ref, gt_ref, o_ref):
    ya = yat_ref[...]
    ya = ya * lax.rsqrt(jnp.mean(ya * ya, axis=0, keepdims=True) + EPS) * ng_ref[...]
    acc = _tn3(ya, w_ref[0:GROUP, :])
    acc += _mm3(yb_ref[...], w_ref[GROUP:2 * GROUP, :])
    acc += _tn3(yct_ref[...], w_ref[2 * GROUP:3 * GROUP, :])
    acc += _mm3(yd_ref[...], w_ref[3 * GROUP:4 * GROUP, :])
    o_ref[...] = x_ref[...] + gt_ref[0] * acc


def _out_proj_s(yat, ng_b, yb, yct, yd, w_bf, x2d, mod):
    bd, d = x2d.shape
    full = lambda a: pl.BlockSpec(a.shape, lambda i: (0,) * a.ndim)
    return pl.pallas_call(
        _outproj_s_kernel, grid=(1,),
        in_specs=[full(yat), full(ng_b), full(yb), full(yct), full(yd), full(w_bf), full(x2d),
                  _mod_spec(True, bd, 1, 2)],
        out_specs=pl.BlockSpec((bd, d), lambda i: (0, 0)),
        out_shape=jax.ShapeDtypeStruct((bd, d), F32),
        compiler_params=_cparams(("arbitrary",)), name="out_proj_sample",
    )(yat, ng_b, yb, yct, yd, w_bf, x2d, mod)


def _ffn_kernel(x_ref, g_ref, sc_ref, sh_ref, gt_ref, wg_ref, wu_ref, wd_ref, o_ref, h_scr, acc_scr, *, precise):
    f, nf = pl.program_id(1), pl.num_programs(1)
    mm = _mm3 if precise else (lambda a, b: _dot(_bf(a), b))

    @pl.when(f == 0)
    def _():
        h_scr[...] = _normmod(x_ref[...], g_ref[...], sc_ref[0], sh_ref[0]).astype(h_scr.dtype)
        acc_scr[...] = jnp.zeros_like(acc_scr)

    h = h_scr[...]
    a = _silu(mm(h, wg_ref[...])) * mm(h, wu_ref[...])
    acc_scr[...] += mm(a, wd_ref[...])

    @pl.when(f == nf - 1)
    def _():
        o_ref[...] = x_ref[...] + gt_ref[0] * acc_scr[...]


def _ffn_dense(x2d, g, mod, wg, wu, wd, *, tm, tf, per_row, rows_per_batch, precise=False):
    t, d = x2d.shape
    ff = wg.shape[1]
    ms = functools.partial(_mod_spec, per_row, tm, rows_per_batch)
    return pl.pallas_call(
        functools.partial(_ffn_kernel, precise=precise), grid=(t // tm, ff // tf),
        in_specs=[pl.BlockSpec((tm, d), lambda i, f: (i, 0)), pl.BlockSpec((1, d), lambda i, f: (0, 0)),
                  ms(4), ms(3), ms(5),
                  pl.BlockSpec((d, tf), lambda i, f: (0, f)), pl.BlockSpec((d, tf), lambda i, f: (0, f)),
                  pl.BlockSpec((tf, d), lambda i, f: (f, 0))],
        out_specs=pl.BlockSpec((tm, d), lambda i, f: (i, 0)),
        out_shape=jax.ShapeDtypeStruct((t, d), F32),
        scratch_shapes=[pltpu.VMEM((tm, d), F32 if precise else BF16), pltpu.VMEM((tm, d), F32)],
        compiler_params=_cparams(("parallel", "arbitrary")), name="ffn_dense",
    )(x2d, g, mod, mod, mod, wg, wu, wd)


def _moe_kernel(x_ref, g_ref, sc_ref, sh_ref, gt_ref, r_ref, wg_ref, wu_ref, wd_ref, o_ref,
                h_scr, acc_scr, gate_scr, *, n_experts):
    e, f = pl.program_id(1), pl.program_id(2)
    ne, nf = pl.num_programs(1), pl.num_programs(2)
    tm = x_ref.shape[0]
    lane = _iota((tm, LANES), 1)

    @pl.when(jnp.logical_and(e == 0, f == 0))
    def _():
        hf = _normmod(x_ref[...], g_ref[...], sc_ref[0], sh_ref[0])
        h_scr[...] = _bf(hf)
        acc_scr[...] = jnp.zeros_like(acc_scr)
        lane_f = lane.astype(F32)
        logits = jnp.where(lane < n_experts, _mm3(hf, r_ref[...]), NEG)
        v1 = jnp.max(logits, axis=1, keepdims=True)
        i1 = jnp.min(jnp.where(logits == v1, lane_f, float(LANES)), axis=1, keepdims=True)
        rest = jnp.where(lane_f == i1, NEG, logits)
        v2 = jnp.max(rest, axis=1, keepdims=True)
        i2 = jnp.min(jnp.where(rest == v2, lane_f, float(LANES)), axis=1, keepdims=True)
        e2 = jnp.exp(v2 - v1)
        w1 = 1.0 / (1.0 + e2)
        gate_scr[...] = jnp.where(lane_f == i1, w1, jnp.where(lane_f == i2, e2 * w1, 0.0))

    h = h_scr[...]
    ge = jnp.sum(jnp.where(lane == e, gate_scr[...], 0.0), axis=1, keepdims=True)
    a = _silu(_dot(h, wg_ref[0])) * _dot(h, wu_ref[0]) * ge
    acc_scr[...] += _dot(_bf(a), wd_ref[0])

    @pl.when(jnp.logical_and(e == ne - 1, f == nf - 1))
    def _():
        o_ref[...] = x_ref[...] + gt_ref[0] * acc_scr[...]


def _ffn_moe(x2d, g, mod, router_pad, wg, wu, wd, *, tm, tf, per_row, rows_per_batch):
    t, d = x2d.shape
    ne, _, ff = wg.shape
    ms = functools.partial(_mod_spec, per_row, tm, rows_per_batch)
    return pl.pallas_call(
        functools.partial(_moe_kernel, n_experts=ne), grid=(t // tm, ne, ff // tf),
        in_specs=[pl.BlockSpec((tm, d), lambda i, e, f: (i, 0)), pl.BlockSpec((1, d), lambda i, e, f: (0, 0)),
                  ms(4), ms(3), ms(5), pl.BlockSpec((d, LANES), lambda i, e, f: (0, 0)),
                  pl.BlockSpec((1, d, tf), lambda i, e, f: (e, 0, f)), pl.BlockSpec((1, d, tf), lambda i, e, f: (e, 0, f)),
                  pl.BlockSpec((1, tf, d), lambda i, e, f: (e, f, 0))],
        out_specs=pl.BlockSpec((tm, d), lambda i, e, f: (i, 0)),
        out_shape=jax.ShapeDtypeStruct((t, d), F32),
        scratch_shapes=[pltpu.VMEM((tm, d), BF16), pltpu.VMEM((tm, d), F32), pltpu.VMEM((tm, LANES), F32)],
        compiler_params=_cparams(("parallel", "arbitrary", "arbitrary"), vmem_mb=56), name="ffn_moe",
    )(x2d, g, mod, mod, mod, router_pad, wg, wu, wd)


def _lane_row(vals, offset):
    return jnp.zeros((1, LANES), F32).at[0, offset:offset + vals.shape[0]].set(vals.astype(F32))


def _reorder_w_in(w):
    pad = jnp.zeros((w.shape[0], LANES - 16), w.dtype)
    return jnp.concatenate([w[:, 256:768], w[:, 0:256], w[:, 772:1540], w[:, 1544:2312], w[:, 2320:2576],
                            w[:, 2576:3344], w[:, 768:772], w[:, 1540:1544], w[:, 2312:2320], pad], axis=1)


def _tile_heads(g):
    return jnp.tile(g.astype(F32), HEADS).reshape(1, GROUP)


def _per_head_rows(vals, bd):
    return jnp.broadcast_to(vals.astype(F32).reshape(HEADS, 1, 1), (HEADS, 1, bd))


def kernel(x_prompt, x_sample, cache_fox_k, cache_fox_v, cache_fox_logf, cache_moba_k, cache_moba_v, state_ssm, state_ssm_conv, state_gdn, state_gdn_conv, page_table, c_prompt, c_sample, w_ada, b_ada, norm_mix, norm_ffn, w_in, w_out, ssd_conv_w, ssd_conv_b, ssd_dt_bias, ssd_a_log, ssd_d, ssd_norm, fox_b_f, fox_q_norm, fox_k_norm, gdn_conv_w, gdn_a_log, gdn_dt_bias, gdn_norm, moba_q_norm, moba_k_norm, ffn_w_gate, ffn_w_up, ffn_w_down, moe_router, moe_w_gate, moe_w_up, moe_w_down):
    bp, ln, d = x_prompt.shape
    bd = x_sample.shape[0]
    depth = w_in.shape[0]
    n_pages = page_table.shape[1]
    n_pool = cache_fox_k.shape[1]
    assert x_sample.shape[1] == 1 and d == D_MODEL and ln % MOBA_BLOCK == 0
    tp = bp * ln
    tm_p = 512 if ln % 512 == 0 else 256
    t_attn = 256

    mod = _ada_mod(jnp.concatenate([c_prompt, c_sample], axis=0), w_ada, b_ada)
    pt_flat = page_table.reshape(-1).astype(jnp.int32)
    page_t = lambda c: jnp.transpose(c, (0, 1, 3, 4, 2))
    fox_kc, fox_vc, moba_kc, moba_vc = page_t(cache_fox_k), page_t(cache_fox_v), page_t(cache_moba_k), page_t(cache_moba_v)
    fox_lfc = jnp.transpose(cache_fox_logf, (0, 1, 3, 2)).reshape(depth, n_pool, 1, HEADS * PAGE)
    ssm_t = jnp.transpose(state_ssm, (0, 2, 3, 4, 1))
    gdn_t = jnp.transpose(state_gdn, (0, 2, 3, 4, 1))
    xp = x_prompt.reshape(tp, d)
    xs = x_sample.reshape(bd, d)
    outs_p, outs_s = [], []
    for l in range(depth):
        mod_p = mod[l, :bp].reshape(bp, 1, 6 * d)
        mod_s = mod[l, bp:].reshape(1, bd, 6 * d)
        kw_p = dict(tm=tm_p, per_row=False, rows_per_batch=ln)
        kw_s = dict(tm=bd, per_row=True, rows_per_batch=1)
        g_mix = norm_mix[l].reshape(1, d)
        g_ffn = norm_ffn[l].reshape(1, d)
        w_in_f = _reorder_w_in(w_in[l])
        w_in_bf = _bf(w_in_f)
        w_out_bf = _bf(w_out[l])
        gfq, gfk = _tile_heads(fox_q_norm[l]), _tile_heads(fox_k_norm[l])
        gmq, gmk = _tile_heads(moba_q_norm[l]), _tile_heads(moba_k_norm[l])
        bf_full = _lane_row(fox_b_f[l], L_FF)
        scw, scb, gcw = ssd_conv_w[l], ssd_conv_b[l].reshape(1, XBC_W), gdn_conv_w[l]

        proj = _in_proj(xp, g_mix, mod_p, w_in_bf, tn=PROJ_COLS, **kw_p).reshape(bp, ln, PROJ_COLS)
        fqn, fkt, fvt, mqn, mkt, mvt, lft, cum, cumt, kmean = _prep(proj, gfq, gfk, gmq, gmk, bf_full, tm=MOBA_BLOCK)
        y_a, ssm_h, ssm_buf = _ssd_prompt(proj, scw, scb, _lane_row(ssd_dt_bias[l], L_DT), _lane_row(ssd_a_log[l], L_DT),
                                          _lane_row(ssd_d[l], L_DT), ssd_norm[l].reshape(1, GROUP))
        y_c, gdn_s, gdn_buf = _gdn_prompt(proj, gcw, _lane_row(gdn_a_log[l], L_GA), _lane_row(gdn_dt_bias[l], L_GA),
                                          _tile_heads(gdn_norm[l]))
        y_b = _fox_prompt(fqn, fkt, fvt, cum, cumt, t=t_attn)
        kmean_pad = jnp.pad(kmean.reshape(bp, ln // MOBA_BLOCK, GROUP), ((0, 0), (0, LANES - ln // MOBA_BLOCK), (0, 0)))
        y_d = _moba_prompt(mqn, mkt, mvt, kmean_pad)
        flat = lambda a: a.reshape(tp, GROUP)
        xp = _out_proj(flat(y_a), flat(y_b), flat(y_c), flat(y_d), w_out_bf, xp, mod_p, **kw_p)
        outs_p.append((fkt, fvt, lft, mkt, mvt, ssm_h, ssm_buf, gdn_s, gdn_buf))

        proj_s = _in_proj(xs, g_mix, mod_s, w_in_f, tn=PROJ_COLS // 9, precise=True, **kw_s)
        (sq, sk, smq, smk, slf, sfkt, sfvt, smkt, smvt, slft, xbct, zat, smt, sbo, qkvt, zct, gbo) = _sprep(
            proj_s, jnp.transpose(state_ssm_conv[l], (1, 0, 2)), jnp.transpose(state_gdn_conv[l], (1, 0, 2)),
            gfq, gfk, gmq, gmk, bf_full, scw, scb, gcw)
        smt3 = smt.reshape(LANES, 1, bd)
        yat, ssm_new = _ssd_step(l, xbct, zat, smt3, ssm_t, _per_head_rows(ssd_dt_bias[l], bd),
                                 _per_head_rows(ssd_a_log[l], bd), _per_head_rows(ssd_d[l], bd))
        yct, gdn_new = _gdn_step(l, qkvt, zct, smt3, gdn_t, _per_head_rows(gdn_a_log[l], bd),
                                 _per_head_rows(gdn_dt_bias[l], bd),
                                 jnp.broadcast_to(gdn_norm[l].astype(F32).reshape(HD, 1), (HD, bd)))
        rows = lambda a: a.reshape(bd, 1, a.shape[-1])
        ys_b = _fox_decode(pt_flat, l, rows(sq), rows(sk), rows(proj_s), rows(slf), fox_kc, fox_vc, fox_lfc[l], n_pages=n_pages)
        ys_d = _moba_decode(pt_flat, l, rows(smq), rows(smk), rows(proj_s), moba_kc, moba_vc, n_pages=n_pages)
        xs = _out_proj_s(yat, jnp.broadcast_to(ssd_norm[l].astype(F32).reshape(GROUP, 1), (GROUP, bd)),
                         ys_b.reshape(bd, GROUP), yct, ys_d.reshape(bd, GROUP), w_out[l], xs, mod_s)
        outs_s.append((sfkt, sfvt, slft, smkt, smvt, ssm_new, sbo, gdn_new, gbo))

        i = l // 2
        if l % 2 == 0:
            wg, wu, wd = _bf(ffn_w_gate[i]), _bf(ffn_w_up[i]), _bf(ffn_w_down[i])
            tf = wg.shape[1] // 2
            xp = _ffn_dense(xp, g_ffn, mod_p, wg, wu, wd, tf=tf, **kw_p)
            xs = _ffn_dense(xs, g_ffn, mod_s, ffn_w_gate[i], ffn_w_up[i], ffn_w_down[i], tf=2 * LANES, precise=True, **kw_s)
        else:
            wg, wu, wd = _bf(moe_w_gate[i]), _bf(moe_w_up[i]), _bf(moe_w_down[i])
            ne = wg.shape[0]
            router_pad = jnp.pad(moe_router[i], ((0, 0), (0, LANES - ne)))
            tf = wg.shape[2]
            xp = _ffn_moe(xp, g_ffn, mod_p, router_pad, wg, wu, wd, tf=tf, **kw_p)
            xs = _ffn_moe(xs, g_ffn, mod_s, router_pad, wg, wu, wd, tf=tf, **kw_s)

    stack = lambda outs, j: jnp.stack([o[j] for o in outs], axis=0)
    kv_p = lambda j: jnp.transpose(stack(outs_p, j), (0, 1, 4, 2, 3))
    kv_s = lambda j: jnp.transpose(stack(outs_s, j).reshape(depth, HEADS, HD, bd), (0, 3, 1, 2)).reshape(depth, bd, 1, HEADS, HD)
    st_s = lambda j: jnp.transpose(stack(outs_s, j), (0, 4, 1, 2, 3))
    buf_s = lambda j: jnp.transpose(stack(outs_s, j), (0, 2, 1, 3))
    return (xp.reshape(bp, ln, d), xs.reshape(bd, 1, d),
            kv_p(0), kv_p(1), jnp.transpose(stack(outs_p, 2), (0, 1, 3, 2)), kv_p(3), kv_p(4),
            stack(outs_p, 5), stack(outs_p, 6), stack(outs_p, 7), stack(outs_p, 8),
            kv_s(0), kv_s(1), jnp.transpose(stack(outs_s, 2), (0, 2, 1)).reshape(depth, bd, 1, HEADS), kv_s(3), kv_s(4),
            st_s(5), buf_s(6), st_s(7), buf_s(8))
```

```python
import functools

import jax
import jax.numpy as jnp
from jax import lax
from jax.experimental import pallas as pl
from jax.experimental.pallas import tpu as pltpu

F32 = jnp.float32
BF16 = jnp.bfloat16
EPS = 1e-6
NEG = -1e30

D_MODEL = 1024
GROUP = 256
HEADS = 4
HD = 64
PAGE = 128
MOBA_BLOCK = 256
MOBA_TOPK = 3
CONV_W = 4
SSD_CHUNK = 128
GDN_CHUNK = 64
GDN_TOK = 256
LANES = 128

XBC_W, QKV_W = 512, 768
COL_XBC = 0
COL_ZA = 2
COL_FQ, COL_FK, COL_FV = 3, 4, 5
COL_QKV = 2
COL_ZC = 9
COL_MQ, COL_MK, COL_MV = 10, 11, 12
COL_SMALL = 26
PROJ_COLS = 3456
L_DT, L_FF, L_BETA, L_GA = 0, 4, 8, 12


def _dot(a, b):
    return jnp.dot(a, b, preferred_element_type=F32)


def _dg(a, b, ca, cb):
    return lax.dot_general(a, b, (((ca,), (cb,)), ((), ())), preferred_element_type=F32)


def _bf(a):
    return a.astype(BF16)


def _hi_lo(a):
    hi = _bf(a)
    return hi, _bf(a - hi.astype(F32))


def _split3(a):
    a1 = _bf(a)
    r = a - a1.astype(F32)
    a2 = _bf(r)
    return a1, a2, _bf(r - a2.astype(F32))


def _mm(a, b):
    return _dot(_bf(a), _bf(b))


def _nt(a, b):
    return _dg(_bf(a), _bf(b), 1, 1)


def _tn(a, b):
    return _dg(_bf(a), _bf(b), 0, 0)


def _x3(f, a, b):
    a1, a2 = _hi_lo(a)
    b1, b2 = _hi_lo(b)
    return f(a1, b1) + (f(a1, b2) + f(a2, b1))


def _mm3(a, b):
    return _x3(_dot, a, b)


def _nt3(a, b):
    return _x3(lambda x, y: _dg(x, y, 1, 1), a, b)


def _tn3(a, b):
    return _x3(lambda x, y: _dg(x, y, 0, 0), a, b)


def _hi_lo_f32(a):
    hi = _bf(a).astype(F32)
    return hi, a - hi


def _mm3c(a, b):
    a1, a2 = _hi_lo_f32(a)
    b1, b2 = _hi_lo_f32(b)
    return _dot(_bf(jnp.concatenate([a1, a2, a1], axis=1)), _bf(jnp.concatenate([b1, b1, b2], axis=0)))


def _nt3c(a, b):
    a1, a2 = _hi_lo_f32(a)
    b1, b2 = _hi_lo_f32(b)
    return _dg(_bf(jnp.concatenate([a1, a2, a1], axis=1)), _bf(jnp.concatenate([b1, b1, b2], axis=1)), 1, 1)


def _tn3c(a, b):
    a1, a2 = _hi_lo_f32(a)
    b1, b2 = _hi_lo_f32(b)
    return _dg(_bf(jnp.concatenate([a1, a2, a1], axis=0)), _bf(jnp.concatenate([b1, b1, b2], axis=0)), 0, 0)


def _mm01(m01, x):
    x1, x2, x3 = _split3(x)
    return _dot(m01, x1) + (_dot(m01, x2) + _dot(m01, x3))


def _mmx01(x, m01):
    x1, x2, x3 = _split3(x)
    return _dot(x1, m01) + (_dot(x2, m01) + _dot(x3, m01))


def _tr(x):
    n = x.shape[1]
    eye = _bf(lax.broadcasted_iota(jnp.int32, (n, n), 0) == lax.broadcasted_iota(jnp.int32, (n, n), 1))
    x1, x2, x3 = _split3(x)
    return _dg(eye, x1, 1, 1) + (_dg(eye, x2, 1, 1) + _dg(eye, x3, 1, 1))


def _iota(shape, dim):
    return lax.broadcasted_iota(jnp.int32, shape, dim)


def _silu(x):
    return x * jax.nn.sigmoid(x)


def _softplus(x):
    return jnp.maximum(x, 0.0) + jnp.log1p(jnp.exp(-jnp.abs(x)))


def _log_sigmoid(x):
    return -_softplus(-x)


def _head_mat():
    r = lax.shift_right_logical(_iota((GROUP, GROUP), 0), 6)
    c = lax.shift_right_logical(_iota((GROUP, GROUP), 1), 6)
    return _bf(r == c)


def _headsum(x, hm):
    x1, x2 = _hi_lo(x)
    return _dot(x1, hm) + _dot(x2, hm)


def _normmod(x, g, sc, sh):
    y = x * lax.rsqrt(jnp.mean(x * x, axis=-1, keepdims=True) + EPS) * g
    return y * (1.0 + sc) + sh


def _cparams(sem, vmem_mb=48):
    return pltpu.CompilerParams(dimension_semantics=sem, vmem_limit_bytes=vmem_mb * 1024 * 1024)


def _mod_spec(per_row, tm, rows_per_batch, chunk):
    if per_row:
        return pl.BlockSpec((1, tm, D_MODEL), lambda i, *_: (0, i, chunk))
    return pl.BlockSpec((1, 1, D_MODEL), lambda i, *_: ((i * tm) // rows_per_batch, 0, chunk))


def _ada_kernel(c_ref, w_ref, b_ref, o_ref):
    o_ref[0] = _mm3(_silu(c_ref[...]), w_ref[0]) + b_ref[0]


def _ada_mod(c_all, w_ada, b_ada):
    depth, d, n = w_ada.shape
    r = c_all.shape[0]
    tn = 1536
    return pl.pallas_call(
        _ada_kernel, grid=(depth, n // tn),
        in_specs=[pl.BlockSpec((r, d), lambda l, j: (0, 0)),
                  pl.BlockSpec((1, d, tn), lambda l, j: (l, 0, j)),
                  pl.BlockSpec((1, 1, tn), lambda l, j: (l, 0, j))],
        out_specs=pl.BlockSpec((1, r, tn), lambda l, j: (l, 0, j)),
        out_shape=jax.ShapeDtypeStruct((depth, r, n), F32),
        compiler_params=_cparams(("parallel", "parallel")), name="ada_mod",
    )(c_all, w_ada, b_ada.reshape(depth, 1, n))


def _inproj_kernel(x_ref, g_ref, sc_ref, sh_ref, w_ref, o_ref, *, precise):
    h = _normmod(x_ref[...], g_ref[...], sc_ref[0], sh_ref[0])
    o_ref[...] = _mm3(h, w_ref[...]) if precise else _dot(_bf(h), w_ref[...])


def _in_proj(x2d, g, mod, w, *, tm, tn, per_row, rows_per_batch, precise=False):
    t, d = x2d.shape
    n = w.shape[1]
    return pl.pallas_call(
        functools.partial(_inproj_kernel, precise=precise), grid=(t // tm, n // tn),
        in_specs=[pl.BlockSpec((tm, d), lambda i, j: (i, 0)),
                  pl.BlockSpec((1, d), lambda i, j: (0, 0)),
                  _mod_spec(per_row, tm, rows_per_batch, 1),
                  _mod_spec(per_row, tm, rows_per_batch, 0),
                  pl.BlockSpec((d, tn), lambda i, j: (0, j))],
        out_specs=pl.BlockSpec((tm, tn), lambda i, j: (i, j)),
        out_shape=jax.ShapeDtypeStruct((t, n), F32),
        compiler_params=_cparams(("parallel", "arbitrary")), name="in_proj",
    )(x2d, g, mod, mod, w)


def _prep_kernel(fq_ref, fk_ref, fv_ref, mq_ref, mk_ref, mv_ref, sm_ref, gfq_ref, gfk_ref, gmq_ref, gmk_ref, bf_ref,
                 fqn_ref, fkt_ref, fvt_ref, mqn_ref, mkt_ref, mvt_ref, lft_ref, cum_ref, cumt_ref, kmean_ref, carry_scr):
    hm = _head_mat()

    def hnorm(x, g):
        return x * lax.rsqrt(_headsum(x * x, hm) * (1.0 / HD) + EPS) * g

    def put_t(ref, x):
        xt = _tr(x)
        for h in range(HEADS):
            ref[0, h] = xt[h * HD:(h + 1) * HD, :]

    fqn_ref[0] = hnorm(fq_ref[0], gfq_ref[...])
    put_t(fkt_ref, hnorm(fk_ref[0], gfk_ref[...]))
    put_t(fvt_ref, fv_ref[0])
    mqn_ref[0] = hnorm(mq_ref[0], gmq_ref[...])
    mkn = hnorm(mk_ref[0], gmk_ref[...])
    put_t(mkt_ref, mkn)
    put_t(mvt_ref, mv_ref[0])
    kmean_ref[0, 0] = jnp.mean(mkn, axis=0, keepdims=True)
    lf = _log_sigmoid(sm_ref[0] + bf_ref[...])
    lft_ref[0] = _tr(lf)[L_FF:L_FF + HEADS, :]

    @pl.when(pl.program_id(1) == 0)
    def _():
        carry_scr[...] = jnp.zeros_like(carry_scr)

    tm = lf.shape[0]
    tril = _bf(_iota((tm, tm), 0) >= _iota((tm, tm), 1))
    cum = _mm01(tril, lf) + carry_scr[...]
    carry_scr[...] = cum[tm - 1:tm, :]
    cum_ref[0] = cum
    cumt_ref[0] = _tr(cum)[0:8, :]


def _prep(proj3, gfq, gfk, gmq, gmk, bf_full, *, tm):
    nb, ln, _ = proj3.shape

    def col(c):
        return pl.BlockSpec((1, tm, GROUP), lambda b, i: (b, i, c))

    row = pl.BlockSpec((1, GROUP), lambda b, i: (0, 0))
    rows_blk = pl.BlockSpec((1, tm, GROUP), lambda b, i: (b, i, 0))
    t_blk = pl.BlockSpec((1, HEADS, HD, tm), lambda b, i: (b, 0, 0, i))
    rows_shape = jax.ShapeDtypeStruct((nb, ln, GROUP), F32)
    t_shape = jax.ShapeDtypeStruct((nb, HEADS, HD, ln), F32)
    return pl.pallas_call(
        _prep_kernel, grid=(nb, ln // tm),
        in_specs=[col(COL_FQ), col(COL_FK), col(COL_FV), col(COL_MQ), col(COL_MK), col(COL_MV),
                  pl.BlockSpec((1, tm, LANES), lambda b, i: (b, i, COL_SMALL)),
                  row, row, row, row, pl.BlockSpec((1, LANES), lambda b, i: (0, 0))],
        out_specs=[rows_blk, t_blk, t_blk, rows_blk, t_blk, t_blk,
                   pl.BlockSpec((1, HEADS, tm), lambda b, i: (b, 0, i)),
                   pl.BlockSpec((1, tm, LANES), lambda b, i: (b, i, 0)),
                   pl.BlockSpec((1, 8, tm), lambda b, i: (b, 0, i)),
                   pl.BlockSpec((1, 1, 1, GROUP), lambda b, i: (b, i, 0, 0))],
        out_shape=[rows_shape, t_shape, t_shape, rows_shape, t_shape, t_shape,
                   jax.ShapeDtypeStruct((nb, HEADS, ln), F32), jax.ShapeDtypeStruct((nb, ln, LANES), F32),
                   jax.ShapeDtypeStruct((nb, 8, ln), F32), jax.ShapeDtypeStruct((nb, ln // tm, 1, GROUP), F32)],
        scratch_shapes=[pltpu.VMEM((1, LANES), F32)],
        compiler_params=_cparams(("parallel", "arbitrary")), name="attn_prep",
    )(proj3, proj3, proj3, proj3, proj3, proj3, proj3, gfq, gfk, gmq, gmk, bf_full)


def _attn_update(s, vt, h, m_scr, l_scr, acc_scr):
    rep = s.shape[1] // LANES
    m_old = m_scr[h]
    m_new = jnp.maximum(m_old, jnp.max(s, axis=1, keepdims=True))
    p = jnp.exp(s - jnp.concatenate([m_new] * rep, axis=1))
    alpha = jnp.exp(m_old - m_new)
    l_scr[h] = alpha * l_scr[h] + jnp.sum(p, axis=1, keepdims=True)
    acc_scr[h] = alpha[:, 0:HD] * acc_scr[h] + _nt(p, vt)
    m_scr[h] = m_new


def _attn_init(m_scr, l_scr, acc_scr):
    m_scr[...] = jnp.full_like(m_scr, NEG)
    l_scr[...] = jnp.zeros_like(l_scr)
    acc_scr[...] = jnp.zeros_like(acc_scr)


def _attn_finish(o_ref, l_scr, acc_scr):
    for h in range(HEADS):
        o_ref[0, :, h * HD:(h + 1) * HD] = acc_scr[h] / l_scr[h][:, 0:HD]


def _attn_scratch(t):
    return [pltpu.VMEM((HEADS, t, LANES), F32), pltpu.VMEM((HEADS, t, LANES), F32), pltpu.VMEM((HEADS, t, HD), F32)]


def _causal_pairs(n):
    pairs = [(qb, kb) for qb in range(n) for kb in range(qb + 1)]
    return jnp.asarray([p[0] for p in pairs], jnp.int32), jnp.asarray([p[1] for p in pairs], jnp.int32)


def _fox_kernel(qtab_ref, ktab_ref, q_ref, kt_ref, vt_ref, cq_ref, ck_ref, o_ref, m_scr, l_scr, acc_scr, cqb_scr,
                *, t, scale):
    step_id = pl.program_id(1)
    qi, ki = qtab_ref[step_id], ktab_ref[step_id]
    rep = t // LANES

    @pl.when(ki == 0)
    def _():
        _attn_init(m_scr, l_scr, acc_scr)
        cq = cq_ref[0]
        for h in range(HEADS):
            cqb_scr[h] = jnp.broadcast_to(cq[:, L_FF + h:L_FF + h + 1], (t, LANES))

    def step(diagonal):
        q, ck = q_ref[0], ck_ref[0]
        for h in range(HEADS):
            s = _dot(_bf(q[:, h * HD:(h + 1) * HD]), _bf(kt_ref[0, h])) * scale
            s = s + (jnp.concatenate([cqb_scr[h]] * rep, axis=1) - ck[L_FF + h:L_FF + h + 1, :])
            if diagonal:
                s = jnp.where(_iota((t, t), 1) <= _iota((t, t), 0), s, NEG)
            _attn_update(s, vt_ref[0, h], h, m_scr, l_scr, acc_scr)

    @pl.when(ki < qi)
    def _():
        step(False)

    @pl.when(ki == qi)
    def _():
        step(True)
        _attn_finish(o_ref, l_scr, acc_scr)


def _fox_prompt(q, kt, vt, cum, cumt, *, t):
    b, ln, _ = q.shape
    n = ln // t
    qtab, ktab = _causal_pairs(n)
    kv = pl.BlockSpec((1, HEADS, HD, t), lambda bb, s, qt, kt_: (bb, 0, 0, kt_[s]))
    qs = pl.BlockSpec((1, t, GROUP), lambda bb, s, qt, kt_: (bb, qt[s], 0))
    return pl.pallas_call(
        functools.partial(_fox_kernel, t=t, scale=HD ** -0.5),
        grid_spec=pltpu.PrefetchScalarGridSpec(
            num_scalar_prefetch=2, grid=(b, qtab.shape[0]),
            in_specs=[qs, kv, kv,
                      pl.BlockSpec((1, t, LANES), lambda bb, s, qt, kt_: (bb, qt[s], 0)),
                      pl.BlockSpec((1, 8, t), lambda bb, s, qt, kt_: (bb, 0, kt_[s]))],
            out_specs=qs,
            scratch_shapes=_attn_scratch(t) + [pltpu.VMEM((HEADS, t, LANES), F32)]),
        out_shape=jax.ShapeDtypeStruct((b, ln, GROUP), F32),
        compiler_params=_cparams(("parallel", "arbitrary")), name="fox_prompt",
    )(qtab, ktab, q, kt, vt, cum, cumt)


def _moba_kernel(qtab_ref, jtab_ref, q_ref, kt_ref, vt_ref, km_ref, o_ref, m_scr, l_scr, acc_scr, sel_scr, *, scale):
    t = MOBA_BLOCK
    step_id = pl.program_id(1)
    qi, j = qtab_ref[step_id], jtab_ref[step_id]
    lane = _iota((t, LANES), 1)

    def rel():
        return (_iota((t, t), 0) - _iota((t, t), 1)).astype(F32)

    @pl.when(j == 0)
    def _():
        _attn_init(m_scr, l_scr, acc_scr)
        q, km = q_ref[0], km_ref[0]
        lane_f = lane.astype(F32)
        r = rel()
        for h in range(HEADS):
            sl = slice(h * HD, (h + 1) * HD)
            gate = jnp.where(lane < qi, _nt3(q[:, sl], km[:, sl]), NEG)
            sel = jnp.zeros((t, LANES), F32)
            for k in range(MOBA_TOPK):
                mx = jnp.max(gate, axis=1, keepdims=True)
                idx = jnp.min(jnp.where(gate == mx, lane_f, float(LANES)), axis=1, keepdims=True)
                hit = lane_f == idx
                sel = jnp.where(hit, jnp.maximum(sel, jnp.where(qi > k, 1.0, 0.0)), sel)
                gate = jnp.where(hit, NEG, gate)
            sel_scr[h] = sel
            s = _dot(_bf(q[:, sl]), _bf(kt_ref[0, h])) * scale - (2.0 ** (-2 * (h + 1))) * r
            _attn_update(jnp.where(r >= 0.0, s, NEG), vt_ref[0, h], h, m_scr, l_scr, acc_scr)

    @pl.when(jnp.logical_and(j >= 1, j <= qi))
    def _():
        q = q_ref[0]
        n = j - 1
        dist = rel() + ((qi - n) * t).astype(F32)
        for h in range(HEADS):
            sl = slice(h * HD, (h + 1) * HD)
            picked = jnp.sum(jnp.where(lane == n, sel_scr[h], 0.0), axis=1, keepdims=True)
            s = _dot(_bf(q[:, sl]), _bf(kt_ref[0, h])) * scale - (2.0 ** (-2 * (h + 1))) * dist
            _attn_update(jnp.where(picked > 0.5, s, NEG), vt_ref[0, h], h, m_scr, l_scr, acc_scr)

    @pl.when(j == qi)
    def _():
        _attn_finish(o_ref, l_scr, acc_scr)


def _moba_prompt(q, kt, vt, kmean_pad):
    b, ln, _ = q.shape
    t = MOBA_BLOCK
    n = ln // t

    qtab, jtab = _causal_pairs(n)

    def kv_idx(bb, s, qt, jt):
        return (bb, 0, 0, jnp.where(jt[s] == 0, qt[s], jt[s] - 1))

    qs = pl.BlockSpec((1, t, GROUP), lambda bb, s, qt, jt: (bb, qt[s], 0))
    kv = pl.BlockSpec((1, HEADS, HD, t), kv_idx)
    return pl.pallas_call(
        functools.partial(_moba_kernel, scale=HD ** -0.5),
        grid_spec=pltpu.PrefetchScalarGridSpec(
            num_scalar_prefetch=2, grid=(b, qtab.shape[0]),
            in_specs=[qs, kv, kv, pl.BlockSpec((1, LANES, GROUP), lambda bb, s, qt, jt: (bb, 0, 0))],
            out_specs=qs,
            scratch_shapes=_attn_scratch(t) + [pltpu.VMEM((HEADS, t, LANES), F32)]),
        out_shape=jax.ShapeDtypeStruct((b, ln, GROUP), F32),
        compiler_params=_cparams(("parallel", "arbitrary")), name="moba_prompt",
    )(qtab, jtab, q, kt, vt, kmean_pad)


def _query_cols(q):
    return _tr(jnp.broadcast_to(q, (LANES, GROUP)))


def _head_dots(a, b):
    ind = _bf(lax.shift_right_logical(_iota((GROUP, LANES), 0), 6) == _iota((GROUP, LANES), 1))
    return _mmx01(jnp.broadcast_to(a * b, (8, GROUP)), ind)[0:1, :]


def _lane_sums(acc):
    a1, a2, a3 = _split3(acc)
    ones = jnp.ones((8, LANES), BF16)
    return (_dg(ones, a1, 1, 1) + (_dg(ones, a2, 1, 1) + _dg(ones, a3, 1, 1)))[0:1, :]


def _decode_head(srows, s_self, v_refs, h):
    mrow = srows[0]
    for r in srows[1:]:
        mrow = jnp.maximum(mrow, r)
    m = jnp.maximum(jnp.max(mrow, axis=1, keepdims=True), s_self)
    acc = jnp.zeros((HD, LANES), F32)
    lrow = jnp.zeros((1, LANES), F32)
    for j, r in enumerate(srows):
        p = jnp.exp(r - m)
        lrow = lrow + p
        acc = acc + p * v_refs[j][0, 0, h]
    w_self = jnp.exp(s_self - m)
    return acc, w_self, jnp.sum(lrow, axis=1, keepdims=True) + w_self


def _fox_dec_kernel(pt_ref, q_ref, kn_ref, vn_ref, lfn_ref, lfc_ref, *rest, n_pages, scale):
    n = n_pages
    k_refs, v_refs = rest[:n], rest[n:2 * n]
    o_ref, qb_scr, lf_scr, acc_scr = rest[2 * n:]
    q = q_ref[0]
    qb_scr[...] = _query_cols(q)
    s_new = _head_dots(q, kn_ref[0]) * scale
    lfn = lfn_ref[0]
    first = pl.program_id(0) * n
    for j in range(n):
        row = lfc_ref[pt_ref[first + j]]
        for h in range(HEADS):
            lf_scr[h * n + j:h * n + j + 1, :] = row[:, h * PAGE:(h + 1) * PAGE]
    lf_all = lf_scr[...]
    later = _bf(_iota((PAGE, PAGE), 0) > _iota((PAGE, PAGE), 1))
    suffix = _mmx01(lf_all, later)
    page_sum = jnp.sum(lf_all, axis=1, keepdims=True)
    grp = lax.shift_right_logical(_iota((1, GROUP), 1), 6)
    w_row = jnp.zeros((1, GROUP), F32)
    l_row = jnp.ones((1, GROUP), F32)
    for h in range(HEADS):
        carry = lfn[:, h:h + 1]
        offs = [None] * n
        for j in reversed(range(n)):
            offs[j] = carry
            carry = carry + page_sum[h * n + j:h * n + j + 1, :]
        qh = qb_scr[h * HD:(h + 1) * HD, :]
        srows = [jnp.sum(qh * k_refs[j][0, 0, h], axis=0, keepdims=True) * scale
                 + (suffix[h * n + j:h * n + j + 1, :] + offs[j]) for j in range(n)]
        acc, w_self, l = _decode_head(srows, s_new[:, h:h + 1], v_refs, h)
        acc_scr[h * HD:(h + 1) * HD, :] = acc
        w_row = jnp.where(grp == h, w_self, w_row)
        l_row = jnp.where(grp == h, l, l_row)
    o_ref[0] = (_lane_sums(acc_scr[...]) + w_row * vn_ref[0]) / l_row


def _fox_decode(pt_flat, layer, q, kn, proj_s3, lfn, kc, vc, lfc, *, n_pages):
    bd = q.shape[0]
    row = pl.BlockSpec((1, 1, GROUP), lambda b, pt: (b, 0, 0))

    def page(j):
        return lambda b, pt: (layer, pt[b * n_pages + j], 0, 0, 0)

    kv_specs = [pl.BlockSpec((1, 1, HEADS, HD, PAGE), page(j)) for j in range(n_pages)]
    return pl.pallas_call(
        functools.partial(_fox_dec_kernel, n_pages=n_pages, scale=HD ** -0.5),
        grid_spec=pltpu.PrefetchScalarGridSpec(
            num_scalar_prefetch=1, grid=(bd,),
            in_specs=[row, row, pl.BlockSpec((1, 1, GROUP), lambda b, pt: (b, 0, COL_FV)),
                      pl.BlockSpec((1, 1, HEADS), lambda b, pt: (b, 0, 0)),
                      pl.BlockSpec(lfc.shape, lambda b, pt: (0, 0, 0))] + kv_specs + kv_specs,
            out_specs=row,
            scratch_shapes=[pltpu.VMEM((GROUP, LANES), F32), pltpu.VMEM((HEADS * n_pages, PAGE), F32),
                            pltpu.VMEM((GROUP, LANES), F32)]),
        out_shape=jax.ShapeDtypeStruct((bd, 1, GROUP), F32),
        compiler_params=_cparams(("parallel",)), name="fox_decode",
    )(pt_flat, q, kn, proj_s3, lfn, lfc, *([kc] * n_pages), *([vc] * n_pages))


def _moba_dec_kernel(pt_ref, q_ref, kn_ref, vn_ref, *rest, n_pages, scale):
    del pt_ref
    n = n_pages
    ppb = MOBA_BLOCK // PAGE
    n_past = n // ppb
    past_len = n * PAGE
    k_refs, v_refs = rest[:n], rest[n:2 * n]
    o_ref, qb_scr, acc_scr = rest[2 * n:]
    q = q_ref[0]
    qb_scr[...] = _query_cols(q)
    s_own = _head_dots(q, kn_ref[0]) * scale
    rows8, lanes8 = _iota((8, LANES), 0), _iota((8, LANES), 1)
    raws = []
    gates = jnp.full((8, LANES), NEG, F32)
    for h in range(HEADS):
        qh = qb_scr[h * HD:(h + 1) * HD, :]
        raw_h = [jnp.sum(qh * k_refs[j][0, 0, h], axis=0, keepdims=True) for j in range(n)]
        raws.append(raw_h)
        for blk in range(n_past):
            tot = raw_h[ppb * blk]
            for j in range(ppb * blk + 1, ppb * (blk + 1)):
                tot = tot + raw_h[j]
            gate = jnp.sum(tot, axis=1, keepdims=True) * (1.0 / MOBA_BLOCK)
            gates = jnp.where(jnp.logical_and(rows8 == h, lanes8 == blk), gate, gates)
    lanes_f = lanes8.astype(F32)
    sel = jnp.zeros((8, LANES), F32)
    for _ in range(min(MOBA_TOPK, n_past)):
        mx = jnp.max(gates, axis=1, keepdims=True)
        idx = jnp.min(jnp.where(gates == mx, lanes_f, float(LANES)), axis=1, keepdims=True)
        hit = lanes_f == idx
        sel = jnp.where(hit, 1.0, sel)
        gates = jnp.where(hit, NEG, gates)
    lane_f = _iota((1, LANES), 1).astype(F32)
    grp = lax.shift_right_logical(_iota((1, GROUP), 1), 6)
    w_row = jnp.zeros((1, GROUP), F32)
    l_row = jnp.ones((1, GROUP), F32)
    for h in range(HEADS):
        slope = 2.0 ** (-2 * (h + 1))
        srows = []
        for j in range(n):
            keep = sel[h:h + 1, j // ppb:j // ppb + 1]
            s = raws[h][j] * scale - slope * (float(past_len - j * PAGE) - lane_f)
            srows.append(s * keep + (1.0 - keep) * NEG)
        acc, w_self, l = _decode_head(srows, s_own[:, h:h + 1], v_refs, h)
        acc_scr[h * HD:(h + 1) * HD, :] = acc
        w_row = jnp.where(grp == h, w_self, w_row)
        l_row = jnp.where(grp == h, l, l_row)
    o_ref[0] = (_lane_sums(acc_scr[...]) + w_row * vn_ref[0]) / l_row


def _moba_decode(pt_flat, layer, q, kn, proj_s3, kc, vc, *, n_pages):
    bd = q.shape[0]
    assert (n_pages * PAGE) % MOBA_BLOCK == 0 and n_pages * PAGE // MOBA_BLOCK <= LANES
    row = pl.BlockSpec((1, 1, GROUP), lambda b, pt: (b, 0, 0))

    def page(j):
        return lambda b, pt: (layer, pt[b * n_pages + j], 0, 0, 0)

    kv_specs = [pl.BlockSpec((1, 1, HEADS, HD, PAGE), page(j)) for j in range(n_pages)]
    return pl.pallas_call(
        functools.partial(_moba_dec_kernel, n_pages=n_pages, scale=HD ** -0.5),
        grid_spec=pltpu.PrefetchScalarGridSpec(
            num_scalar_prefetch=1, grid=(bd,),
            in_specs=[row, row, pl.BlockSpec((1, 1, GROUP), lambda b, pt: (b, 0, COL_MV))] + kv_specs + kv_specs,
            out_specs=row,
            scratch_shapes=[pltpu.VMEM((GROUP, LANES), F32), pltpu.VMEM((GROUP, LANES), F32)]),
        out_shape=jax.ShapeDtypeStruct((bd, 1, GROUP), F32),
        compiler_params=_cparams(("parallel",)), name="moba_decode",
    )(pt_flat, q, kn, proj_s3, *([kc] * n_pages), *([vc] * n_pages))


def _chunk_conv(u, cw, ext_scr, c_len):
    ext_scr[8:8 + c_len, :] = u
    out = (cw[3:4, :] * u + cw[2:3, :] * ext_scr[7:7 + c_len, :]
           + cw[1:2, :] * ext_scr[6:6 + c_len, :] + cw[0:1, :] * ext_scr[5:5 + c_len, :])
    ext_scr[0:8, :] = ext_scr[c_len:c_len + 8, :]
    return out


def _ssd_kernel(xbc_ref, z_ref, sm_ref, cw_ref, cb_ref, dtb_ref, alog_ref, d_ref, ng_ref,
                y_ref, hs_ref, cs_ref, ext_scr, h_scr, y_scr, *, c_len):
    c, nc = pl.program_id(1), pl.num_programs(1)

    @pl.when(c == 0)
    def _():
        ext_scr[0:8, :] = jnp.zeros((8, XBC_W), F32)
        h_scr[...] = jnp.zeros_like(h_scr)

    u = xbc_ref[0]
    act = _silu(_chunk_conv(u, cw_ref[...], ext_scr, c_len) + cb_ref[...])

    @pl.when(c == nc - 1)
    def _():
        cs_ref[0] = u[c_len - (CONV_W - 1):c_len, :]

    xa, bm, cm = act[:, 0:GROUP], act[:, GROUP:GROUP + 2 * HD], act[:, GROUP + 2 * HD:]
    dt = _softplus(sm_ref[0] + dtb_ref[...])
    a_neg = -jnp.exp(alog_ref[...])
    row, colm = _iota((c_len, c_len), 0), _iota((c_len, c_len), 1)
    causal = row >= colm
    gam = _mm01(_bf(causal), dt * a_neg)
    gam_t, dt_t = _tr(gam), _tr(dt)
    glast = gam[c_len - 1:c_len, :]
    eg = jnp.exp(gam)
    wst = jnp.exp(glast - gam) * dt
    elast = jnp.exp(glast)
    dvec = d_ref[...]
    for g in range(2):
        cg, bg = cm[:, g * HD:(g + 1) * HD], bm[:, g * HD:(g + 1) * HD]
        cb = _nt(cg, bg)
        for h in (2 * g, 2 * g + 1):
            sl = slice(h * HD, (h + 1) * HD)
            hl = slice(L_DT + h, L_DT + h + 1)
            dec = jnp.exp(jnp.where(causal, gam[:, hl] - gam_t[hl, :], NEG))
            xh = xa[:, sl]
            hprev = h_scr[h]
            y_scr[:, sl] = (_mm(cb * dec * dt_t[hl, :], xh) + _nt(cg, hprev) * eg[:, hl] + dvec[:, hl] * xh)
            h_scr[h] = elast[:, hl] * hprev + _tn3c(xh * wst[:, hl], bg)
    yg = y_scr[...] * _silu(z_ref[0])
    y_ref[0] = yg * lax.rsqrt(jnp.mean(yg * yg, axis=-1, keepdims=True) + EPS) * ng_ref[...]

    @pl.when(c == nc - 1)
    def _():
        hs_ref[0] = h_scr[...]


def _ssd_prompt(proj3, cw, cb, dtb, alog, dvec, ng):
    b, ln, _ = proj3.shape
    c_len = SSD_CHUNK
    vec = lambda w: pl.BlockSpec((1, w), lambda bb, c: (0, 0))
    return pl.pallas_call(
        functools.partial(_ssd_kernel, c_len=c_len), grid=(b, ln // c_len),
        in_specs=[pl.BlockSpec((1, c_len, XBC_W), lambda bb, c: (bb, c, COL_XBC)),
                  pl.BlockSpec((1, c_len, GROUP), lambda bb, c: (bb, c, COL_ZA)),
                  pl.BlockSpec((1, c_len, LANES), lambda bb, c: (bb, c, COL_SMALL)),
                  pl.BlockSpec((CONV_W, XBC_W), lambda bb, c: (0, 0)), vec(XBC_W), vec(LANES), vec(LANES), vec(LANES),
                  vec(GROUP)],
        out_specs=[pl.BlockSpec((1, c_len, GROUP), lambda bb, c: (bb, c, 0)),
                   pl.BlockSpec((1, HEADS, HD, HD), lambda bb, c: (bb, 0, 0, 0)),
                   pl.BlockSpec((1, CONV_W - 1, XBC_W), lambda bb, c: (bb, 0, 0))],
        out_shape=[jax.ShapeDtypeStruct((b, ln, GROUP), F32), jax.ShapeDtypeStruct((b, HEADS, HD, HD), F32),
                   jax.ShapeDtypeStruct((b, CONV_W - 1, XBC_W), F32)],
        scratch_shapes=[pltpu.VMEM((c_len + 8, XBC_W), F32), pltpu.VMEM((HEADS, HD, HD), F32),
                        pltpu.VMEM((c_len, GROUP), F32)],
        compiler_params=_cparams(("parallel", "arbitrary")), name="ssd_prompt",
    )(proj3, proj3, proj3, cw, cb, dtb, alog, dvec, ng)


def _gdn_kernel(qkv_ref, z_ref, sm_ref, cw_ref, alog_ref, dtb_ref, ng_ref,
                y_ref, ss_ref, cs_ref, ext_scr, s_scr, *, tok, c_len):
    c, nc = pl.program_id(1), pl.num_programs(1)

    @pl.when(c == 0)
    def _():
        ext_scr[0:8, :] = jnp.zeros((8, QKV_W), F32)
        s_scr[...] = jnp.zeros_like(s_scr)

    u = qkv_ref[0]
    act = _silu(_chunk_conv(u, cw_ref[...], ext_scr, tok))

    @pl.when(c == nc - 1)
    def _():
        cs_ref[0] = u[tok - (CONV_W - 1):tok, :]

    hm = _head_mat()
    q, k, v = act[:, 0:GROUP], act[:, GROUP:2 * GROUP], act[:, 2 * GROUP:]
    qn = q * lax.rsqrt(_headsum(q * q, hm) + EPS) * (HD ** -0.5)
    kn = k * lax.rsqrt(_headsum(k * k, hm) + EPS)
    sm = sm_ref[0]
    beta = jax.nn.sigmoid(sm)
    gl = -jnp.exp(alog_ref[...]) * _softplus(sm + dtb_ref[...])
    shift = c_len.bit_length() - 1
    ri, ci = _iota((tok, tok), 0), _iota((tok, tok), 1)
    same_chunk = lax.shift_right_logical(ri, shift) == lax.shift_right_logical(ci, shift)
    gam = _mm01(_bf(jnp.logical_and(ri >= ci, same_chunk)), gl)
    gam_t = _tr(gam)
    eg = jnp.exp(gam)
    row, colm = _iota((c_len, c_len), 0), _iota((c_len, c_len), 1)
    incl = row >= colm
    strict = row > colm
    eye = (row == colm).astype(F32)
    z, ng = z_ref[0], ng_ref[...]
    units = [(i, h) for i in range(tok // c_len) for h in range(HEADS)]
    n_fac = c_len.bit_length() - 2

    dec, kh, qh, bcol, pinv, mpow = {}, {}, {}, {}, {}, {}
    for un in units:
        i, h = un
        r = slice(i * c_len, (i + 1) * c_len)
        g = L_GA + h
        dec[un] = jnp.exp(jnp.where(incl, gam[r, g:g + 1] - gam_t[g:g + 1, r], NEG))
        kh[un], qh[un] = kn[r, h * HD:(h + 1) * HD], qn[r, h * HD:(h + 1) * HD]
        bcol[un] = beta[r, L_BETA + h:L_BETA + h + 1]
        nmat = bcol[un] * _nt3c(kh[un], kh[un]) * jnp.where(strict, dec[un], 0.0)
        pinv[un] = eye - nmat
        mpow[un] = _mm3c(nmat, nmat)
    for lvl in range(n_fac):
        for un in units:
            if lvl + 1 < n_fac:
                both = _mm3c(jnp.concatenate([pinv[un], mpow[un]], axis=0), mpow[un])
                pinv[un], mpow[un] = pinv[un] + both[0:c_len], both[c_len:]
            else:
                pinv[un] = pinv[un] + _mm3c(pinv[un], mpow[un])
    uu, wk, qkd = {}, {}, {}
    for un in units:
        i, h = un
        r = slice(i * c_len, (i + 1) * c_len)
        g = L_GA + h
        rhs = jnp.concatenate([bcol[un] * v[r, h * HD:(h + 1) * HD], (bcol[un] * eg[r, g:g + 1]) * kh[un]], axis=1)
        sol = _mm3c(pinv[un], rhs)
        uu[un], wk[un] = sol[:, 0:HD], sol[:, HD:]
        qkd[un] = _nt(qh[un], kh[un]) * dec[un]

    for h in range(HEADS):
        sl = slice(h * HD, (h + 1) * HD)
        g = L_GA + h
        s_run = s_scr[h]
        for i in range(tok // c_len):
            un = (i, h)
            r = slice(i * c_len, (i + 1) * c_len)
            glast = gam[(i + 1) * c_len - 1:(i + 1) * c_len, g:g + 1]
            on_state = _mm3c(jnp.concatenate([wk[un], qh[un] * eg[r, g:g + 1]], axis=0), s_run)
            w = uu[un] - on_state[0:c_len]
            o = on_state[c_len:] + _mm(qkd[un], w)
            s_run = jnp.exp(glast) * s_run + _tn3c(kh[un] * jnp.exp(glast - gam[r, g:g + 1]), w)
            on = o * lax.rsqrt(jnp.mean(o * o, axis=-1, keepdims=True) + EPS) * ng[:, sl]
            y_ref[0, r, sl] = on * _silu(z[r, sl])
        s_scr[h] = s_run

    @pl.when(c == nc - 1)
    def _():
        ss_ref[0] = s_scr[...]


def _gdn_prompt(proj3, cw, alog, dtb, ng):
    b, ln, _ = proj3.shape
    tok = GDN_TOK
    vec = lambda w: pl.BlockSpec((1, w), lambda bb, c: (0, 0))
    return pl.pallas_call(
        functools.partial(_gdn_kernel, tok=tok, c_len=GDN_CHUNK), grid=(b, ln // tok),
        in_specs=[pl.BlockSpec((1, tok, QKV_W), lambda bb, c: (bb, c, COL_QKV)),
                  pl.BlockSpec((1, tok, GROUP), lambda bb, c: (bb, c, COL_ZC)),
                  pl.BlockSpec((1, tok, LANES), lambda bb, c: (bb, c, COL_SMALL)),
                  pl.BlockSpec((CONV_W, QKV_W), lambda bb, c: (0, 0)), vec(LANES), vec(LANES), vec(GROUP)],
        out_specs=[pl.BlockSpec((1, tok, GROUP), lambda bb, c: (bb, c, 0)),
                   pl.BlockSpec((1, HEADS, HD, HD), lambda bb, c: (bb, 0, 0, 0)),
                   pl.BlockSpec((1, CONV_W - 1, QKV_W), lambda bb, c: (bb, 0, 0))],
        out_shape=[jax.ShapeDtypeStruct((b, ln, GROUP), F32), jax.ShapeDtypeStruct((b, HEADS, HD, HD), F32),
                   jax.ShapeDtypeStruct((b, CONV_W - 1, QKV_W), F32)],
        scratch_shapes=[pltpu.VMEM((tok + 8, QKV_W), F32), pltpu.VMEM((HEADS, HD, HD), F32)],
        compiler_params=_cparams(("parallel", "arbitrary")), name="gdn_prompt",
    )(proj3, proj3, proj3, cw, alog, dtb, ng)


def _sprep_kernel(fq_ref, fk_ref, fv_ref, mq_ref, mk_ref, mv_ref, xbc_ref, qkv_ref, za_ref, zc_ref, sm_ref,
                  sbuf_ref, gbuf_ref, gfq_ref, gfk_ref, gmq_ref, gmk_ref, bf_ref, scw_ref, scb_ref, gcw_ref,
                  fqn_ref, fkn_ref, mqn_ref, mkn_ref, lf_ref, fkt_ref, fvt_ref, mkt_ref, mvt_ref, lft_ref,
                  xbct_ref, zat_ref, smt_ref, sbo_ref, qkvt_ref, zct_ref, gbo_ref):
    hm = _head_mat()

    def hnorm(x, g):
        return x * lax.rsqrt(_headsum(x * x, hm) * (1.0 / HD) + EPS) * g

    fqn_ref[...] = hnorm(fq_ref[...], gfq_ref[...])
    fkn = hnorm(fk_ref[...], gfk_ref[...])
    fkn_ref[...] = fkn
    fkt_ref[...] = _tr(fkn)
    fvt_ref[...] = _tr(fv_ref[...])
    mqn_ref[...] = hnorm(mq_ref[...], gmq_ref[...])
    mkn = hnorm(mk_ref[...], gmk_ref[...])
    mkn_ref[...] = mkn
    mkt_ref[...] = _tr(mkn)
    mvt_ref[...] = _tr(mv_ref[...])
    sm = sm_ref[...]
    lf = _log_sigmoid(sm + bf_ref[...])
    lf_ref[...] = lf[:, L_FF:L_FF + HEADS]
    lft_ref[...] = _tr(lf)[L_FF:L_FF + HEADS, :]
    smt_ref[...] = _tr(sm)
    zat_ref[...] = _tr(za_ref[...])
    zct_ref[...] = _tr(zc_ref[...])

    def conv(u, buf_ref, cw):
        return cw[3:4, :] * u + cw[2:3, :] * buf_ref[2] + cw[1:2, :] * buf_ref[1] + cw[0:1, :] * buf_ref[0]

    def roll_buf(out_ref, buf_ref, u):
        out_ref[0] = buf_ref[1]
        out_ref[1] = buf_ref[2]
        out_ref[2] = u

    u = xbc_ref[...]
    xbct_ref[...] = _tr(_silu(conv(u, sbuf_ref, scw_ref[...]) + scb_ref[...]))
    roll_buf(sbo_ref, sbuf_ref, u)
    ug = qkv_ref[...]
    act = _silu(conv(ug, gbuf_ref, gcw_ref[...]))
    roll_buf(gbo_ref, gbuf_ref, ug)
    q, k = act[:, 0:GROUP], act[:, GROUP:2 * GROUP]
    qkvt_ref[0:GROUP, :] = _tr(q * lax.rsqrt(_headsum(q * q, hm) + EPS) * (HD ** -0.5))
    qkvt_ref[GROUP:2 * GROUP, :] = _tr(k * lax.rsqrt(_headsum(k * k, hm) + EPS))
    qkvt_ref[2 * GROUP:, :] = _tr(act[:, 2 * GROUP:])


def _sprep(proj_s, sbuf, gbuf, gfq, gfk, gmq, gmk, bf_full, scw, scb, gcw):
    bd = proj_s.shape[0]
    col = lambda c, w=GROUP: pl.BlockSpec((bd, w), lambda i: (0, c))
    full = lambda a: pl.BlockSpec(a.shape, lambda i: (0,) * a.ndim)
    sds = lambda *s: jax.ShapeDtypeStruct(s, F32)
    out_shape = [sds(bd, GROUP)] * 4 + [sds(bd, HEADS)] + [sds(GROUP, bd)] * 4 + [sds(HEADS, bd),
                 sds(XBC_W, bd), sds(GROUP, bd), sds(LANES, bd), sds(CONV_W - 1, bd, XBC_W),
                 sds(QKV_W, bd), sds(GROUP, bd), sds(CONV_W - 1, bd, QKV_W)]
    return pl.pallas_call(
        _sprep_kernel, grid=(1,),
        in_specs=[col(COL_FQ), col(COL_FK), col(COL_FV), col(COL_MQ), col(COL_MK), col(COL_MV),
                  col(COL_XBC, XBC_W), col(COL_QKV, QKV_W), col(COL_ZA), col(COL_ZC), col(COL_SMALL, LANES),
                  full(sbuf), full(gbuf), full(gfq), full(gfk), full(gmq), full(gmk), full(bf_full),
                  full(scw), full(scb), full(gcw)],
        out_specs=[pl.BlockSpec(s.shape, lambda i, n=len(s.shape): (0,) * n) for s in out_shape],
        out_shape=out_shape,
        compiler_params=_cparams(("arbitrary",)), name="sample_prep",
    )(proj_s, proj_s, proj_s, proj_s, proj_s, proj_s, proj_s, proj_s, proj_s, proj_s, proj_s,
      sbuf, gbuf, gfq, gfk, gmq, gmk, bf_full, scw, scb, gcw)


def _ssd_step_kernel(x_ref, b_ref, c_ref, z_ref, dtr_ref, dtb_ref, alog_ref, d_ref, st_ref, y_ref, so_ref, y_scr):
    dt = _softplus(dtr_ref[0] + dtb_ref[0])
    dec = jnp.exp(dt * (-jnp.exp(alog_ref[0])))
    xt, bt, ct = x_ref[...], b_ref[...], c_ref[...]
    xdt = xt * dt
    for p in range(HD):
        s_new = dec * st_ref[0, 0, p] + xdt[p:p + 1, :] * bt
        so_ref[0, p] = s_new
        y_scr[p:p + 1, :] = jnp.sum(s_new * ct, axis=0, keepdims=True)
    y_ref[...] = (y_scr[...] + d_ref[0] * xt) * _silu(z_ref[...])


def _ssd_step(layer, xbct, zat, smt3, state_t, dtb, alog, dvec):
    bd = xbct.shape[1]
    tile = lambda f: pl.BlockSpec((HD, bd), f)
    par = pl.BlockSpec((1, 1, bd), lambda h: (h, 0, 0))
    return pl.pallas_call(
        _ssd_step_kernel, grid=(HEADS,),
        in_specs=[tile(lambda h: (h, 0)), tile(lambda h: (HEADS + h // 2, 0)), tile(lambda h: (HEADS + 2 + h // 2, 0)),
                  tile(lambda h: (h, 0)), pl.BlockSpec((1, 1, bd), lambda h: (L_DT + h, 0, 0)), par, par, par,
                  pl.BlockSpec((1, 1, HD, HD, bd), lambda h: (layer, h, 0, 0, 0))],
        out_specs=[tile(lambda h: (h, 0)), pl.BlockSpec((1, HD, HD, bd), lambda h: (h, 0, 0, 0))],
        out_shape=[jax.ShapeDtypeStruct((GROUP, bd), F32), jax.ShapeDtypeStruct((HEADS, HD, HD, bd), F32)],
        scratch_shapes=[pltpu.VMEM((HD, bd), F32)],
        compiler_params=_cparams(("parallel",)), name="ssd_step",
    )(xbct, xbct, xbct, zat, smt3, dtb, alog, dvec, state_t)


def _gdn_step_kernel(q_ref, k_ref, v_ref, z_ref, br_ref, ar_ref, alog_ref, dtb_ref, ng_ref, st_ref, y_ref, so_ref):
    eg = jnp.exp(-jnp.exp(alog_ref[0]) * _softplus(ar_ref[0] + dtb_ref[0]))
    beta = jax.nn.sigmoid(br_ref[0])
    qt, kt, vt = q_ref[...], k_ref[...], v_ref[...]
    ks = jnp.zeros_like(vt)
    qs = jnp.zeros_like(vt)
    for dk in range(HD):
        s_old = st_ref[0, 0, dk]
        ks = ks + kt[dk:dk + 1, :] * s_old
        qs = qs + qt[dk:dk + 1, :] * s_old
    w = beta * (vt - eg * ks)
    o = eg * qs + jnp.sum(qt * kt, axis=0, keepdims=True) * w
    for dk in range(HD):
        so_ref[0, dk] = eg * st_ref[0, 0, dk] + kt[dk:dk + 1, :] * w
    on = o * lax.rsqrt(jnp.mean(o * o, axis=0, keepdims=True) + EPS) * ng_ref[...]
    y_ref[...] = on * _silu(z_ref[...])


def _gdn_step(layer, qkvt, zct, smt3, state_t, alog, dtb, ng_b):
    bd = qkvt.shape[1]
    tile = lambda f: pl.BlockSpec((HD, bd), f)
    par = pl.BlockSpec((1, 1, bd), lambda h: (h, 0, 0))
    return pl.pallas_call(
        _gdn_step_kernel, grid=(HEADS,),
        in_specs=[tile(lambda h: (h, 0)), tile(lambda h: (HEADS + h, 0)), tile(lambda h: (2 * HEADS + h, 0)),
                  tile(lambda h: (h, 0)), pl.BlockSpec((1, 1, bd), lambda h: (L_BETA + h, 0, 0)),
                  pl.BlockSpec((1, 1, bd), lambda h: (L_GA + h, 0, 0)), par, par, tile(lambda h: (0, 0)),
                  pl.BlockSpec((1, 1, HD, HD, bd), lambda h: (layer, h, 0, 0, 0))],
        out_specs=[tile(lambda h: (h, 0)), pl.BlockSpec((1, HD, HD, bd), lambda h: (h, 0, 0, 0))],
        out_shape=[jax.ShapeDtypeStruct((GROUP, bd), F32), jax.ShapeDtypeStruct((HEADS, HD, HD, bd), F32)],
        compiler_params=_cparams(("parallel",)), name="gdn_step",
    )(qkvt, qkvt, qkvt, zct, smt3, smt3, alog, dtb, ng_b, state_t)


def _outproj_kernel(ya_ref, yb_ref, yc_ref, yd_ref, w_ref, x_ref, gt_ref, o_ref):
    acc = _dot(_bf(ya_ref[...]), w_ref[0:GROUP, :])
    acc += _dot(_bf(yb_ref[...]), w_ref[GROUP:2 * GROUP, :])
    acc += _dot(_bf(yc_ref[...]), w_ref[2 * GROUP:3 * GROUP, :])
    acc += _dot(_bf(yd_ref[...]), w_ref[3 * GROUP:4 * GROUP, :])
    o_ref[...] = x_ref[...] + gt_ref[0] * acc


def _out_proj(ya, yb, yc, yd, w_bf, x2d, mod, *, tm, per_row, rows_per_batch):
    t, d = x2d.shape
    yspec = pl.BlockSpec((tm, GROUP), lambda i: (i, 0))
    return pl.pallas_call(
        _outproj_kernel, grid=(t // tm,),
        in_specs=[yspec, yspec, yspec, yspec, pl.BlockSpec((d, d), lambda i: (0, 0)),
                  pl.BlockSpec((tm, d), lambda i: (i, 0)), _mod_spec(per_row, tm, rows_per_batch, 2)],
        out_specs=pl.BlockSpec((tm, d), lambda i: (i, 0)),
        out_shape=jax.ShapeDtypeStruct((t, d), F32),
        compiler_params=_cparams(("parallel",)), name="out_proj",
    )(ya, yb, yc, yd, w_bf, x2d, mod)


def _outproj_s_kernel(yat_ref, ng_ref, yb_ref, yct_ref, yd_ref, w_ref, x_ref, gt_ref, o_ref):
    ya = yat_ref[...]
    ya = ya * lax.rsqrt(jnp.mean(ya * ya, axis=0, keepdims=True) + EPS) * ng_ref[...]
    acc = _tn3(ya, w_ref[0:GROUP, :])
    acc += _mm3(yb_ref[...], w_ref[GROUP:2 * GROUP, :])
    acc += _tn3(yct_ref[...], w_ref[2 * GROUP:3 * GROUP, :])
    acc += _mm3(yd_ref[...], w_ref[3 * GROUP:4 * GROUP, :])
    o_ref[...] = x_ref[...] + gt_ref[0] * acc


def _out_proj_s(yat, ng_b, yb, yct, yd, w_bf, x2d, mod):
    bd, d = x2d.shape
    full = lambda a: pl.BlockSpec(a.shape, lambda i: (0,) * a.ndim)
    return pl.pallas_call(
        _outproj_s_kernel, grid=(1,),
        in_specs=[full(yat), full(ng_b), full(yb), full(yct), full(yd), full(w_bf), full(x2d),
                  _mod_spec(True, bd, 1, 2)],
        out_specs=pl.BlockSpec((bd, d), lambda i: (0, 0)),
        out_shape=jax.ShapeDtypeStruct((bd, d), F32),
        compiler_params=_cparams(("arbitrary",)), name="out_proj_sample",
    )(yat, ng_b, yb, yct, yd, w_bf, x2d, mod)


def _ffn_kernel(x_ref, g_ref, sc_ref, sh_ref, gt_ref, wg_ref, wu_ref, wd_ref, o_ref, h_scr, acc_scr, *, precise):
    f, nf = pl.program_id(1), pl.num_programs(1)
    mm = _mm3 if precise else (lambda a, b: _dot(_bf(a), b))

    @pl.when(f == 0)
    def _():
        h_scr[...] = _normmod(x_ref[...], g_ref[...], sc_ref[0], sh_ref[0]).astype(h_scr.dtype)
        acc_scr[...] = jnp.zeros_like(acc_scr)

    h = h_scr[...]
    a = _silu(mm(h, wg_ref[...])) * mm(h, wu_ref[...])
    acc_scr[...] += mm(a, wd_ref[...])

    @pl.when(f == nf - 1)
    def _():
        o_ref[...] = x_ref[...] + gt_ref[0] * acc_scr[...]


def _ffn_dense(x2d, g, mod, wg, wu, wd, *, tm, tf, per_row, rows_per_batch, precise=False):
    t, d = x2d.shape
    ff = wg.shape[1]
    ms = functools.partial(_mod_spec, per_row, tm, rows_per_batch)
    return pl.pallas_call(
        functools.partial(_ffn_kernel, precise=precise), grid=(t // tm, ff // tf),
        in_specs=[pl.BlockSpec((tm, d), lambda i, f: (i, 0)), pl.BlockSpec((1, d), lambda i, f: (0, 0)),
                  ms(4), ms(3), ms(5),
                  pl.BlockSpec((d, tf), lambda i, f: (0, f)), pl.BlockSpec((d, tf), lambda i, f: (0, f)),
                  pl.BlockSpec((tf, d), lambda i, f: (f, 0))],
        out_specs=pl.BlockSpec((tm, d), lambda i, f: (i, 0)),
        out_shape=jax.ShapeDtypeStruct((t, d), F32),
        scratch_shapes=[pltpu.VMEM((tm, d), F32 if precise else BF16), pltpu.VMEM((tm, d), F32)],
        compiler_params=_cparams(("parallel", "arbitrary")), name="ffn_dense",
    )(x2d, g, mod, mod, mod, wg, wu, wd)


def _moe_kernel(x_ref, g_ref, sc_ref, sh_ref, gt_ref, r_ref, wg_ref, wu_ref, wd_ref, o_ref,
                h_scr, acc_scr, gate_scr, gatet_scr, *, n_experts, cap):
    e, f = pl.program_id(1), pl.program_id(2)
    ne, nf = pl.num_programs(1), pl.num_programs(2)
    tm = x_ref.shape[0]
    lane = _iota((tm, LANES), 1)

    @pl.when(jnp.logical_and(e == 0, f == 0))
    def _():
        hf = _normmod(x_ref[...], g_ref[...], sc_ref[0], sh_ref[0])
        h_scr[...] = _bf(hf)
        acc_scr[...] = jnp.zeros_like(acc_scr)
        lane_f = lane.astype(F32)
        logits = jnp.where(lane < n_experts, _mm3(hf, r_ref[...]), NEG)
        v1 = jnp.max(logits, axis=1, keepdims=True)
        i1 = jnp.min(jnp.where(logits == v1, lane_f, float(LANES)), axis=1, keepdims=True)
        rest = jnp.where(lane_f == i1, NEG, logits)
        v2 = jnp.max(rest, axis=1, keepdims=True)
        i2 = jnp.min(jnp.where(rest == v2, lane_f, float(LANES)), axis=1, keepdims=True)
        e2 = jnp.exp(v2 - v1)
        w1 = 1.0 / (1.0 + e2)
        gates = jnp.where(lane_f == i1, w1, jnp.where(lane_f == i2, e2 * w1, 0.0))
        gate_scr[...] = gates
        gatet_scr[...] = _tr(gates)

    h = h_scr[...]
    ge = jnp.sum(jnp.where(lane == e, gate_scr[...], 0.0), axis=1, keepdims=True)
    ge_row = gatet_scr[pl.ds(e, 1), :]
    on_col = jnp.where(ge != 0.0, 1.0, 0.0)
    on_row = jnp.where(ge_row != 0.0, 1.0, 0.0)
    count = jnp.sum(on_col)
    ri, ci = _iota((tm, tm), 0), _iota((tm, tm), 1)

    @pl.when(count <= cap)
    def _():
        rank_col = _dot(_bf(ri > ci), _bf(jnp.broadcast_to(on_col, (tm, LANES))))[:, 0:1]
        rank_row = _dot(_bf(jnp.broadcast_to(on_row, (8, tm))), _bf(ri < ci))[0:1, :]
        slot_r = _iota((cap, tm), 0).astype(F32)
        pick = _bf(jnp.where(jnp.logical_and(rank_row == slot_r, on_row > 0.5), 1.0, 0.0))
        slot_c = _iota((tm, cap), 1).astype(F32)
        spread = _bf(jnp.where(jnp.logical_and(rank_col == slot_c, on_col > 0.5), 1.0, 0.0))
        hc = _bf(_dot(pick, h))
        gc = _mm01(pick, jnp.broadcast_to(ge, (tm, LANES)))[:, 0:1]
        a = _silu(_dot(hc, wg_ref[0])) * _dot(hc, wu_ref[0])
        yh, yl = _hi_lo(_dot(_bf(a), wd_ref[0]) * gc)
        acc_scr[...] += _dot(spread, yh) + _dot(spread, yl)

    @pl.when(count > cap)
    def _():
        a = _silu(_dot(h, wg_ref[0])) * _dot(h, wu_ref[0]) * ge
        acc_scr[...] += _dot(_bf(a), wd_ref[0])

    @pl.when(jnp.logical_and(e == ne - 1, f == nf - 1))
    def _():
        o_ref[...] = x_ref[...] + gt_ref[0] * acc_scr[...]


def _ffn_moe(x2d, g, mod, router_pad, wg, wu, wd, *, tm, tf, per_row, rows_per_batch):
    t, d = x2d.shape
    ne, _, ff = wg.shape
    ms = functools.partial(_mod_spec, per_row, tm, rows_per_batch)
    return pl.pallas_call(
        functools.partial(_moe_kernel, n_experts=ne, cap=-(-tm * 3 // 8 // 16) * 16), grid=(t // tm, ne, ff // tf),
        in_specs=[pl.BlockSpec((tm, d), lambda i, e, f: (i, 0)), pl.BlockSpec((1, d), lambda i, e, f: (0, 0)),
                  ms(4), ms(3), ms(5), pl.BlockSpec((d, LANES), lambda i, e, f: (0, 0)),
                  pl.BlockSpec((1, d, tf), lambda i, e, f: (e, 0, f)), pl.BlockSpec((1, d, tf), lambda i, e, f: (e, 0, f)),
                  pl.BlockSpec((1, tf, d), lambda i, e, f: (e, f, 0))],
        out_specs=pl.BlockSpec((tm, d), lambda i, e, f: (i, 0)),
        out_shape=jax.ShapeDtypeStruct((t, d), F32),
        scratch_shapes=[pltpu.VMEM((tm, d), BF16), pltpu.VMEM((tm, d), F32), pltpu.VMEM((tm, LANES), F32),
                        pltpu.VMEM((LANES, tm), F32)],
        compiler_params=_cparams(("parallel", "arbitrary", "arbitrary"), vmem_mb=56), name="ffn_moe",
    )(x2d, g, mod, mod, mod, router_pad, wg, wu, wd)


def _lane_row(vals, offset):
    return jnp.zeros((1, LANES), F32).at[0, offset:offset + vals.shape[0]].set(vals.astype(F32))


def _reorder_w_in(w):
    pad = jnp.zeros((w.shape[0], LANES - 16), w.dtype)
    return jnp.concatenate([w[:, 256:768], w[:, 0:256], w[:, 772:1540], w[:, 1544:2312], w[:, 2320:2576],
                            w[:, 2576:3344], w[:, 768:772], w[:, 1540:1544], w[:, 2312:2320], pad], axis=1)


def _tile_heads(g):
    return jnp.tile(g.astype(F32), HEADS).reshape(1, GROUP)


def _per_head_rows(vals, bd):
    return jnp.broadcast_to(vals.astype(F32).reshape(HEADS, 1, 1), (HEADS, 1, bd))


def kernel(x_prompt, x_sample, cache_fox_k, cache_fox_v, cache_fox_logf, cache_moba_k, cache_moba_v, state_ssm, state_ssm_conv, state_gdn, state_gdn_conv, page_table, c_prompt, c_sample, w_ada, b_ada, norm_mix, norm_ffn, w_in, w_out, ssd_conv_w, ssd_conv_b, ssd_dt_bias, ssd_a_log, ssd_d, ssd_norm, fox_b_f, fox_q_norm, fox_k_norm, gdn_conv_w, gdn_a_log, gdn_dt_bias, gdn_norm, moba_q_norm, moba_k_norm, ffn_w_gate, ffn_w_up, ffn_w_down, moe_router, moe_w_gate, moe_w_up, moe_w_down):
    bp, ln, d = x_prompt.shape
    bd = x_sample.shape[0]
    depth = w_in.shape[0]
    n_pages = page_table.shape[1]
    n_pool = cache_fox_k.shape[1]
    assert x_sample.shape[1] == 1 and d == D_MODEL and ln % MOBA_BLOCK == 0
    tp = bp * ln
    tm_p = 512 if ln % 512 == 0 else 256
    t_attn = 256

    mod = _ada_mod(jnp.concatenate([c_prompt, c_sample], axis=0), w_ada, b_ada)
    pt_flat = page_table.reshape(-1).astype(jnp.int32)
    page_t = lambda c: jnp.transpose(c, (0, 1, 3, 4, 2))
    fox_kc, fox_vc, moba_kc, moba_vc = page_t(cache_fox_k), page_t(cache_fox_v), page_t(cache_moba_k), page_t(cache_moba_v)
    fox_lfc = jnp.transpose(cache_fox_logf, (0, 1, 3, 2)).reshape(depth, n_pool, 1, HEADS * PAGE)
    ssm_t = jnp.transpose(state_ssm, (0, 2, 3, 4, 1))
    gdn_t = jnp.transpose(state_gdn, (0, 2, 3, 4, 1))
    xp = x_prompt.reshape(tp, d)
    xs = x_sample.reshape(bd, d)
    outs_p, outs_s = [], []
    for l in range(depth):
        mod_p = mod[l, :bp].reshape(bp, 1, 6 * d)
        mod_s = mod[l, bp:].reshape(1, bd, 6 * d)
        kw_p = dict(tm=tm_p, per_row=False, rows_per_batch=ln)
        kw_s = dict(tm=bd, per_row=True, rows_per_batch=1)
        g_mix = norm_mix[l].reshape(1, d)
        g_ffn = norm_ffn[l].reshape(1, d)
        w_in_f = _reorder_w_in(w_in[l])
        w_in_bf = _bf(w_in_f)
        w_out_bf = _bf(w_out[l])
        gfq, gfk = _tile_heads(fox_q_norm[l]), _tile_heads(fox_k_norm[l])
        gmq, gmk = _tile_heads(moba_q_norm[l]), _tile_heads(moba_k_norm[l])
        bf_full = _lane_row(fox_b_f[l], L_FF)
        scw, scb, gcw = ssd_conv_w[l], ssd_conv_b[l].reshape(1, XBC_W), gdn_conv_w[l]

        proj = _in_proj(xp, g_mix, mod_p, w_in_bf, tn=PROJ_COLS, **kw_p).reshape(bp, ln, PROJ_COLS)
        fqn, fkt, fvt, mqn, mkt, mvt, lft, cum, cumt, kmean = _prep(proj, gfq, gfk, gmq, gmk, bf_full, tm=MOBA_BLOCK)
        y_a, ssm_h, ssm_buf = _ssd_prompt(proj, scw, scb, _lane_row(ssd_dt_bias[l], L_DT), _lane_row(ssd_a_log[l], L_DT),
                                          _lane_row(ssd_d[l], L_DT), ssd_norm[l].reshape(1, GROUP))
        y_c, gdn_s, gdn_buf = _gdn_prompt(proj, gcw, _lane_row(gdn_a_log[l], L_GA), _lane_row(gdn_dt_bias[l], L_GA),
                                          _tile_heads(gdn_norm[l]))
        y_b = _fox_prompt(fqn, fkt, fvt, cum, cumt, t=t_attn)
        kmean_pad = jnp.pad(kmean.reshape(bp, ln // MOBA_BLOCK, GROUP), ((0, 0), (0, LANES - ln // MOBA_BLOCK), (0, 0)))
        y_d = _moba_prompt(mqn, mkt, mvt, kmean_pad)
        flat = lambda a: a.reshape(tp, GROUP)
        xp = _out_proj(flat(y_a), flat(y_b), flat(y_c), flat(y_d), w_out_bf, xp, mod_p, **kw_p)
        outs_p.append((fkt, fvt, lft, mkt, mvt, ssm_h, ssm_buf, gdn_s, gdn_buf))

        proj_s = _in_proj(xs, g_mix, mod_s, w_in_f, tn=PROJ_COLS // 9, precise=True, **kw_s)
        (sq, sk, smq, smk, slf, sfkt, sfvt, smkt, smvt, slft, xbct, zat, smt, sbo, qkvt, zct, gbo) = _sprep(
            proj_s, jnp.transpose(state_ssm_conv[l], (1, 0, 2)), jnp.transpose(state_gdn_conv[l], (1, 0, 2)),
            gfq, gfk, gmq, gmk, bf_full, scw, scb, gcw)
        smt3 = smt.reshape(LANES, 1, bd)
        yat, ssm_new = _ssd_step(l, xbct, zat, smt3, ssm_t, _per_head_rows(ssd_dt_bias[l], bd),
                                 _per_head_rows(ssd_a_log[l], bd), _per_head_rows(ssd_d[l], bd))
        yct, gdn_new = _gdn_step(l, qkvt, zct, smt3, gdn_t, _per_head_rows(gdn_a_log[l], bd),
                                 _per_head_rows(gdn_dt_bias[l], bd),
                                 jnp.broadcast_to(gdn_norm[l].astype(F32).reshape(HD, 1), (HD, bd)))
        rows = lambda a: a.reshape(bd, 1, a.shape[-1])
        ys_b = _fox_decode(pt_flat, l, rows(sq), rows(sk), rows(proj_s), rows(slf), fox_kc, fox_vc, fox_lfc[l], n_pages=n_pages)
        ys_d = _moba_decode(pt_flat, l, rows(smq), rows(smk), rows(proj_s), moba_kc, moba_vc, n_pages=n_pages)
        xs = _out_proj_s(yat, jnp.broadcast_to(ssd_norm[l].astype(F32).reshape(GROUP, 1), (GROUP, bd)),
                         ys_b.reshape(bd, GROUP), yct, ys_d.reshape(bd, GROUP), w_out[l], xs, mod_s)
        outs_s.append((sfkt, sfvt, slft, smkt, smvt, ssm_new, sbo, gdn_new, gbo))

        i = l // 2
        if l % 2 == 0:
            wg, wu, wd = _bf(ffn_w_gate[i]), _bf(ffn_w_up[i]), _bf(ffn_w_down[i])
            tf = wg.shape[1] // 2
            xp = _ffn_dense(xp, g_ffn, mod_p, wg, wu, wd, tf=tf, **kw_p)
            xs = _ffn_dense(xs, g_ffn, mod_s, ffn_w_gate[i], ffn_w_up[i], ffn_w_down[i], tf=2 * LANES, precise=True, **kw_s)
        else:
            wg, wu, wd = _bf(moe_w_gate[i]), _bf(moe_w_up[i]), _bf(moe_w_down[i])
            ne = wg.shape[0]
            router_pad = jnp.pad(moe_router[i], ((0, 0), (0, LANES - ne)))
            tf = wg.shape[2]
            xp = _ffn_moe(xp, g_ffn, mod_p, router_pad, wg, wu, wd, tf=tf, **kw_p)
            xs = _ffn_moe(xs, g_ffn, mod_s, router_pad, wg, wu, wd, tf=tf, **kw_s)

    stack = lambda outs, j: jnp.stack([o[j] for o in outs], axis=0)
    kv_p = lambda j: jnp.transpose(stack(outs_p, j), (0, 1, 4, 2, 3))
    kv_s = lambda j: jnp.transpose(stack(outs_s, j).reshape(depth, HEADS, HD, bd), (0, 3, 1, 2)).reshape(depth, bd, 1, HEADS, HD)
    st_s = lambda j: jnp.transpose(stack(outs_s, j), (0, 4, 1, 2, 3))
    buf_s = lambda j: jnp.transpose(stack(outs_s, j), (0, 2, 1, 3))
    return (xp.reshape(bp, ln, d), xs.reshape(bd, 1, d),
            kv_p(0), kv_p(1), jnp.transpose(stack(outs_p, 2), (0, 1, 3, 2)), kv_p(3), kv_p(4),
            stack(outs_p, 5), stack(outs_p, 6), stack(outs_p, 7), stack(outs_p, 8),
            kv_s(0), kv_s(1), jnp.transpose(stack(outs_s, 2), (0, 2, 1)).reshape(depth, bd, 1, HEADS), kv_s(3), kv_s(4),
            st_s(5), buf_s(6), st_s(7), buf_s(8))
```

```python
import functools

import jax
import jax.numpy as jnp
from jax import lax
from jax.experimental import pallas as pl
from jax.experimental.pallas import tpu as pltpu

F32 = jnp.float32
BF16 = jnp.bfloat16
EPS = 1e-6
NEG = -1e30

D_MODEL = 1024
GROUP = 256
HEADS = 4
HD = 64
PAGE = 128
MOBA_BLOCK = 256
MOBA_TOPK = 3
CONV_W = 4
SSD_CHUNK = 128
GDN_CHUNK = 64
GDN_TOK = 256
LANES = 128

XBC_W, QKV_W = 512, 768
COL_XBC = 0
COL_ZA = 2
COL_FQ, COL_FK, COL_FV = 3, 4, 5
COL_QKV = 2
COL_ZC = 9
COL_MQ, COL_MK, COL_MV = 10, 11, 12
COL_SMALL = 26
PROJ_COLS = 3456
L_DT, L_FF, L_BETA, L_GA = 0, 4, 8, 12


def _dot(a, b):
    return jnp.dot(a, b, preferred_element_type=F32)


def _dg(a, b, ca, cb):
    return lax.dot_general(a, b, (((ca,), (cb,)), ((), ())), preferred_element_type=F32)


def _bf(a):
    return a.astype(BF16)


def _hi_lo(a):
    hi = _bf(a)
    return hi, _bf(a - hi.astype(F32))


def _split3(a):
    a1 = _bf(a)
    r = a - a1.astype(F32)
    a2 = _bf(r)
    return a1, a2, _bf(r - a2.astype(F32))


def _mm(a, b):
    return _dot(_bf(a), _bf(b))


def _nt(a, b):
    return _dg(_bf(a), _bf(b), 1, 1)


def _tn(a, b):
    return _dg(_bf(a), _bf(b), 0, 0)


def _x3(f, a, b):
    a1, a2 = _hi_lo(a)
    b1, b2 = _hi_lo(b)
    return f(a1, b1) + (f(a1, b2) + f(a2, b1))


def _mm3(a, b):
    return _x3(_dot, a, b)


def _nt3(a, b):
    return _x3(lambda x, y: _dg(x, y, 1, 1), a, b)


def _tn3(a, b):
    return _x3(lambda x, y: _dg(x, y, 0, 0), a, b)


def _hi_lo_f32(a):
    hi = _bf(a).astype(F32)
    return hi, a - hi


def _mm3c(a, b):
    a1, a2 = _hi_lo_f32(a)
    b1, b2 = _hi_lo_f32(b)
    return _dot(_bf(jnp.concatenate([a1, a2, a1], axis=1)), _bf(jnp.concatenate([b1, b1, b2], axis=0)))


def _nt3c(a, b):
    a1, a2 = _hi_lo_f32(a)
    b1, b2 = _hi_lo_f32(b)
    return _dg(_bf(jnp.concatenate([a1, a2, a1], axis=1)), _bf(jnp.concatenate([b1, b1, b2], axis=1)), 1, 1)


def _tn3c(a, b):
    a1, a2 = _hi_lo_f32(a)
    b1, b2 = _hi_lo_f32(b)
    return _dg(_bf(jnp.concatenate([a1, a2, a1], axis=0)), _bf(jnp.concatenate([b1, b1, b2], axis=0)), 0, 0)


def _mm01(m01, x):
    x1, x2, x3 = _split3(x)
    return _dot(m01, x1) + (_dot(m01, x2) + _dot(m01, x3))


def _mmx01(x, m01):
    x1, x2, x3 = _split3(x)
    return _dot(x1, m01) + (_dot(x2, m01) + _dot(x3, m01))


def _tr(x):
    n = x.shape[1]
    eye = _bf(lax.broadcasted_iota(jnp.int32, (n, n), 0) == lax.broadcasted_iota(jnp.int32, (n, n), 1))
    x1, x2, x3 = _split3(x)
    return _dg(eye, x1, 1, 1) + (_dg(eye, x2, 1, 1) + _dg(eye, x3, 1, 1))


def _iota(shape, dim):
    return lax.broadcasted_iota(jnp.int32, shape, dim)


def _silu(x):
    return x * jax.nn.sigmoid(x)


def _softplus(x):
    return jnp.maximum(x, 0.0) + jnp.log1p(jnp.exp(-jnp.abs(x)))


def _log_sigmoid(x):
    return -_softplus(-x)


def _head_mat():
    r = lax.shift_right_logical(_iota((GROUP, GROUP), 0), 6)
    c = lax.shift_right_logical(_iota((GROUP, GROUP), 1), 6)
    return _bf(r == c)


def _headsum(x, hm):
    x1, x2 = _hi_lo(x)
    return _dot(x1, hm) + _dot(x2, hm)


def _normmod(x, g, sc, sh):
    y = x * lax.rsqrt(jnp.mean(x * x, axis=-1, keepdims=True) + EPS) * g
    return y * (1.0 + sc) + sh


def _cparams(sem, vmem_mb=48):
    return pltpu.CompilerParams(dimension_semantics=sem, vmem_limit_bytes=vmem_mb * 1024 * 1024)


def _mod_spec(per_row, tm, rows_per_batch, chunk):
    if per_row:
        return pl.BlockSpec((1, tm, D_MODEL), lambda i, *_: (0, i, chunk))
    return pl.BlockSpec((1, 1, D_MODEL), lambda i, *_: ((i * tm) // rows_per_batch, 0, chunk))


def _ada_kernel(c_ref, w_ref, b_ref, o_ref):
    o_ref[0] = _mm3(_silu(c_ref[...]), w_ref[0]) + b_ref[0]


def _ada_mod(c_all, w_ada, b_ada):
    depth, d, n = w_ada.shape
    r = c_all.shape[0]
    tn = 1536
    return pl.pallas_call(
        _ada_kernel, grid=(depth, n // tn),
        in_specs=[pl.BlockSpec((r, d), lambda l, j: (0, 0)),
                  pl.BlockSpec((1, d, tn), lambda l, j: (l, 0, j)),
                  pl.BlockSpec((1, 1, tn), lambda l, j: (l, 0, j))],
        out_specs=pl.BlockSpec((1, r, tn), lambda l, j: (l, 0, j)),
        out_shape=jax.ShapeDtypeStruct((depth, r, n), F32),
        compiler_params=_cparams(("parallel", "parallel")), name="ada_mod",
    )(c_all, w_ada, b_ada.reshape(depth, 1, n))


def _inproj_kernel(x_ref, g_ref, sc_ref, sh_ref, w_ref, o_ref, *, precise):
    h = _normmod(x_ref[...], g_ref[...], sc_ref[0], sh_ref[0])
    o_ref[...] = _mm3(h, w_ref[...]) if precise else _dot(_bf(h), w_ref[...])


def _in_proj(x2d, g, mod, w, *, tm, tn, per_row, rows_per_batch, precise=False):
    t, d = x2d.shape
    n = w.shape[1]
    return pl.pallas_call(
        functools.partial(_inproj_kernel, precise=precise), grid=(t // tm, n // tn),
        in_specs=[pl.BlockSpec((tm, d), lambda i, j: (i, 0)),
                  pl.BlockSpec((1, d), lambda i, j: (0, 0)),
                  _mod_spec(per_row, tm, rows_per_batch, 1),
                  _mod_spec(per_row, tm, rows_per_batch, 0),
                  pl.BlockSpec((d, tn), lambda i, j: (0, j))],
        out_specs=pl.BlockSpec((tm, tn), lambda i, j: (i, j)),
        out_shape=jax.ShapeDtypeStruct((t, n), F32),
        compiler_params=_cparams(("parallel", "arbitrary")), name="in_proj",
    )(x2d, g, mod, mod, w)


def _split3_f32(a):
    a1 = _bf(a).astype(F32)
    r = a - a1
    a2 = _bf(r).astype(F32)
    return a1, a2, _bf(r - a2).astype(F32)


def _query_tiles(qn, extra_fn, scale):
    return jnp.concatenate([jnp.concatenate([qn[:, h * HD:(h + 1) * HD] * scale, extra_fn(h)], axis=1)
                            for h in range(HEADS)], axis=1)


def _prep_kernel(fq_ref, fk_ref, fv_ref, mq_ref, mk_ref, mv_ref, sm_ref, gfq_ref, gfk_ref, gmq_ref, gmk_ref, bf_ref,
                 fqa_ref, fkt_ref, fkx_ref, fvt_ref, mqa_ref, mkt_ref, mvt_ref, lft_ref, kmean_ref, carry_scr):
    hm = _head_mat()
    tm = sm_ref.shape[1]
    scale = HD ** -0.5

    def hnorm(x, g):
        return x * lax.rsqrt(_headsum(x * x, hm) * (1.0 / HD) + EPS) * g

    def put_t(ref, x):
        xt = _tr(x)
        for h in range(HEADS):
            ref[0, h] = xt[h * HD:(h + 1) * HD, :]

    put_t(fkt_ref, hnorm(fk_ref[0], gfk_ref[...]))
    put_t(fvt_ref, fv_ref[0])
    mkn = hnorm(mk_ref[0], gmk_ref[...])
    put_t(mkt_ref, mkn)
    put_t(mvt_ref, mv_ref[0])
    kmean_ref[0, 0] = jnp.mean(mkn, axis=0, keepdims=True)
    lf = _log_sigmoid(sm_ref[0] + bf_ref[...])
    lft_ref[0] = _tr(lf)[L_FF:L_FF + HEADS, :]

    @pl.when(pl.program_id(1) == 0)
    def _():
        carry_scr[...] = jnp.zeros_like(carry_scr)

    tril = _bf(_iota((tm, tm), 0) >= _iota((tm, tm), 1))
    cum = _mm01(tril, lf) + carry_scr[...]
    carry_scr[...] = cum[tm - 1:tm, :]
    cum_t = _tr(cum)
    lane64 = _iota((tm, HD), 1)
    row8 = _iota((8, tm), 0)

    def fox_extra(h):
        c1, c2, c3 = _split3_f32(cum[:, L_FF + h:L_FF + h + 1])
        return jnp.where(lane64 == 0, c1, jnp.where(lane64 == 1, c2, jnp.where(lane64 == 2, c3,
                         jnp.where(lane64 < 6, 1.0, 0.0))))

    fqa_ref[0] = _query_tiles(hnorm(fq_ref[0], gfq_ref[...]), fox_extra, scale)
    for h in range(HEADS):
        k1, k2, k3 = _split3_f32(cum_t[L_FF + h:L_FF + h + 1, :])
        fkx_ref[0, h] = jnp.where(row8 < 3, 1.0, jnp.where(row8 == 3, -k1, jnp.where(row8 == 4, -k2,
                                  jnp.where(row8 == 5, -k3, 0.0))))

    q_hi = (pl.program_id(1) * tm).astype(F32)
    q_lo = _iota((tm, HD), 0).astype(F32)

    def moba_extra(h):
        slope = 2.0 ** (-2 * (h + 1))
        return jnp.where(lane64 == 0, -slope * q_hi, jnp.where(lane64 == 1, -slope * q_lo,
                         jnp.where(lane64 < 4, slope, 0.0)))

    mqa_ref[0] = _query_tiles(hnorm(mq_ref[0], gmq_ref[...]), moba_extra, scale)


def _prep(proj3, gfq, gfk, gmq, gmk, bf_full, *, tm):
    nb, ln, _ = proj3.shape

    def col(c):
        return pl.BlockSpec((1, tm, GROUP), lambda b, i: (b, i, c))

    assert tm == MOBA_BLOCK
    row = pl.BlockSpec((1, GROUP), lambda b, i: (0, 0))
    q_blk = pl.BlockSpec((1, tm, HEADS * LANES), lambda b, i: (b, i, 0))
    t_blk = pl.BlockSpec((1, HEADS, HD, tm), lambda b, i: (b, 0, 0, i))
    q_shape = jax.ShapeDtypeStruct((nb, ln, HEADS * LANES), F32)
    t_shape = jax.ShapeDtypeStruct((nb, HEADS, HD, ln), F32)
    return pl.pallas_call(
        _prep_kernel, grid=(nb, ln // tm),
        in_specs=[col(COL_FQ), col(COL_FK), col(COL_FV), col(COL_MQ), col(COL_MK), col(COL_MV),
                  pl.BlockSpec((1, tm, LANES), lambda b, i: (b, i, COL_SMALL)),
                  row, row, row, row, pl.BlockSpec((1, LANES), lambda b, i: (0, 0))],
        out_specs=[q_blk, t_blk, pl.BlockSpec((1, HEADS, 8, tm), lambda b, i: (b, 0, 0, i)), t_blk, q_blk, t_blk, t_blk,
                   pl.BlockSpec((1, HEADS, tm), lambda b, i: (b, 0, i)),
                   pl.BlockSpec((1, 1, 1, GROUP), lambda b, i: (b, i, 0, 0))],
        out_shape=[q_shape, t_shape, jax.ShapeDtypeStruct((nb, HEADS, 8, ln), F32), t_shape, q_shape, t_shape, t_shape,
                   jax.ShapeDtypeStruct((nb, HEADS, ln), F32), jax.ShapeDtypeStruct((nb, ln // tm, 1, GROUP), F32)],
        scratch_shapes=[pltpu.VMEM((1, LANES), F32)],
        compiler_params=_cparams(("parallel", "arbitrary")), name="attn_prep",
    )(proj3, proj3, proj3, proj3, proj3, proj3, proj3, gfq, gfk, gmq, gmk, bf_full)


def _attn_update(s, vt, h, m_scr, l_scr, acc_scr):
    rep = s.shape[1] // LANES
    m_old = m_scr[h]
    m_new = jnp.maximum(m_old, jnp.max(s, axis=1, keepdims=True))
    p = jnp.exp(s - jnp.concatenate([m_new] * rep, axis=1))
    alpha = jnp.exp(m_old - m_new)
    l_scr[h] = alpha * l_scr[h] + jnp.sum(p, axis=1, keepdims=True)
    acc_scr[h] = alpha[:, 0:HD] * acc_scr[h] + _nt(p, vt)
    m_scr[h] = m_new


def _attn_init(m_scr, l_scr, acc_scr):
    m_scr[...] = jnp.full_like(m_scr, NEG)
    l_scr[...] = jnp.zeros_like(l_scr)
    acc_scr[...] = jnp.zeros_like(acc_scr)


def _attn_finish(o_ref, l_scr, acc_scr):
    for h in range(HEADS):
        o_ref[0, :, h * HD:(h + 1) * HD] = acc_scr[h] / l_scr[h][:, 0:HD]


def _attn_scratch(t):
    return [pltpu.VMEM((HEADS, t, LANES), F32), pltpu.VMEM((HEADS, t, LANES), F32), pltpu.VMEM((HEADS, t, HD), F32)]


def _causal_pairs(n):
    pairs = [(qb, kb) for qb in range(n) for kb in range(qb + 1)]
    return jnp.asarray([p[0] for p in pairs], jnp.int32), jnp.asarray([p[1] for p in pairs], jnp.int32)


def _key_tile(kt, extra):
    return jnp.concatenate([kt, extra, jnp.zeros((LANES - HD - 8, kt.shape[1]), F32)], axis=0)


def _fox_kernel(qtab_ref, ktab_ref, q_ref, kt_ref, kx_ref, vt_ref, o_ref, m_scr, l_scr, acc_scr, *, t, wide):
    step_id = pl.program_id(1)
    qi, ki = qtab_ref[step_id], ktab_ref[step_id]
    last = qi // wide

    @pl.when(ki == 0)
    def _():
        _attn_init(m_scr, l_scr, acc_scr)

    def step(diagonal):
        q = q_ref[0]
        for h in range(HEADS):
            s = _dot(_bf(q[:, h * LANES:(h + 1) * LANES]), _bf(_key_tile(kt_ref[0, h], kx_ref[0, h])))
            if diagonal:
                shift = (qi - last * wide) * t
                s = jnp.where(_iota((t, wide * t), 1) <= _iota((t, wide * t), 0) + shift, s, NEG)
            _attn_update(s, vt_ref[0, h], h, m_scr, l_scr, acc_scr)

    @pl.when(ki < last)
    def _():
        step(False)

    @pl.when(ki == last)
    def _():
        step(True)
        _attn_finish(o_ref, l_scr, acc_scr)


def _fox_prompt(qa, kt, kx, vt, *, t, wide):
    b, ln, _ = qa.shape
    assert ln % (wide * t) == 0
    pairs = [(qb, kb) for qb in range(ln // t) for kb in range(qb // wide + 1)]
    qtab = jnp.asarray([p[0] for p in pairs], jnp.int32)
    ktab = jnp.asarray([p[1] for p in pairs], jnp.int32)
    kv = pl.BlockSpec((1, HEADS, HD, wide * t), lambda bb, s, qt, kt_: (bb, 0, 0, kt_[s]))
    return pl.pallas_call(
        functools.partial(_fox_kernel, t=t, wide=wide),
        grid_spec=pltpu.PrefetchScalarGridSpec(
            num_scalar_prefetch=2, grid=(b, len(pairs)),
            in_specs=[pl.BlockSpec((1, t, HEADS * LANES), lambda bb, s, qt, kt_: (bb, qt[s], 0)), kv,
                      pl.BlockSpec((1, HEADS, 8, wide * t), lambda bb, s, qt, kt_: (bb, 0, 0, kt_[s])), kv],
            out_specs=pl.BlockSpec((1, t, GROUP), lambda bb, s, qt, kt_: (bb, qt[s], 0)),
            scratch_shapes=_attn_scratch(t)),
        out_shape=jax.ShapeDtypeStruct((b, ln, GROUP), F32),
        compiler_params=_cparams(("parallel", "arbitrary")), name="fox_prompt",
    )(qtab, ktab, qa, kt, kx, vt)


def _moba_kernel(qtab_ref, jtab_ref, q_ref, kt_ref, vt_ref, km_ref, o_ref, m_scr, l_scr, acc_scr, sel_scr):
    t = MOBA_BLOCK
    step_id = pl.program_id(1)
    qi, j = qtab_ref[step_id], jtab_ref[step_id]
    lane = _iota((t, LANES), 1)
    row8 = _iota((8, t), 0)

    def key_extra(blk):
        k_hi = (blk * t).astype(F32)
        k_lo = _iota((8, t), 1).astype(F32)
        return jnp.where(row8 < 2, 1.0, jnp.where(row8 == 2, k_hi, jnp.where(row8 == 3, k_lo, 0.0)))

    @pl.when(j == 0)
    def _():
        _attn_init(m_scr, l_scr, acc_scr)
        q, km = q_ref[0], km_ref[0]
        lane_f = lane.astype(F32)
        causal = _iota((t, t), 1) <= _iota((t, t), 0)
        kx = key_extra(qi)
        for h in range(HEADS):
            tl = slice(h * LANES, (h + 1) * LANES)
            gate = jnp.where(lane < qi, _nt3(q[:, tl], km[:, tl]), NEG)
            sel = jnp.zeros((t, LANES), F32)
            for k in range(MOBA_TOPK):
                mx = jnp.max(gate, axis=1, keepdims=True)
                idx = jnp.min(jnp.where(gate == mx, lane_f, float(LANES)), axis=1, keepdims=True)
                hit = lane_f == idx
                sel = jnp.where(hit, jnp.maximum(sel, jnp.where(qi > k, 1.0, 0.0)), sel)
                gate = jnp.where(hit, NEG, gate)
            sel_scr[h] = sel
            s = _dot(_bf(q[:, tl]), _bf(_key_tile(kt_ref[0, h], kx)))
            _attn_update(jnp.where(causal, s, NEG), vt_ref[0, h], h, m_scr, l_scr, acc_scr)

    @pl.when(jnp.logical_and(j >= 1, j <= qi))
    def _():
        q = q_ref[0]
        n = j - 1
        kx = key_extra(n)
        for h in range(HEADS):
            picked = jnp.sum(jnp.where(lane == n, sel_scr[h], 0.0), axis=1, keepdims=True)
            s = _dot(_bf(q[:, h * LANES:(h + 1) * LANES]), _bf(_key_tile(kt_ref[0, h], kx)))
            _attn_update(jnp.where(picked > 0.5, s, NEG), vt_ref[0, h], h, m_scr, l_scr, acc_scr)

    @pl.when(j == qi)
    def _():
        _attn_finish(o_ref, l_scr, acc_scr)


def _moba_prompt(qa, kt, vt, kmean_tiles):
    b, ln, _ = qa.shape
    t = MOBA_BLOCK
    n = ln // t
    qtab, jtab = _causal_pairs(n)

    def kv_idx(bb, s, qt, jt):
        return (bb, 0, 0, jnp.where(jt[s] == 0, qt[s], jt[s] - 1))

    kv = pl.BlockSpec((1, HEADS, HD, t), kv_idx)
    return pl.pallas_call(
        _moba_kernel,
        grid_spec=pltpu.PrefetchScalarGridSpec(
            num_scalar_prefetch=2, grid=(b, qtab.shape[0]),
            in_specs=[pl.BlockSpec((1, t, HEADS * LANES), lambda bb, s, qt, jt: (bb, qt[s], 0)), kv, kv,
                      pl.BlockSpec((1, LANES, HEADS * LANES), lambda bb, s, qt, jt: (bb, 0, 0))],
            out_specs=pl.BlockSpec((1, t, GROUP), lambda bb, s, qt, jt: (bb, qt[s], 0)),
            scratch_shapes=_attn_scratch(t) + [pltpu.VMEM((HEADS, t, LANES), F32)]),
        out_shape=jax.ShapeDtypeStruct((b, ln, GROUP), F32),
        compiler_params=_cparams(("parallel", "arbitrary")), name="moba_prompt",
    )(qtab, jtab, qa, kt, vt, kmean_tiles)


def _query_cols(q):
    return _tr(jnp.broadcast_to(q, (LANES, GROUP)))


def _head_dots(a, b):
    ind = _bf(lax.shift_right_logical(_iota((GROUP, LANES), 0), 6) == _iota((GROUP, LANES), 1))
    return _mmx01(jnp.broadcast_to(a * b, (8, GROUP)), ind)[0:1, :]


def _lane_sums(acc):
    a1, a2, a3 = _split3(acc)
    ones = jnp.ones((8, LANES), BF16)
    return (_dg(ones, a1, 1, 1) + (_dg(ones, a2, 1, 1) + _dg(ones, a3, 1, 1)))[0:1, :]


def _decode_head(srows, s_self, v_refs, h):
    mrow = srows[0]
    for r in srows[1:]:
        mrow = jnp.maximum(mrow, r)
    m = jnp.maximum(jnp.max(mrow, axis=1, keepdims=True), s_self)
    acc = jnp.zeros((HD, LANES), F32)
    lrow = jnp.zeros((1, LANES), F32)
    for j, r in enumerate(srows):
        p = jnp.exp(r - m)
        lrow = lrow + p
        acc = acc + p * v_refs[j][0, 0, h]
    w_self = jnp.exp(s_self - m)
    return acc, w_self, jnp.sum(lrow, axis=1, keepdims=True) + w_self


def _fox_dec_kernel(pt_ref, q_ref, kn_ref, vn_ref, lfn_ref, lfc_ref, *rest, n_pages, scale):
    n = n_pages
    k_refs, v_refs = rest[:n], rest[n:2 * n]
    o_ref, qb_scr, lf_scr, acc_scr = rest[2 * n:]
    q = q_ref[0]
    qb_scr[...] = _query_cols(q)
    s_new = _head_dots(q, kn_ref[0]) * scale
    lfn = lfn_ref[0]
    first = pl.program_id(0) * n
    for j in range(n):
        row = lfc_ref[pt_ref[first + j]]
        for h in range(HEADS):
            lf_scr[h * n + j:h * n + j + 1, :] = row[:, h * PAGE:(h + 1) * PAGE]
    lf_all = lf_scr[...]
    later = _bf(_iota((PAGE, PAGE), 0) > _iota((PAGE, PAGE), 1))
    suffix = _mmx01(lf_all, later)
    page_sum = jnp.sum(lf_all, axis=1, keepdims=True)
    grp = lax.shift_right_logical(_iota((1, GROUP), 1), 6)
    w_row = jnp.zeros((1, GROUP), F32)
    l_row = jnp.ones((1, GROUP), F32)
    for h in range(HEADS):
        carry = lfn[:, h:h + 1]
        offs = [None] * n
        for j in reversed(range(n)):
            offs[j] = carry
            carry = carry + page_sum[h * n + j:h * n + j + 1, :]
        qh = qb_scr[h * HD:(h + 1) * HD, :]
        srows = [jnp.sum(qh * k_refs[j][0, 0, h], axis=0, keepdims=True) * scale
                 + (suffix[h * n + j:h * n + j + 1, :] + offs[j]) for j in range(n)]
        acc, w_self, l = _decode_head(srows, s_new[:, h:h + 1], v_refs, h)
        acc_scr[h * HD:(h + 1) * HD, :] = acc
        w_row = jnp.where(grp == h, w_self, w_row)
        l_row = jnp.where(grp == h, l, l_row)
    o_ref[0] = (_lane_sums(acc_scr[...]) + w_row * vn_ref[0]) / l_row


def _fox_decode(pt_flat, layer, q, kn, proj_s3, lfn, kc, vc, lfc, *, n_pages):
    bd = q.shape[0]
    row = pl.BlockSpec((1, 1, GROUP), lambda b, pt: (b, 0, 0))

    def page(j):
        return lambda b, pt: (layer, pt[b * n_pages + j], 0, 0, 0)

    kv_specs = [pl.BlockSpec((1, 1, HEADS, HD, PAGE), page(j)) for j in range(n_pages)]
    return pl.pallas_call(
        functools.partial(_fox_dec_kernel, n_pages=n_pages, scale=HD ** -0.5),
        grid_spec=pltpu.PrefetchScalarGridSpec(
            num_scalar_prefetch=1, grid=(bd,),
            in_specs=[row, row, pl.BlockSpec((1, 1, GROUP), lambda b, pt: (b, 0, COL_FV)),
                      pl.BlockSpec((1, 1, HEADS), lambda b, pt: (b, 0, 0)),
                      pl.BlockSpec(lfc.shape, lambda b, pt: (0, 0, 0))] + kv_specs + kv_specs,
            out_specs=row,
            scratch_shapes=[pltpu.VMEM((GROUP, LANES), F32), pltpu.VMEM((HEADS * n_pages, PAGE), F32),
                            pltpu.VMEM((GROUP, LANES), F32)]),
        out_shape=jax.ShapeDtypeStruct((bd, 1, GROUP), F32),
        compiler_params=_cparams(("parallel",)), name="fox_decode",
    )(pt_flat, q, kn, proj_s3, lfn, lfc, *([kc] * n_pages), *([vc] * n_pages))


def _moba_dec_kernel(pt_ref, q_ref, kn_ref, vn_ref, *rest, n_pages, scale):
    del pt_ref
    n = n_pages
    ppb = MOBA_BLOCK // PAGE
    n_past = n // ppb
    past_len = n * PAGE
    k_refs, v_refs = rest[:n], rest[n:2 * n]
    o_ref, qb_scr, acc_scr = rest[2 * n:]
    q = q_ref[0]
    qb_scr[...] = _query_cols(q)
    s_own = _head_dots(q, kn_ref[0]) * scale
    rows8, lanes8 = _iota((8, LANES), 0), _iota((8, LANES), 1)
    raws = []
    gates = jnp.full((8, LANES), NEG, F32)
    for h in range(HEADS):
        qh = qb_scr[h * HD:(h + 1) * HD, :]
        raw_h = [jnp.sum(qh * k_refs[j][0, 0, h], axis=0, keepdims=True) for j in range(n)]
        raws.append(raw_h)
        for blk in range(n_past):
            tot = raw_h[ppb * blk]
            for j in range(ppb * blk + 1, ppb * (blk + 1)):
                tot = tot + raw_h[j]
            gate = jnp.sum(tot, axis=1, keepdims=True) * (1.0 / MOBA_BLOCK)
            gates = jnp.where(jnp.logical_and(rows8 == h, lanes8 == blk), gate, gates)
    lanes_f = lanes8.astype(F32)
    sel = jnp.zeros((8, LANES), F32)
    for _ in range(min(MOBA_TOPK, n_past)):
        mx = jnp.max(gates, axis=1, keepdims=True)
        idx = jnp.min(jnp.where(gates == mx, lanes_f, float(LANES)), axis=1, keepdims=True)
        hit = lanes_f == idx
        sel = jnp.where(hit, 1.0, sel)
        gates = jnp.where(hit, NEG, gates)
    lane_f = _iota((1, LANES), 1).astype(F32)
    grp = lax.shift_right_logical(_iota((1, GROUP), 1), 6)
    w_row = jnp.zeros((1, GROUP), F32)
    l_row = jnp.ones((1, GROUP), F32)
    for h in range(HEADS):
        slope = 2.0 ** (-2 * (h + 1))
        srows = []
        for j in range(n):
            keep = sel[h:h + 1, j // ppb:j // ppb + 1]
            s = raws[h][j] * scale - slope * (float(past_len - j * PAGE) - lane_f)
            srows.append(s * keep + (1.0 - keep) * NEG)
        acc, w_self, l = _decode_head(srows, s_own[:, h:h + 1], v_refs, h)
        acc_scr[h * HD:(h + 1) * HD, :] = acc
        w_row = jnp.where(grp == h, w_self, w_row)
        l_row = jnp.where(grp == h, l, l_row)
    o_ref[0] = (_lane_sums(acc_scr[...]) + w_row * vn_ref[0]) / l_row


def _moba_decode(pt_flat, layer, q, kn, proj_s3, kc, vc, *, n_pages):
    bd = q.shape[0]
    assert (n_pages * PAGE) % MOBA_BLOCK == 0 and n_pages * PAGE // MOBA_BLOCK <= LANES
    row = pl.BlockSpec((1, 1, GROUP), lambda b, pt: (b, 0, 0))

    def page(j):
        return lambda b, pt: (layer, pt[b * n_pages + j], 0, 0, 0)

    kv_specs = [pl.BlockSpec((1, 1, HEADS, HD, PAGE), page(j)) for j in range(n_pages)]
    return pl.pallas_call(
        functools.partial(_moba_dec_kernel, n_pages=n_pages, scale=HD ** -0.5),
        grid_spec=pltpu.PrefetchScalarGridSpec(
            num_scalar_prefetch=1, grid=(bd,),
            in_specs=[row, row, pl.BlockSpec((1, 1, GROUP), lambda b, pt: (b, 0, COL_MV))] + kv_specs + kv_specs,
            out_specs=row,
            scratch_shapes=[pltpu.VMEM((GROUP, LANES), F32), pltpu.VMEM((GROUP, LANES), F32)]),
        out_shape=jax.ShapeDtypeStruct((bd, 1, GROUP), F32),
        compiler_params=_cparams(("parallel",)), name="moba_decode",
    )(pt_flat, q, kn, proj_s3, *([kc] * n_pages), *([vc] * n_pages))


def _chunk_conv(u, cw, ext_scr, c_len):
    ext_scr[8:8 + c_len, :] = u
    out = (cw[3:4, :] * u + cw[2:3, :] * ext_scr[7:7 + c_len, :]
           + cw[1:2, :] * ext_scr[6:6 + c_len, :] + cw[0:1, :] * ext_scr[5:5 + c_len, :])
    ext_scr[0:8, :] = ext_scr[c_len:c_len + 8, :]
    return out


def _ssd_kernel(xbc_ref, z_ref, sm_ref, cw_ref, cb_ref, dtb_ref, alog_ref, d_ref, ng_ref,
                y_ref, hs_ref, cs_ref, ext_scr, h_scr, y_scr, *, c_len):
    c, nc = pl.program_id(1), pl.num_programs(1)

    @pl.when(c == 0)
    def _():
        ext_scr[0:8, :] = jnp.zeros((8, XBC_W), F32)
        h_scr[...] = jnp.zeros_like(h_scr)

    u = xbc_ref[0]
    act = _silu(_chunk_conv(u, cw_ref[...], ext_scr, c_len) + cb_ref[...])

    @pl.when(c == nc - 1)
    def _():
        cs_ref[0] = u[c_len - (CONV_W - 1):c_len, :]

    xa, bm, cm = act[:, 0:GROUP], act[:, GROUP:GROUP + 2 * HD], act[:, GROUP + 2 * HD:]
    dt = _softplus(sm_ref[0] + dtb_ref[...])
    a_neg = -jnp.exp(alog_ref[...])
    row, colm = _iota((c_len, c_len), 0), _iota((c_len, c_len), 1)
    causal = row >= colm
    gam = _mm01(_bf(causal), dt * a_neg)
    gam_t, dt_t = _tr(gam), _tr(dt)
    glast = gam[c_len - 1:c_len, :]
    eg = jnp.exp(gam)
    wst = jnp.exp(glast - gam) * dt
    elast = jnp.exp(glast)
    dvec = d_ref[...]
    for g in range(2):
        cg, bg = cm[:, g * HD:(g + 1) * HD], bm[:, g * HD:(g + 1) * HD]
        cb = _nt(cg, bg)
        for h in (2 * g, 2 * g + 1):
            sl = slice(h * HD, (h + 1) * HD)
            hl = slice(L_DT + h, L_DT + h + 1)
            dec = jnp.exp(jnp.where(causal, gam[:, hl] - gam_t[hl, :], NEG))
            xh = xa[:, sl]
            hprev = h_scr[h]
            y_scr[:, sl] = (_mm(cb * dec * dt_t[hl, :], xh) + _nt(cg, hprev) * eg[:, hl] + dvec[:, hl] * xh)
            h_scr[h] = elast[:, hl] * hprev + _tn3c(xh * wst[:, hl], bg)
    yg = y_scr[...] * _silu(z_ref[0])
    y_ref[0] = yg * lax.rsqrt(jnp.mean(yg * yg, axis=-1, keepdims=True) + EPS) * ng_ref[...]

    @pl.when(c == nc - 1)
    def _():
        hs_ref[0] = h_scr[...]


def _ssd_prompt(proj3, cw, cb, dtb, alog, dvec, ng):
    b, ln, _ = proj3.shape
    c_len = SSD_CHUNK
    vec = lambda w: pl.BlockSpec((1, w), lambda bb, c: (0, 0))
    return pl.pallas_call(
        functools.partial(_ssd_kernel, c_len=c_len), grid=(b, ln // c_len),
        in_specs=[pl.BlockSpec((1, c_len, XBC_W), lambda bb, c: (bb, c, COL_XBC)),
                  pl.BlockSpec((1, c_len, GROUP), lambda bb, c: (bb, c, COL_ZA)),
                  pl.BlockSpec((1, c_len, LANES), lambda bb, c: (bb, c, COL_SMALL)),
                  pl.BlockSpec((CONV_W, XBC_W), lambda bb, c: (0, 0)), vec(XBC_W), vec(LANES), vec(LANES), vec(LANES),
                  vec(GROUP)],
        out_specs=[pl.BlockSpec((1, c_len, GROUP), lambda bb, c: (bb, c, 0)),
                   pl.BlockSpec((1, HEADS, HD, HD), lambda bb, c: (bb, 0, 0, 0)),
                   pl.BlockSpec((1, CONV_W - 1, XBC_W), lambda bb, c: (bb, 0, 0))],
        out_shape=[jax.ShapeDtypeStruct((b, ln, GROUP), F32), jax.ShapeDtypeStruct((b, HEADS, HD, HD), F32),
                   jax.ShapeDtypeStruct((b, CONV_W - 1, XBC_W), F32)],
        scratch_shapes=[pltpu.VMEM((c_len + 8, XBC_W), F32), pltpu.VMEM((HEADS, HD, HD), F32),
                        pltpu.VMEM((c_len, GROUP), F32)],
        compiler_params=_cparams(("parallel", "arbitrary")), name="ssd_prompt",
    )(proj3, proj3, proj3, cw, cb, dtb, alog, dvec, ng)


def _gdn_kernel(qkv_ref, z_ref, sm_ref, cw_ref, alog_ref, dtb_ref, ng_ref,
                y_ref, ss_ref, cs_ref, ext_scr, s_scr, *, tok, c_len):
    c, nc = pl.program_id(1), pl.num_programs(1)

    @pl.when(c == 0)
    def _():
        ext_scr[0:8, :] = jnp.zeros((8, QKV_W), F32)
        s_scr[...] = jnp.zeros_like(s_scr)

    u = qkv_ref[0]
    act = _silu(_chunk_conv(u, cw_ref[...], ext_scr, tok))

    @pl.when(c == nc - 1)
    def _():
        cs_ref[0] = u[tok - (CONV_W - 1):tok, :]

    hm = _head_mat()
    q, k, v = act[:, 0:GROUP], act[:, GROUP:2 * GROUP], act[:, 2 * GROUP:]
    qn = q * lax.rsqrt(_headsum(q * q, hm) + EPS) * (HD ** -0.5)
    kn = k * lax.rsqrt(_headsum(k * k, hm) + EPS)
    sm = sm_ref[0]
    beta = jax.nn.sigmoid(sm)
    gl = -jnp.exp(alog_ref[...]) * _softplus(sm + dtb_ref[...])
    shift = c_len.bit_length() - 1
    ri, ci = _iota((tok, tok), 0), _iota((tok, tok), 1)
    same_chunk = lax.shift_right_logical(ri, shift) == lax.shift_right_logical(ci, shift)
    gam = _mm01(_bf(jnp.logical_and(ri >= ci, same_chunk)), gl)
    gam_t = _tr(gam)
    eg = jnp.exp(gam)
    row, colm = _iota((c_len, c_len), 0), _iota((c_len, c_len), 1)
    incl = row >= colm
    strict = row > colm
    eye = (row == colm).astype(F32)
    z, ng = z_ref[0], ng_ref[...]
    units = [(i, h) for i in range(tok // c_len) for h in range(HEADS)]
    n_fac = c_len.bit_length() - 2

    dec, kh, qh, bcol, pinv, mpow = {}, {}, {}, {}, {}, {}
    for un in units:
        i, h = un
        r = slice(i * c_len, (i + 1) * c_len)
        g = L_GA + h
        dec[un] = jnp.exp(jnp.where(incl, gam[r, g:g + 1] - gam_t[g:g + 1, r], NEG))
        kh[un], qh[un] = kn[r, h * HD:(h + 1) * HD], qn[r, h * HD:(h + 1) * HD]
        bcol[un] = beta[r, L_BETA + h:L_BETA + h + 1]
        nmat = bcol[un] * _nt3c(kh[un], kh[un]) * jnp.where(strict, dec[un], 0.0)
        pinv[un] = eye - nmat
        mpow[un] = _mm3c(nmat, nmat)
    for lvl in range(n_fac):
        for un in units:
            if lvl + 1 < n_fac:
                both = _mm3c(jnp.concatenate([pinv[un], mpow[un]], axis=0), mpow[un])
                pinv[un], mpow[un] = pinv[un] + both[0:c_len], both[c_len:]
            else:
                pinv[un] = pinv[un] + _mm3c(pinv[un], mpow[un])
    uu, wk, qkd = {}, {}, {}
    for un in units:
        i, h = un
        r = slice(i * c_len, (i + 1) * c_len)
        g = L_GA + h
        rhs = jnp.concatenate([bcol[un] * v[r, h * HD:(h + 1) * HD], (bcol[un] * eg[r, g:g + 1]) * kh[un]], axis=1)
        sol = _mm3c(pinv[un], rhs)
        uu[un], wk[un] = sol[:, 0:HD], sol[:, HD:]
        qkd[un] = _nt(qh[un], kh[un]) * dec[un]

    for h in range(HEADS):
        sl = slice(h * HD, (h + 1) * HD)
        g = L_GA + h
        s_run = s_scr[h]
        for i in range(tok // c_len):
            un = (i, h)
            r = slice(i * c_len, (i + 1) * c_len)
            glast = gam[(i + 1) * c_len - 1:(i + 1) * c_len, g:g + 1]
            on_state = _mm3c(jnp.concatenate([wk[un], qh[un] * eg[r, g:g + 1]], axis=0), s_run)
            w = uu[un] - on_state[0:c_len]
            o = on_state[c_len:] + _mm(qkd[un], w)
            s_run = jnp.exp(glast) * s_run + _tn3c(kh[un] * jnp.exp(glast - gam[r, g:g + 1]), w)
            on = o * lax.rsqrt(jnp.mean(o * o, axis=-1, keepdims=True) + EPS) * ng[:, sl]
            y_ref[0, r, sl] = on * _silu(z[r, sl])
        s_scr[h] = s_run

    @pl.when(c == nc - 1)
    def _():
        ss_ref[0] = s_scr[...]


def _gdn_prompt(proj3, cw, alog, dtb, ng):
    b, ln, _ = proj3.shape
    tok = GDN_TOK
    vec = lambda w: pl.BlockSpec((1, w), lambda bb, c: (0, 0))
    return pl.pallas_call(
        functools.partial(_gdn_kernel, tok=tok, c_len=GDN_CHUNK), grid=(b, ln // tok),
        in_specs=[pl.BlockSpec((1, tok, QKV_W), lambda bb, c: (bb, c, COL_QKV)),
                  pl.BlockSpec((1, tok, GROUP), lambda bb, c: (bb, c, COL_ZC)),
                  pl.BlockSpec((1, tok, LANES), lambda bb, c: (bb, c, COL_SMALL)),
                  pl.BlockSpec((CONV_W, QKV_W), lambda bb, c: (0, 0)), vec(LANES), vec(LANES), vec(GROUP)],
        out_specs=[pl.BlockSpec((1, tok, GROUP), lambda bb, c: (bb, c, 0)),
                   pl.BlockSpec((1, HEADS, HD, HD), lambda bb, c: (bb, 0, 0, 0)),
                   pl.BlockSpec((1, CONV_W - 1, QKV_W), lambda bb, c: (bb, 0, 0))],
        out_shape=[jax.ShapeDtypeStruct((b, ln, GROUP), F32), jax.ShapeDtypeStruct((b, HEADS, HD, HD), F32),
                   jax.ShapeDtypeStruct((b, CONV_W - 1, QKV_W), F32)],
        scratch_shapes=[pltpu.VMEM((tok + 8, QKV_W), F32), pltpu.VMEM((HEADS, HD, HD), F32)],
        compiler_params=_cparams(("parallel", "arbitrary")), name="gdn_prompt",
    )(proj3, proj3, proj3, cw, alog, dtb, ng)


def _sprep_kernel(fq_ref, fk_ref, fv_ref, mq_ref, mk_ref, mv_ref, xbc_ref, qkv_ref, za_ref, zc_ref, sm_ref,
                  sbuf_ref, gbuf_ref, gfq_ref, gfk_ref, gmq_ref, gmk_ref, bf_ref, scw_ref, scb_ref, gcw_ref,
                  fqn_ref, fkn_ref, mqn_ref, mkn_ref, lf_ref, fkt_ref, fvt_ref, mkt_ref, mvt_ref, lft_ref,
                  xbct_ref, zat_ref, smt_ref, sbo_ref, qkvt_ref, zct_ref, gbo_ref):
    hm = _head_mat()

    def hnorm(x, g):
        return x * lax.rsqrt(_headsum(x * x, hm) * (1.0 / HD) + EPS) * g

    fqn_ref[...] = hnorm(fq_ref[...], gfq_ref[...])
    fkn = hnorm(fk_ref[...], gfk_ref[...])
    fkn_ref[...] = fkn
    fkt_ref[...] = _tr(fkn)
    fvt_ref[...] = _tr(fv_ref[...])
    mqn_ref[...] = hnorm(mq_ref[...], gmq_ref[...])
    mkn = hnorm(mk_ref[...], gmk_ref[...])
    mkn_ref[...] = mkn
    mkt_ref[...] = _tr(mkn)
    mvt_ref[...] = _tr(mv_ref[...])
    sm = sm_ref[...]
    lf = _log_sigmoid(sm + bf_ref[...])
    lf_ref[...] = lf[:, L_FF:L_FF + HEADS]
    lft_ref[...] = _tr(lf)[L_FF:L_FF + HEADS, :]
    smt_ref[...] = _tr(sm)
    zat_ref[...] = _tr(za_ref[...])
    zct_ref[...] = _tr(zc_ref[...])

    def conv(u, buf_ref, cw):
        return cw[3:4, :] * u + cw[2:3, :] * buf_ref[2] + cw[1:2, :] * buf_ref[1] + cw[0:1, :] * buf_ref[0]

    def roll_buf(out_ref, buf_ref, u):
        out_ref[0] = buf_ref[1]
        out_ref[1] = buf_ref[2]
        out_ref[2] = u

    u = xbc_ref[...]
    xbct_ref[...] = _tr(_silu(conv(u, sbuf_ref, scw_ref[...]) + scb_ref[...]))
    roll_buf(sbo_ref, sbuf_ref, u)
    ug = qkv_ref[...]
    act = _silu(conv(ug, gbuf_ref, gcw_ref[...]))
    roll_buf(gbo_ref, gbuf_ref, ug)
    q, k = act[:, 0:GROUP], act[:, GROUP:2 * GROUP]
    qkvt_ref[0:GROUP, :] = _tr(q * lax.rsqrt(_headsum(q * q, hm) + EPS) * (HD ** -0.5))
    qkvt_ref[GROUP:2 * GROUP, :] = _tr(k * lax.rsqrt(_headsum(k * k, hm) + EPS))
    qkvt_ref[2 * GROUP:, :] = _tr(act[:, 2 * GROUP:])


def _sprep(proj_s, sbuf, gbuf, gfq, gfk, gmq, gmk, bf_full, scw, scb, gcw):
    bd = proj_s.shape[0]
    col = lambda c, w=GROUP: pl.BlockSpec((bd, w), lambda i: (0, c))
    full = lambda a: pl.BlockSpec(a.shape, lambda i: (0,) * a.ndim)
    sds = lambda *s: jax.ShapeDtypeStruct(s, F32)
    out_shape = [sds(bd, GROUP)] * 4 + [sds(bd, HEADS)] + [sds(GROUP, bd)] * 4 + [sds(HEADS, bd),
                 sds(XBC_W, bd), sds(GROUP, bd), sds(LANES, bd), sds(CONV_W - 1, bd, XBC_W),
                 sds(QKV_W, bd), sds(GROUP, bd), sds(CONV_W - 1, bd, QKV_W)]
    return pl.pallas_call(
        _sprep_kernel, grid=(1,),
        in_specs=[col(COL_FQ), col(COL_FK), col(COL_FV), col(COL_MQ), col(COL_MK), col(COL_MV),
                  col(COL_XBC, XBC_W), col(COL_QKV, QKV_W), col(COL_ZA), col(COL_ZC), col(COL_SMALL, LANES),
                  full(sbuf), full(gbuf), full(gfq), full(gfk), full(gmq), full(gmk), full(bf_full),
                  full(scw), full(scb), full(gcw)],
        out_specs=[pl.BlockSpec(s.shape, lambda i, n=len(s.shape): (0,) * n) for s in out_shape],
        out_shape=out_shape,
        compiler_params=_cparams(("arbitrary",)), name="sample_prep",
    )(proj_s, proj_s, proj_s, proj_s, proj_s, proj_s, proj_s, proj_s, proj_s, proj_s, proj_s,
      sbuf, gbuf, gfq, gfk, gmq, gmk, bf_full, scw, scb, gcw)


def _ssd_step_kernel(x_ref, b_ref, c_ref, z_ref, dtr_ref, dtb_ref, alog_ref, d_ref, st_ref, y_ref, so_ref, y_scr):
    dt = _softplus(dtr_ref[0] + dtb_ref[0])
    dec = jnp.exp(dt * (-jnp.exp(alog_ref[0])))
    xt, bt, ct = x_ref[...], b_ref[...], c_ref[...]
    xdt = xt * dt
    for p in range(HD):
        s_new = dec * st_ref[0, 0, p] + xdt[p:p + 1, :] * bt
        so_ref[0, p] = s_new
        y_scr[p:p + 1, :] = jnp.sum(s_new * ct, axis=0, keepdims=True)
    y_ref[...] = (y_scr[...] + d_ref[0] * xt) * _silu(z_ref[...])


def _ssd_step(layer, xbct, zat, smt3, state_t, dtb, alog, dvec):
    bd = xbct.shape[1]
    tile = lambda f: pl.BlockSpec((HD, bd), f)
    par = pl.BlockSpec((1, 1, bd), lambda h: (h, 0, 0))
    return pl.pallas_call(
        _ssd_step_kernel, grid=(HEADS,),
        in_specs=[tile(lambda h: (h, 0)), tile(lambda h: (HEADS + h // 2, 0)), tile(lambda h: (HEADS + 2 + h // 2, 0)),
                  tile(lambda h: (h, 0)), pl.BlockSpec((1, 1, bd), lambda h: (L_DT + h, 0, 0)), par, par, par,
                  pl.BlockSpec((1, 1, HD, HD, bd), lambda h: (layer, h, 0, 0, 0))],
        out_specs=[tile(lambda h: (h, 0)), pl.BlockSpec((1, HD, HD, bd), lambda h: (h, 0, 0, 0))],
        out_shape=[jax.ShapeDtypeStruct((GROUP, bd), F32), jax.ShapeDtypeStruct((HEADS, HD, HD, bd), F32)],
        scratch_shapes=[pltpu.VMEM((HD, bd), F32)],
        compiler_params=_cparams(("parallel",)), name="ssd_step",
    )(xbct, xbct, xbct, zat, smt3, dtb, alog, dvec, state_t)


def _gdn_step_kernel(q_ref, k_ref, v_ref, z_ref, br_ref, ar_ref, alog_ref, dtb_ref, ng_ref, st_ref, y_ref, so_ref):
    eg = jnp.exp(-jnp.exp(alog_ref[0]) * _softplus(ar_ref[0] + dtb_ref[0]))
    beta = jax.nn.sigmoid(br_ref[0])
    qt, kt, vt = q_ref[...], k_ref[...], v_ref[...]
    ks = jnp.zeros_like(vt)
    qs = jnp.zeros_like(vt)
    for dk in range(HD):
        s_old = st_ref[0, 0, dk]
        ks = ks + kt[dk:dk + 1, :] * s_old
        qs = qs + qt[dk:dk + 1, :] * s_old
    w = beta * (vt - eg * ks)
    o = eg * qs + jnp.sum(qt * kt, axis=0, keepdims=True) * w
    for dk in range(HD):
        so_ref[0, dk] = eg * st_ref[0, 0, dk] + kt[dk:dk + 1, :] * w
    on = o * lax.rsqrt(jnp.mean(o * o, axis=0, keepdims=True) + EPS) * ng_ref[...]
    y_ref[...] = on * _silu(z_ref[...])


def _gdn_step(layer, qkvt, zct, smt3, state_t, alog, dtb, ng_b):
    bd = qkvt.shape[1]
    tile = lambda f: pl.BlockSpec((HD, bd), f)
    par = pl.BlockSpec((1, 1, bd), lambda h: (h, 0, 0))
    return pl.pallas_call(
        _gdn_step_kernel, grid=(HEADS,),
        in_specs=[tile(lambda h: (h, 0)), tile(lambda h: (HEADS + h, 0)), tile(lambda h: (2 * HEADS + h, 0)),
                  tile(lambda h: (h, 0)), pl.BlockSpec((1, 1, bd), lambda h: (L_BETA + h, 0, 0)),
                  pl.BlockSpec((1, 1, bd), lambda h: (L_GA + h, 0, 0)), par, par, tile(lambda h: (0, 0)),
                  pl.BlockSpec((1, 1, HD, HD, bd), lambda h: (layer, h, 0, 0, 0))],
        out_specs=[tile(lambda h: (h, 0)), pl.BlockSpec((1, HD, HD, bd), lambda h: (h, 0, 0, 0))],
        out_shape=[jax.ShapeDtypeStruct((GROUP, bd), F32), jax.ShapeDtypeStruct((HEADS, HD, HD, bd), F32)],
        compiler_params=_cparams(("parallel",)), name="gdn_step",
    )(qkvt, qkvt, qkvt, zct, smt3, smt3, alog, dtb, ng_b, state_t)


def _outproj_kernel(ya_ref, yb_ref, yc_ref, yd_ref, w_ref, x_ref, gt_ref, o_ref):
    acc = _dot(_bf(ya_ref[...]), w_ref[0:GROUP, :])
    acc += _dot(_bf(yb_ref[...]), w_ref[GROUP:2 * GROUP, :])
    acc += _dot(_bf(yc_ref[...]), w_ref[2 * GROUP:3 * GROUP, :])
    acc += _dot(_bf(yd_ref[...]), w_ref[3 * GROUP:4 * GROUP, :])
    o_ref[...] = x_ref[...] + gt_ref[0] * acc


def _out_proj(ya, yb, yc, yd, w_bf, x2d, mod, *, tm, per_row, rows_per_batch):
    t, d = x2d.shape
    yspec = pl.BlockSpec((tm, GROUP), lambda i: (i, 0))
    return pl.pallas_call(
        _outproj_kernel, grid=(t // tm,),
        in_specs=[yspec, yspec, yspec, yspec, pl.BlockSpec((d, d), lambda i: (0, 0)),
                  pl.BlockSpec((tm, d), lambda i: (i, 0)), _mod_spec(per_row, tm, rows_per_batch, 2)],
        out_specs=pl.BlockSpec((tm, d), lambda i: (i, 0)),
        out_shape=jax.ShapeDtypeStruct((t, d), F32),
        compiler_params=_cparams(("parallel",)), name="out_proj",
    )(ya, yb, yc, yd, w_bf, x2d, mod)


def _outproj_s_kernel(yat_ref, ng_ref, yb_ref, yct_ref, yd_ref, w_ref, x_ref, gt_ref, o_ref):
    ya = yat_ref[...]
    ya = ya * lax.rsqrt(jnp.mean(ya * ya, axis=0, keepdims=True) + EPS) * ng_ref[...]
    acc = _tn3(ya, w_ref[0:GROUP, :])
    acc += _mm3(yb_ref[...], w_ref[GROUP:2 * GROUP, :])
    acc += _tn3(yct_ref[...], w_ref[2 * GROUP:3 * GROUP, :])
    acc += _mm3(yd_ref[...], w_ref[3 * GROUP:4 * GROUP, :])
    o_ref[...] = x_ref[...] + gt_ref[0] * acc


def _out_proj_s(yat, ng_b, yb, yct, yd, w_bf, x2d, mod):
    bd, d = x2d.shape
    full = lambda a: pl.BlockSpec(a.shape, lambda i: (0,) * a.ndim)
    return pl.pallas_call(
        _outproj_s_kernel, grid=(1,),
        in_specs=[full(yat), full(ng_b), full(yb), full(yct), full(yd), full(w_bf), full(x2d),
                  _mod_spec(True, bd, 1, 2)],
        out_specs=pl.BlockSpec((bd, d), lambda i: (0, 0)),
        out_shape=jax.ShapeDtypeStruct((bd, d), F32),
        compiler_params=_cparams(("arbitrary",)), name="out_proj_sample",
    )(yat, ng_b, yb, yct, yd, w_bf, x2d, mod)


def _ffn_kernel(x_ref, g_ref, sc_ref, sh_ref, gt_ref, wg_ref, wu_ref, wd_ref, o_ref, h_scr, acc_scr, *, precise):
    f, nf = pl.program_id(1), pl.num_programs(1)
    mm = _mm3 if precise else (lambda a, b: _dot(_bf(a), b))

    @pl.when(f == 0)
    def _():
        h_scr[...] = _normmod(x_ref[...], g_ref[...], sc_ref[0], sh_ref[0]).astype(h_scr.dtype)
        acc_scr[...] = jnp.zeros_like(acc_scr)

    h = h_scr[...]
    a = _silu(mm(h, wg_ref[...])) * mm(h, wu_ref[...])
    acc_scr[...] += mm(a, wd_ref[...])

    @pl.when(f == nf - 1)
    def _():
        o_ref[...] = x_ref[...] + gt_ref[0] * acc_scr[...]


def _ffn_dense(x2d, g, mod, wg, wu, wd, *, tm, tf, per_row, rows_per_batch, precise=False):
    t, d = x2d.shape
    ff = wg.shape[1]
    ms = functools.partial(_mod_spec, per_row, tm, rows_per_batch)
    return pl.pallas_call(
        functools.partial(_ffn_kernel, precise=precise), grid=(t // tm, ff // tf),
        in_specs=[pl.BlockSpec((tm, d), lambda i, f: (i, 0)), pl.BlockSpec((1, d), lambda i, f: (0, 0)),
                  ms(4), ms(3), ms(5),
                  pl.BlockSpec((d, tf), lambda i, f: (0, f)), pl.BlockSpec((d, tf), lambda i, f: (0, f)),
                  pl.BlockSpec((tf, d), lambda i, f: (f, 0))],
        out_specs=pl.BlockSpec((tm, d), lambda i, f: (i, 0)),
        out_shape=jax.ShapeDtypeStruct((t, d), F32),
        scratch_shapes=[pltpu.VMEM((tm, d), F32 if precise else BF16), pltpu.VMEM((tm, d), F32)],
        compiler_params=_cparams(("parallel", "arbitrary")), name="ffn_dense",
    )(x2d, g, mod, mod, mod, wg, wu, wd)


def _moe_kernel(x_ref, g_ref, sc_ref, sh_ref, gt_ref, r_ref, wg_ref, wu_ref, wd_ref, o_ref,
                h_scr, acc_scr, gate_scr, gatet_scr, rank_scr, rankt_scr, count_scr, *, n_experts, cap):
    e, f = pl.program_id(1), pl.program_id(2)
    ne, nf = pl.num_programs(1), pl.num_programs(2)
    tm = x_ref.shape[0]
    lane = _iota((tm, LANES), 1)

    @pl.when(jnp.logical_and(e == 0, f == 0))
    def _():
        hf = _normmod(x_ref[...], g_ref[...], sc_ref[0], sh_ref[0])
        h_scr[...] = _bf(hf)
        acc_scr[...] = jnp.zeros_like(acc_scr)
        lane_f = lane.astype(F32)
        logits = jnp.where(lane < n_experts, _mm3(hf, r_ref[...]), NEG)
        v1 = jnp.max(logits, axis=1, keepdims=True)
        i1 = jnp.min(jnp.where(logits == v1, lane_f, float(LANES)), axis=1, keepdims=True)
        rest = jnp.where(lane_f == i1, NEG, logits)
        v2 = jnp.max(rest, axis=1, keepdims=True)
        i2 = jnp.min(jnp.where(rest == v2, lane_f, float(LANES)), axis=1, keepdims=True)
        e2 = jnp.exp(v2 - v1)
        w1 = 1.0 / (1.0 + e2)
        gates = jnp.where(lane_f == i1, w1, jnp.where(lane_f == i2, e2 * w1, 0.0))
        gate_scr[...] = gates
        gates_t = _tr(gates)
        gatet_scr[...] = gates_t
        on, on_t = _bf(jnp.where(gates != 0.0, 1.0, 0.0)), _bf(jnp.where(gates_t != 0.0, 1.0, 0.0))
        ri, ci = _iota((tm, tm), 0), _iota((tm, tm), 1)
        rank_scr[...] = _dot(_bf(ri > ci), on)
        rankt_scr[...] = _dot(on_t, _bf(ri < ci))
        totals = jnp.sum(on.astype(F32), axis=0, keepdims=True)
        for k in range(n_experts):
            count_scr[k] = totals[0, k].astype(jnp.int32)

    h = h_scr[...]
    ge = jnp.sum(jnp.where(lane == e, gate_scr[...], 0.0), axis=1, keepdims=True)
    ge_row = gatet_scr[pl.ds(e, 1), :]
    on_col = jnp.where(ge != 0.0, 1.0, 0.0)
    on_row = jnp.where(ge_row != 0.0, 1.0, 0.0)
    count = count_scr[e]

    @pl.when(count <= cap)
    def _():
        rank_col = jnp.sum(jnp.where(lane == e, rank_scr[...], 0.0), axis=1, keepdims=True)
        rank_row = rankt_scr[pl.ds(e, 1), :]
        slot_r = _iota((cap, tm), 0).astype(F32)
        pick = _bf(jnp.where(jnp.logical_and(rank_row == slot_r, on_row > 0.5), 1.0, 0.0))
        slot_c = _iota((tm, cap), 1).astype(F32)
        spread = _bf(jnp.where(jnp.logical_and(rank_col == slot_c, on_col > 0.5), 1.0, 0.0))
        hc = _bf(_dot(pick, h))
        gc = _mm01(pick, jnp.broadcast_to(ge, (tm, LANES)))[:, 0:1]
        a = _silu(_dot(hc, wg_ref[0])) * _dot(hc, wu_ref[0])
        yh, yl = _hi_lo(_dot(_bf(a), wd_ref[0]) * gc)
        acc_scr[...] += _dot(spread, yh) + _dot(spread, yl)

    @pl.when(count > cap)
    def _():
        a = _silu(_dot(h, wg_ref[0])) * _dot(h, wu_ref[0]) * ge
        acc_scr[...] += _dot(_bf(a), wd_ref[0])

    @pl.when(jnp.logical_and(e == ne - 1, f == nf - 1))
    def _():
        o_ref[...] = x_ref[...] + gt_ref[0] * acc_scr[...]


def _ffn_moe(x2d, g, mod, router_pad, wg, wu, wd, *, tm, tf, per_row, rows_per_batch):
    t, d = x2d.shape
    ne, _, ff = wg.shape
    ms = functools.partial(_mod_spec, per_row, tm, rows_per_batch)
    return pl.pallas_call(
        functools.partial(_moe_kernel, n_experts=ne, cap=-(-tm * 5 // 16 // 16) * 16), grid=(t // tm, ne, ff // tf),
        in_specs=[pl.BlockSpec((tm, d), lambda i, e, f: (i, 0)), pl.BlockSpec((1, d), lambda i, e, f: (0, 0)),
                  ms(4), ms(3), ms(5), pl.BlockSpec((d, LANES), lambda i, e, f: (0, 0)),
                  pl.BlockSpec((1, d, tf), lambda i, e, f: (e, 0, f)), pl.BlockSpec((1, d, tf), lambda i, e, f: (e, 0, f)),
                  pl.BlockSpec((1, tf, d), lambda i, e, f: (e, f, 0))],
        out_specs=pl.BlockSpec((tm, d), lambda i, e, f: (i, 0)),
        out_shape=jax.ShapeDtypeStruct((t, d), F32),
        scratch_shapes=[pltpu.VMEM((tm, d), BF16), pltpu.VMEM((tm, d), F32), pltpu.VMEM((tm, LANES), F32),
                        pltpu.VMEM((LANES, tm), F32), pltpu.VMEM((tm, LANES), F32), pltpu.VMEM((LANES, tm), F32),
                        pltpu.SMEM((ne,), jnp.int32)],
        compiler_params=_cparams(("parallel", "arbitrary", "arbitrary"), vmem_mb=56), name="ffn_moe",
    )(x2d, g, mod, mod, mod, router_pad, wg, wu, wd)


def _lane_row(vals, offset):
    return jnp.zeros((1, LANES), F32).at[0, offset:offset + vals.shape[0]].set(vals.astype(F32))


def _reorder_w_in(w):
    pad = jnp.zeros((w.shape[0], LANES - 16), w.dtype)
    return jnp.concatenate([w[:, 256:768], w[:, 0:256], w[:, 772:1540], w[:, 1544:2312], w[:, 2320:2576],
                            w[:, 2576:3344], w[:, 768:772], w[:, 1540:1544], w[:, 2312:2320], pad], axis=1)


def _tile_heads(g):
    return jnp.tile(g.astype(F32), HEADS).reshape(1, GROUP)


def _per_head_rows(vals, bd):
    return jnp.broadcast_to(vals.astype(F32).reshape(HEADS, 1, 1), (HEADS, 1, bd))


def kernel(x_prompt, x_sample, cache_fox_k, cache_fox_v, cache_fox_logf, cache_moba_k, cache_moba_v, state_ssm, state_ssm_conv, state_gdn, state_gdn_conv, page_table, c_prompt, c_sample, w_ada, b_ada, norm_mix, norm_ffn, w_in, w_out, ssd_conv_w, ssd_conv_b, ssd_dt_bias, ssd_a_log, ssd_d, ssd_norm, fox_b_f, fox_q_norm, fox_k_norm, gdn_conv_w, gdn_a_log, gdn_dt_bias, gdn_norm, moba_q_norm, moba_k_norm, ffn_w_gate, ffn_w_up, ffn_w_down, moe_router, moe_w_gate, moe_w_up, moe_w_down):
    bp, ln, d = x_prompt.shape
    bd = x_sample.shape[0]
    depth = w_in.shape[0]
    n_pages = page_table.shape[1]
    n_pool = cache_fox_k.shape[1]
    assert x_sample.shape[1] == 1 and d == D_MODEL and ln % MOBA_BLOCK == 0
    tp = bp * ln
    tm_p = 512 if ln % 512 == 0 else 256
    t_attn = 256

    mod = _ada_mod(jnp.concatenate([c_prompt, c_sample], axis=0), w_ada, b_ada)
    pt_flat = page_table.reshape(-1).astype(jnp.int32)
    page_t = lambda c: jnp.transpose(c, (0, 1, 3, 4, 2))
    fox_kc, fox_vc, moba_kc, moba_vc = page_t(cache_fox_k), page_t(cache_fox_v), page_t(cache_moba_k), page_t(cache_moba_v)
    fox_lfc = jnp.transpose(cache_fox_logf, (0, 1, 3, 2)).reshape(depth, n_pool, 1, HEADS * PAGE)
    ssm_t = jnp.transpose(state_ssm, (0, 2, 3, 4, 1))
    gdn_t = jnp.transpose(state_gdn, (0, 2, 3, 4, 1))
    xp = x_prompt.reshape(tp, d)
    xs = x_sample.reshape(bd, d)
    outs_p, outs_s = [], []
    for l in range(depth):
        mod_p = mod[l, :bp].reshape(bp, 1, 6 * d)
        mod_s = mod[l, bp:].reshape(1, bd, 6 * d)
        kw_p = dict(tm=tm_p, per_row=False, rows_per_batch=ln)
        kw_s = dict(tm=bd, per_row=True, rows_per_batch=1)
        g_mix = norm_mix[l].reshape(1, d)
        g_ffn = norm_ffn[l].reshape(1, d)
        w_in_f = _reorder_w_in(w_in[l])
        w_in_bf = _bf(w_in_f)
        w_out_bf = _bf(w_out[l])
        gfq, gfk = _tile_heads(fox_q_norm[l]), _tile_heads(fox_k_norm[l])
        gmq, gmk = _tile_heads(moba_q_norm[l]), _tile_heads(moba_k_norm[l])
        bf_full = _lane_row(fox_b_f[l], L_FF)
        scw, scb, gcw = ssd_conv_w[l], ssd_conv_b[l].reshape(1, XBC_W), gdn_conv_w[l]

        proj = _in_proj(xp, g_mix, mod_p, w_in_bf, tn=PROJ_COLS, **kw_p).reshape(bp, ln, PROJ_COLS)
        fqa, fkt, fkx, fvt, mqa, mkt, mvt, lft, kmean = _prep(proj, gfq, gfk, gmq, gmk, bf_full, tm=MOBA_BLOCK)
        y_a, ssm_h, ssm_buf = _ssd_prompt(proj, scw, scb, _lane_row(ssd_dt_bias[l], L_DT), _lane_row(ssd_a_log[l], L_DT),
                                          _lane_row(ssd_d[l], L_DT), ssd_norm[l].reshape(1, GROUP))
        y_c, gdn_s, gdn_buf = _gdn_prompt(proj, gcw, _lane_row(gdn_a_log[l], L_GA), _lane_row(gdn_dt_bias[l], L_GA),
                                          _tile_heads(gdn_norm[l]))
        y_b = _fox_prompt(fqa, fkt, fkx, fvt, t=t_attn, wide=2 if ln % (2 * t_attn) == 0 else 1)
        nblk = ln // MOBA_BLOCK
        kmean_tiles = jnp.pad(kmean.reshape(bp, nblk, HEADS, HD), ((0, 0), (0, LANES - nblk), (0, 0), (0, LANES - HD)))
        y_d = _moba_prompt(mqa, mkt, mvt, kmean_tiles.reshape(bp, LANES, HEADS * LANES))
        flat = lambda a: a.reshape(tp, GROUP)
        xp = _out_proj(flat(y_a), flat(y_b), flat(y_c), flat(y_d), w_out_bf, xp, mod_p, **kw_p)
        outs_p.append((fkt, fvt, lft, mkt, mvt, ssm_h, ssm_buf, gdn_s, gdn_buf))

        proj_s = _in_proj(xs, g_mix, mod_s, w_in_f, tn=PROJ_COLS // 9, precise=True, **kw_s)
        (sq, sk, smq, smk, slf, sfkt, sfvt, smkt, smvt, slft, xbct, zat, smt, sbo, qkvt, zct, gbo) = _sprep(
            proj_s, jnp.transpose(state_ssm_conv[l], (1, 0, 2)), jnp.transpose(state_gdn_conv[l], (1, 0, 2)),
            gfq, gfk, gmq, gmk, bf_full, scw, scb, gcw)
        smt3 = smt.reshape(LANES, 1, bd)
        yat, ssm_new = _ssd_step(l, xbct, zat, smt3, ssm_t, _per_head_rows(ssd_dt_bias[l], bd),
                                 _per_head_rows(ssd_a_log[l], bd), _per_head_rows(ssd_d[l], bd))
        yct, gdn_new = _gdn_step(l, qkvt, zct, smt3, gdn_t, _per_head_rows(gdn_a_log[l], bd),
                                 _per_head_rows(gdn_dt_bias[l], bd),
                                 jnp.broadcast_to(gdn_norm[l].astype(F32).reshape(HD, 1), (HD, bd)))
        rows = lambda a: a.reshape(bd, 1, a.shape[-1])
        ys_b = _fox_decode(pt_flat, l, rows(sq), rows(sk), rows(proj_s), rows(slf), fox_kc, fox_vc, fox_lfc[l], n_pages=n_pages)
        ys_d = _moba_decode(pt_flat, l, rows(smq), rows(smk), rows(proj_s), moba_kc, moba_vc, n_pages=n_pages)
        xs = _out_proj_s(yat, jnp.broadcast_to(ssd_norm[l].astype(F32).reshape(GROUP, 1), (GROUP, bd)),
                         ys_b.reshape(bd, GROUP), yct, ys_d.reshape(bd, GROUP), w_out[l], xs, mod_s)
        outs_s.append((sfkt, sfvt, slft, smkt, smvt, ssm_new, sbo, gdn_new, gbo))

        i = l // 2
        if l % 2 == 0:
            wg, wu, wd = _bf(ffn_w_gate[i]), _bf(ffn_w_up[i]), _bf(ffn_w_down[i])
            tf = wg.shape[1] // 2
            xp = _ffn_dense(xp, g_ffn, mod_p, wg, wu, wd, tf=tf, **kw_p)
            xs = _ffn_dense(xs, g_ffn, mod_s, ffn_w_gate[i], ffn_w_up[i], ffn_w_down[i], tf=2 * LANES, precise=True, **kw_s)
        else:
            wg, wu, wd = _bf(moe_w_gate[i]), _bf(moe_w_up[i]), _bf(moe_w_down[i])
            ne = wg.shape[0]
            router_pad = jnp.pad(moe_router[i], ((0, 0), (0, LANES - ne)))
            tf = wg.shape[2]
            xp = _ffn_moe(xp, g_ffn, mod_p, router_pad, wg, wu, wd, tf=tf, **kw_p)
            xs = _ffn_moe(xs, g_ffn, mod_s, router_pad, wg, wu, wd, tf=tf, **kw_s)

    stack = lambda outs, j: jnp.stack([o[j] for o in outs], axis=0)
    kv_p = lambda j: jnp.transpose(stack(outs_p, j), (0, 1, 4, 2, 3))
    kv_s = lambda j: jnp.transpose(stack(outs_s, j).reshape(depth, HEADS, HD, bd), (0, 3, 1, 2)).reshape(depth, bd, 1, HEADS, HD)
    st_s = lambda j: jnp.transpose(stack(outs_s, j), (0, 4, 1, 2, 3))
    buf_s = lambda j: jnp.transpose(stack(outs_s, j), (0, 2, 1, 3))
    return (xp.reshape(bp, ln, d), xs.reshape(bd, 1, d),
            kv_p(0), kv_p(1), jnp.transpose(stack(outs_p, 2), (0, 1, 3, 2)), kv_p(3), kv_p(4),
            stack(outs_p, 5), stack(outs_p, 6), stack(outs_p, 7), stack(outs_p, 8),
            kv_s(0), kv_s(1), jnp.transpose(stack(outs_s, 2), (0, 2, 1)).reshape(depth, bd, 1, HEADS), kv_s(3), kv_s(4),
            st_s(5), buf_s(6), st_s(7), buf_s(8))
```

```python
import functools

import jax
import jax.numpy as jnp
from jax import lax
from jax.experimental import pallas as pl
from jax.experimental.pallas import tpu as pltpu

F32 = jnp.float32
BF16 = jnp.bfloat16
EPS = 1e-6
NEG = -1e30

D_MODEL = 1024
GROUP = 256
HEADS = 4
HD = 64
PAGE = 128
MOBA_BLOCK = 256
MOBA_TOPK = 3
CONV_W = 4
SSD_CHUNK = 128
GDN_CHUNK = 64
GDN_TOK = 256
LANES = 128

XBC_W, QKV_W = 512, 768
COL_XBC = 0
COL_ZA = 2
COL_FQ, COL_FK, COL_FV = 3, 4, 5
COL_QKV = 2
COL_ZC = 9
COL_MQ, COL_MK, COL_MV = 10, 11, 12
COL_SMALL = 26
PROJ_COLS = 3456
L_DT, L_FF, L_BETA, L_GA = 0, 4, 8, 12


def _dot(a, b):
    return jnp.dot(a, b, preferred_element_type=F32)


def _dg(a, b, ca, cb):
    return lax.dot_general(a, b, (((ca,), (cb,)), ((), ())), preferred_element_type=F32)


def _bf(a):
    return a.astype(BF16)


def _hi_lo(a):
    hi = _bf(a)
    return hi, _bf(a - hi.astype(F32))


def _split3(a):
    a1 = _bf(a)
    r = a - a1.astype(F32)
    a2 = _bf(r)
    return a1, a2, _bf(r - a2.astype(F32))


def _mm(a, b):
    return _dot(_bf(a), _bf(b))


def _nt(a, b):
    return _dg(_bf(a), _bf(b), 1, 1)


def _tn(a, b):
    return _dg(_bf(a), _bf(b), 0, 0)


def _x3(f, a, b):
    a1, a2 = _hi_lo(a)
    b1, b2 = _hi_lo(b)
    return f(a1, b1) + (f(a1, b2) + f(a2, b1))


def _mm3(a, b):
    return _x3(_dot, a, b)


def _nt3(a, b):
    return _x3(lambda x, y: _dg(x, y, 1, 1), a, b)


def _tn3(a, b):
    return _x3(lambda x, y: _dg(x, y, 0, 0), a, b)


def _hi_lo_f32(a):
    hi = _bf(a).astype(F32)
    return hi, a - hi


def _mm3c(a, b):
    a1, a2 = _hi_lo_f32(a)
    b1, b2 = _hi_lo_f32(b)
    return _dot(_bf(jnp.concatenate([a1, a2, a1], axis=1)), _bf(jnp.concatenate([b1, b1, b2], axis=0)))


def _nt3c(a, b):
    a1, a2 = _hi_lo_f32(a)
    b1, b2 = _hi_lo_f32(b)
    return _dg(_bf(jnp.concatenate([a1, a2, a1], axis=1)), _bf(jnp.concatenate([b1, b1, b2], axis=1)), 1, 1)


def _tn3c(a, b):
    a1, a2 = _hi_lo_f32(a)
    b1, b2 = _hi_lo_f32(b)
    return _dg(_bf(jnp.concatenate([a1, a2, a1], axis=0)), _bf(jnp.concatenate([b1, b1, b2], axis=0)), 0, 0)


def _mm01(m01, x):
    x1, x2, x3 = _split3(x)
    return _dot(m01, x1) + (_dot(m01, x2) + _dot(m01, x3))


def _mmx01(x, m01):
    x1, x2, x3 = _split3(x)
    return _dot(x1, m01) + (_dot(x2, m01) + _dot(x3, m01))


def _tr(x):
    n = x.shape[1]
    eye = _bf(lax.broadcasted_iota(jnp.int32, (n, n), 0) == lax.broadcasted_iota(jnp.int32, (n, n), 1))
    x1, x2, x3 = _split3(x)
    return _dg(eye, x1, 1, 1) + (_dg(eye, x2, 1, 1) + _dg(eye, x3, 1, 1))


def _iota(shape, dim):
    return lax.broadcasted_iota(jnp.int32, shape, dim)


def _silu(x):
    return x * jax.nn.sigmoid(x)


def _softplus(x):
    return jnp.maximum(x, 0.0) + jnp.log1p(jnp.exp(-jnp.abs(x)))


def _log_sigmoid(x):
    return -_softplus(-x)


def _head_mat():
    r = lax.shift_right_logical(_iota((GROUP, GROUP), 0), 6)
    c = lax.shift_right_logical(_iota((GROUP, GROUP), 1), 6)
    return _bf(r == c)


def _headsum(x, hm):
    x1, x2 = _hi_lo(x)
    return _dot(x1, hm) + _dot(x2, hm)


def _normmod(x, g, sc, sh):
    y = x * lax.rsqrt(jnp.mean(x * x, axis=-1, keepdims=True) + EPS) * g
    return y * (1.0 + sc) + sh


def _cparams(sem, vmem_mb=48):
    return pltpu.CompilerParams(dimension_semantics=sem, vmem_limit_bytes=vmem_mb * 1024 * 1024)


def _mod_spec(per_row, tm, rows_per_batch, chunk):
    if per_row:
        return pl.BlockSpec((1, tm, D_MODEL), lambda i, *_: (0, i, chunk))
    return pl.BlockSpec((1, 1, D_MODEL), lambda i, *_: ((i * tm) // rows_per_batch, 0, chunk))


def _ada_kernel(c_ref, w_ref, b_ref, o_ref):
    o_ref[0] = _mm3(_silu(c_ref[...]), w_ref[0]) + b_ref[0]


def _ada_mod(c_all, w_ada, b_ada):
    depth, d, n = w_ada.shape
    r = c_all.shape[0]
    tn = 1536
    return pl.pallas_call(
        _ada_kernel, grid=(depth, n // tn),
        in_specs=[pl.BlockSpec((r, d), lambda l, j: (0, 0)),
                  pl.BlockSpec((1, d, tn), lambda l, j: (l, 0, j)),
                  pl.BlockSpec((1, 1, tn), lambda l, j: (l, 0, j))],
        out_specs=pl.BlockSpec((1, r, tn), lambda l, j: (l, 0, j)),
        out_shape=jax.ShapeDtypeStruct((depth, r, n), F32),
        compiler_params=_cparams(("parallel", "parallel")), name="ada_mod",
    )(c_all, w_ada, b_ada.reshape(depth, 1, n))


def _inproj_kernel(x_ref, g_ref, sc_ref, sh_ref, w_ref, o_ref, *, precise):
    h = _normmod(x_ref[...], g_ref[...], sc_ref[0], sh_ref[0])
    o_ref[...] = _mm3(h, w_ref[...]) if precise else _dot(_bf(h), w_ref[...])


def _in_proj(x2d, g, mod, w, *, tm, tn, per_row, rows_per_batch, precise=False):
    t, d = x2d.shape
    n = w.shape[1]
    return pl.pallas_call(
        functools.partial(_inproj_kernel, precise=precise), grid=(t // tm, n // tn),
        in_specs=[pl.BlockSpec((tm, d), lambda i, j: (i, 0)),
                  pl.BlockSpec((1, d), lambda i, j: (0, 0)),
                  _mod_spec(per_row, tm, rows_per_batch, 1),
                  _mod_spec(per_row, tm, rows_per_batch, 0),
                  pl.BlockSpec((d, tn), lambda i, j: (0, j))],
        out_specs=pl.BlockSpec((tm, tn), lambda i, j: (i, j)),
        out_shape=jax.ShapeDtypeStruct((t, n), F32),
        compiler_params=_cparams(("parallel", "arbitrary")), name="in_proj",
    )(x2d, g, mod, mod, w)


def _split3_f32(a):
    a1 = _bf(a).astype(F32)
    r = a - a1
    a2 = _bf(r).astype(F32)
    return a1, a2, _bf(r - a2).astype(F32)


def _query_tiles(qn, extra_fn, scale):
    return jnp.concatenate([jnp.concatenate([qn[:, h * HD:(h + 1) * HD] * scale, extra_fn(h)], axis=1)
                            for h in range(HEADS)], axis=1)


def _prep_kernel(fq_ref, fk_ref, fv_ref, mq_ref, mk_ref, mv_ref, sm_ref, gfq_ref, gfk_ref, gmq_ref, gmk_ref, bf_ref,
                 fqa_ref, fkt_ref, fkx_ref, fvt_ref, mqa_ref, mkt_ref, mvt_ref, lft_ref, kmean_ref, carry_scr):
    hm = _head_mat()
    tm = sm_ref.shape[1]
    scale = HD ** -0.5

    def hnorm(x, g):
        return x * lax.rsqrt(_headsum(x * x, hm) * (1.0 / HD) + EPS) * g

    def put_t(ref, x):
        xt = _tr(x)
        for h in range(HEADS):
            ref[0, h] = xt[h * HD:(h + 1) * HD, :]

    put_t(fkt_ref, hnorm(fk_ref[0], gfk_ref[...]))
    put_t(fvt_ref, fv_ref[0])
    mkn = hnorm(mk_ref[0], gmk_ref[...])
    put_t(mkt_ref, mkn)
    put_t(mvt_ref, mv_ref[0])
    kmean_ref[0, 0] = jnp.mean(mkn, axis=0, keepdims=True)
    lf = _log_sigmoid(sm_ref[0] + bf_ref[...])
    lft_ref[0] = _tr(lf)[L_FF:L_FF + HEADS, :]

    @pl.when(pl.program_id(1) == 0)
    def _():
        carry_scr[...] = jnp.zeros_like(carry_scr)

    tril = _bf(_iota((tm, tm), 0) >= _iota((tm, tm), 1))
    cum = _mm01(tril, lf) + carry_scr[...]
    carry_scr[...] = cum[tm - 1:tm, :]
    cum_t = _tr(cum)
    lane64 = _iota((tm, HD), 1)
    row8 = _iota((8, tm), 0)

    def fox_extra(h):
        c1, c2, c3 = _split3_f32(cum[:, L_FF + h:L_FF + h + 1])
        return jnp.where(lane64 == 0, c1, jnp.where(lane64 == 1, c2, jnp.where(lane64 == 2, c3,
                         jnp.where(lane64 < 6, 1.0, 0.0))))

    fqa_ref[0] = _query_tiles(hnorm(fq_ref[0], gfq_ref[...]), fox_extra, scale)
    for h in range(HEADS):
        k1, k2, k3 = _split3_f32(cum_t[L_FF + h:L_FF + h + 1, :])
        fkx_ref[0, h] = jnp.where(row8 < 3, 1.0, jnp.where(row8 == 3, -k1, jnp.where(row8 == 4, -k2,
                                  jnp.where(row8 == 5, -k3, 0.0))))

    q_hi = (pl.program_id(1) * tm).astype(F32)
    q_lo = _iota((tm, HD), 0).astype(F32)

    def moba_extra(h):
        slope = 2.0 ** (-2 * (h + 1))
        return jnp.where(lane64 == 0, -slope * q_hi, jnp.where(lane64 == 1, -slope * q_lo,
                         jnp.where(lane64 < 4, slope, 0.0)))

    mqa_ref[0] = _query_tiles(hnorm(mq_ref[0], gmq_ref[...]), moba_extra, scale)


def _prep(proj3, gfq, gfk, gmq, gmk, bf_full, *, tm):
    nb, ln, _ = proj3.shape

    def col(c):
        return pl.BlockSpec((1, tm, GROUP), lambda b, i: (b, i, c))

    assert tm == MOBA_BLOCK
    row = pl.BlockSpec((1, GROUP), lambda b, i: (0, 0))
    q_blk = pl.BlockSpec((1, tm, HEADS * LANES), lambda b, i: (b, i, 0))
    t_blk = pl.BlockSpec((1, HEADS, HD, tm), lambda b, i: (b, 0, 0, i))
    q_shape = jax.ShapeDtypeStruct((nb, ln, HEADS * LANES), F32)
    t_shape = jax.ShapeDtypeStruct((nb, HEADS, HD, ln), F32)
    return pl.pallas_call(
        _prep_kernel, grid=(nb, ln // tm),
        in_specs=[col(COL_FQ), col(COL_FK), col(COL_FV), col(COL_MQ), col(COL_MK), col(COL_MV),
                  pl.BlockSpec((1, tm, LANES), lambda b, i: (b, i, COL_SMALL)),
                  row, row, row, row, pl.BlockSpec((1, LANES), lambda b, i: (0, 0))],
        out_specs=[q_blk, t_blk, pl.BlockSpec((1, HEADS, 8, tm), lambda b, i: (b, 0, 0, i)), t_blk, q_blk, t_blk, t_blk,
                   pl.BlockSpec((1, HEADS, tm), lambda b, i: (b, 0, i)),
                   pl.BlockSpec((1, 1, 1, GROUP), lambda b, i: (b, i, 0, 0))],
        out_shape=[q_shape, t_shape, jax.ShapeDtypeStruct((nb, HEADS, 8, ln), F32), t_shape, q_shape, t_shape, t_shape,
                   jax.ShapeDtypeStruct((nb, HEADS, ln), F32), jax.ShapeDtypeStruct((nb, ln // tm, 1, GROUP), F32)],
        scratch_shapes=[pltpu.VMEM((1, LANES), F32)],
        compiler_params=_cparams(("parallel", "arbitrary")), name="attn_prep",
    )(proj3, proj3, proj3, proj3, proj3, proj3, proj3, gfq, gfk, gmq, gmk, bf_full)


def _attn_update(s, vt, h, m_scr, l_scr, acc_scr):
    rep = s.shape[1] // LANES
    m_old = m_scr[h]
    m_new = jnp.maximum(m_old, jnp.max(s, axis=1, keepdims=True))
    p = jnp.exp(s - jnp.concatenate([m_new] * rep, axis=1))
    alpha = jnp.exp(m_old - m_new)
    l_scr[h] = alpha * l_scr[h] + jnp.sum(p, axis=1, keepdims=True)
    acc_scr[h] = alpha[:, 0:HD] * acc_scr[h] + _nt(p, vt)
    m_scr[h] = m_new


def _attn_init(m_scr, l_scr, acc_scr):
    m_scr[...] = jnp.full_like(m_scr, NEG)
    l_scr[...] = jnp.zeros_like(l_scr)
    acc_scr[...] = jnp.zeros_like(acc_scr)


def _attn_finish(o_ref, l_scr, acc_scr):
    for h in range(HEADS):
        o_ref[0, :, h * HD:(h + 1) * HD] = acc_scr[h] / l_scr[h][:, 0:HD]


def _attn_scratch(t):
    return [pltpu.VMEM((HEADS, t, LANES), F32), pltpu.VMEM((HEADS, t, LANES), F32), pltpu.VMEM((HEADS, t, HD), F32)]


def _causal_pairs(n):
    pairs = [(qb, kb) for qb in range(n) for kb in range(qb + 1)]
    return jnp.asarray([p[0] for p in pairs], jnp.int32), jnp.asarray([p[1] for p in pairs], jnp.int32)


def _key_tile(kt, extra):
    return jnp.concatenate([kt, extra, jnp.zeros((LANES - HD - 8, kt.shape[1]), F32)], axis=0)


def _fox_kernel(qtab_ref, ktab_ref, q_ref, kt_ref, kx_ref, vt_ref, o_ref, m_scr, l_scr, acc_scr, *, t, wide):
    step_id = pl.program_id(1)
    qi, ki = qtab_ref[step_id], ktab_ref[step_id]
    last = qi // wide

    @pl.when(ki == 0)
    def _():
        _attn_init(m_scr, l_scr, acc_scr)

    def step(diagonal):
        q = q_ref[0]
        for h in range(HEADS):
            s = _dot(_bf(q[:, h * LANES:(h + 1) * LANES]), _bf(_key_tile(kt_ref[0, h], kx_ref[0, h])))
            if diagonal:
                shift = (qi - last * wide) * t
                s = jnp.where(_iota((t, wide * t), 1) <= _iota((t, wide * t), 0) + shift, s, NEG)
            _attn_update(s, vt_ref[0, h], h, m_scr, l_scr, acc_scr)

    @pl.when(ki < last)
    def _():
        step(False)

    @pl.when(ki == last)
    def _():
        step(True)
        _attn_finish(o_ref, l_scr, acc_scr)


def _fox_prompt(qa, kt, kx, vt, *, t, wide):
    b, ln, _ = qa.shape
    assert ln % (wide * t) == 0
    pairs = [(qb, kb) for qb in range(ln // t) for kb in range(qb // wide + 1)]
    qtab = jnp.asarray([p[0] for p in pairs], jnp.int32)
    ktab = jnp.asarray([p[1] for p in pairs], jnp.int32)
    kv = pl.BlockSpec((1, HEADS, HD, wide * t), lambda bb, s, qt, kt_: (bb, 0, 0, kt_[s]))
    return pl.pallas_call(
        functools.partial(_fox_kernel, t=t, wide=wide),
        grid_spec=pltpu.PrefetchScalarGridSpec(
            num_scalar_prefetch=2, grid=(b, len(pairs)),
            in_specs=[pl.BlockSpec((1, t, HEADS * LANES), lambda bb, s, qt, kt_: (bb, qt[s], 0)), kv,
                      pl.BlockSpec((1, HEADS, 8, wide * t), lambda bb, s, qt, kt_: (bb, 0, 0, kt_[s])), kv],
            out_specs=pl.BlockSpec((1, t, GROUP), lambda bb, s, qt, kt_: (bb, qt[s], 0)),
            scratch_shapes=_attn_scratch(t)),
        out_shape=jax.ShapeDtypeStruct((b, ln, GROUP), F32),
        compiler_params=_cparams(("parallel", "arbitrary")), name="fox_prompt",
    )(qtab, ktab, qa, kt, kx, vt)


def _moba_kernel(qtab_ref, jtab_ref, q_ref, kt_ref, vt_ref, km_ref, o_ref, m_scr, l_scr, acc_scr, sel_scr):
    t = MOBA_BLOCK
    step_id = pl.program_id(1)
    qi, j = qtab_ref[step_id], jtab_ref[step_id]
    lane = _iota((t, LANES), 1)
    row8 = _iota((8, t), 0)

    def key_extra(blk):
        k_hi = (blk * t).astype(F32)
        k_lo = _iota((8, t), 1).astype(F32)
        return jnp.where(row8 < 2, 1.0, jnp.where(row8 == 2, k_hi, jnp.where(row8 == 3, k_lo, 0.0)))

    @pl.when(j == 0)
    def _():
        _attn_init(m_scr, l_scr, acc_scr)
        q, km = q_ref[0], km_ref[0]
        lane_f = lane.astype(F32)
        causal = _iota((t, t), 1) <= _iota((t, t), 0)
        kx = key_extra(qi)
        for h in range(HEADS):
            tl = slice(h * LANES, (h + 1) * LANES)
            gate = jnp.where(lane < qi, _nt3(q[:, tl], km[:, tl]), NEG)
            sel = jnp.zeros((t, LANES), F32)
            for k in range(MOBA_TOPK):
                mx = jnp.max(gate, axis=1, keepdims=True)
                idx = jnp.min(jnp.where(gate == mx, lane_f, float(LANES)), axis=1, keepdims=True)
                hit = lane_f == idx
                sel = jnp.where(hit, jnp.maximum(sel, jnp.where(qi > k, 1.0, 0.0)), sel)
                gate = jnp.where(hit, NEG, gate)
            sel_scr[h] = sel
            s = _dot(_bf(q[:, tl]), _bf(_key_tile(kt_ref[0, h], kx)))
            _attn_update(jnp.where(causal, s, NEG), vt_ref[0, h], h, m_scr, l_scr, acc_scr)

    @pl.when(jnp.logical_and(j >= 1, j <= qi))
    def _():
        q = q_ref[0]
        n = j - 1
        kx = key_extra(n)
        for h in range(HEADS):
            picked = jnp.sum(jnp.where(lane == n, sel_scr[h], 0.0), axis=1, keepdims=True)
            s = _dot(_bf(q[:, h * LANES:(h + 1) * LANES]), _bf(_key_tile(kt_ref[0, h], kx)))
            _attn_update(jnp.where(picked > 0.5, s, NEG), vt_ref[0, h], h, m_scr, l_scr, acc_scr)

    @pl.when(j == qi)
    def _():
        _attn_finish(o_ref, l_scr, acc_scr)


def _moba_prompt(qa, kt, vt, kmean_tiles):
    b, ln, _ = qa.shape
    t = MOBA_BLOCK
    n = ln // t
    qtab, jtab = _causal_pairs(n)

    def kv_idx(bb, s, qt, jt):
        return (bb, 0, 0, jnp.where(jt[s] == 0, qt[s], jt[s] - 1))

    kv = pl.BlockSpec((1, HEADS, HD, t), kv_idx)
    return pl.pallas_call(
        _moba_kernel,
        grid_spec=pltpu.PrefetchScalarGridSpec(
            num_scalar_prefetch=2, grid=(b, qtab.shape[0]),
            in_specs=[pl.BlockSpec((1, t, HEADS * LANES), lambda bb, s, qt, jt: (bb, qt[s], 0)), kv, kv,
                      pl.BlockSpec((1, LANES, HEADS * LANES), lambda bb, s, qt, jt: (bb, 0, 0))],
            out_specs=pl.BlockSpec((1, t, GROUP), lambda bb, s, qt, jt: (bb, qt[s], 0)),
            scratch_shapes=_attn_scratch(t) + [pltpu.VMEM((HEADS, t, LANES), F32)]),
        out_shape=jax.ShapeDtypeStruct((b, ln, GROUP), F32),
        compiler_params=_cparams(("parallel", "arbitrary")), name="moba_prompt",
    )(qtab, jtab, qa, kt, vt, kmean_tiles)


def _query_cols(q):
    return _tr(jnp.broadcast_to(q, (LANES, GROUP)))


def _head_dots(a, b):
    ind = _bf(lax.shift_right_logical(_iota((GROUP, LANES), 0), 6) == _iota((GROUP, LANES), 1))
    return _mmx01(jnp.broadcast_to(a * b, (8, GROUP)), ind)[0:1, :]


def _lane_sums(acc):
    a1, a2, a3 = _split3(acc)
    ones = jnp.ones((8, LANES), BF16)
    return (_dg(ones, a1, 1, 1) + (_dg(ones, a2, 1, 1) + _dg(ones, a3, 1, 1)))[0:1, :]


def _decode_head(srows, s_self, v_refs, h):
    mrow = srows[0]
    for r in srows[1:]:
        mrow = jnp.maximum(mrow, r)
    m = jnp.maximum(jnp.max(mrow, axis=1, keepdims=True), s_self)
    acc = jnp.zeros((HD, LANES), F32)
    lrow = jnp.zeros((1, LANES), F32)
    for j, r in enumerate(srows):
        p = jnp.exp(r - m)
        lrow = lrow + p
        acc = acc + p * v_refs[j][0, 0, h]
    w_self = jnp.exp(s_self - m)
    return acc, w_self, jnp.sum(lrow, axis=1, keepdims=True) + w_self


def _fox_dec_kernel(pt_ref, q_ref, kn_ref, vn_ref, lfn_ref, lfc_ref, *rest, n_pages, scale):
    n = n_pages
    k_refs, v_refs = rest[:n], rest[n:2 * n]
    o_ref, qb_scr, lf_scr, acc_scr = rest[2 * n:]
    q = q_ref[0]
    qb_scr[...] = _query_cols(q)
    s_new = _head_dots(q, kn_ref[0]) * scale
    lfn = lfn_ref[0]
    first = pl.program_id(0) * n
    for j in range(n):
        row = lfc_ref[pt_ref[first + j]]
        for h in range(HEADS):
            lf_scr[h * n + j:h * n + j + 1, :] = row[:, h * PAGE:(h + 1) * PAGE]
    lf_all = lf_scr[...]
    later = _bf(_iota((PAGE, PAGE), 0) > _iota((PAGE, PAGE), 1))
    suffix = _mmx01(lf_all, later)
    page_sum = jnp.sum(lf_all, axis=1, keepdims=True)
    grp = lax.shift_right_logical(_iota((1, GROUP), 1), 6)
    w_row = jnp.zeros((1, GROUP), F32)
    l_row = jnp.ones((1, GROUP), F32)
    for h in range(HEADS):
        carry = lfn[:, h:h + 1]
        offs = [None] * n
        for j in reversed(range(n)):
            offs[j] = carry
            carry = carry + page_sum[h * n + j:h * n + j + 1, :]
        qh = qb_scr[h * HD:(h + 1) * HD, :]
        srows = [jnp.sum(qh * k_refs[j][0, 0, h], axis=0, keepdims=True) * scale
                 + (suffix[h * n + j:h * n + j + 1, :] + offs[j]) for j in range(n)]
        acc, w_self, l = _decode_head(srows, s_new[:, h:h + 1], v_refs, h)
        acc_scr[h * HD:(h + 1) * HD, :] = acc
        w_row = jnp.where(grp == h, w_self, w_row)
        l_row = jnp.where(grp == h, l, l_row)
    o_ref[0] = (_lane_sums(acc_scr[...]) + w_row * vn_ref[0]) / l_row


def _fox_decode(pt_flat, layer, q, kn, proj_s3, lfn, kc, vc, lfc, *, n_pages):
    bd = q.shape[0]
    row = pl.BlockSpec((1, 1, GROUP), lambda b, pt: (b, 0, 0))

    def page(j):
        return lambda b, pt: (layer, pt[b * n_pages + j], 0, 0, 0)

    kv_specs = [pl.BlockSpec((1, 1, HEADS, HD, PAGE), page(j)) for j in range(n_pages)]
    return pl.pallas_call(
        functools.partial(_fox_dec_kernel, n_pages=n_pages, scale=HD ** -0.5),
        grid_spec=pltpu.PrefetchScalarGridSpec(
            num_scalar_prefetch=1, grid=(bd,),
            in_specs=[row, row, pl.BlockSpec((1, 1, GROUP), lambda b, pt: (b, 0, COL_FV)),
                      pl.BlockSpec((1, 1, HEADS), lambda b, pt: (b, 0, 0)),
                      pl.BlockSpec(lfc.shape, lambda b, pt: (0, 0, 0))] + kv_specs + kv_specs,
            out_specs=row,
            scratch_shapes=[pltpu.VMEM((GROUP, LANES), F32), pltpu.VMEM((HEADS * n_pages, PAGE), F32),
                            pltpu.VMEM((GROUP, LANES), F32)]),
        out_shape=jax.ShapeDtypeStruct((bd, 1, GROUP), F32),
        compiler_params=_cparams(("parallel",)), name="fox_decode",
    )(pt_flat, q, kn, proj_s3, lfn, lfc, *([kc] * n_pages), *([vc] * n_pages))


def _moba_dec_kernel(pt_ref, q_ref, kn_ref, vn_ref, *rest, n_pages, scale):
    del pt_ref
    n = n_pages
    ppb = MOBA_BLOCK // PAGE
    n_past = n // ppb
    past_len = n * PAGE
    k_refs, v_refs = rest[:n], rest[n:2 * n]
    o_ref, qb_scr, acc_scr = rest[2 * n:]
    q = q_ref[0]
    qb_scr[...] = _query_cols(q)
    s_own = _head_dots(q, kn_ref[0]) * scale
    rows8, lanes8 = _iota((8, LANES), 0), _iota((8, LANES), 1)
    raws = []
    gates = jnp.full((8, LANES), NEG, F32)
    for h in range(HEADS):
        qh = qb_scr[h * HD:(h + 1) * HD, :]
        raw_h = [jnp.sum(qh * k_refs[j][0, 0, h], axis=0, keepdims=True) for j in range(n)]
        raws.append(raw_h)
        for blk in range(n_past):
            tot = raw_h[ppb * blk]
            for j in range(ppb * blk + 1, ppb * (blk + 1)):
                tot = tot + raw_h[j]
            gate = jnp.sum(tot, axis=1, keepdims=True) * (1.0 / MOBA_BLOCK)
            gates = jnp.where(jnp.logical_and(rows8 == h, lanes8 == blk), gate, gates)
    lanes_f = lanes8.astype(F32)
    sel = jnp.zeros((8, LANES), F32)
    for _ in range(min(MOBA_TOPK, n_past)):
        mx = jnp.max(gates, axis=1, keepdims=True)
        idx = jnp.min(jnp.where(gates == mx, lanes_f, float(LANES)), axis=1, keepdims=True)
        hit = lanes_f == idx
        sel = jnp.where(hit, 1.0, sel)
        gates = jnp.where(hit, NEG, gates)
    lane_f = _iota((1, LANES), 1).astype(F32)
    grp = lax.shift_right_logical(_iota((1, GROUP), 1), 6)
    w_row = jnp.zeros((1, GROUP), F32)
    l_row = jnp.ones((1, GROUP), F32)
    for h in range(HEADS):
        slope = 2.0 ** (-2 * (h + 1))
        srows = []
        for j in range(n):
            keep = sel[h:h + 1, j // ppb:j // ppb + 1]
            s = raws[h][j] * scale - slope * (float(past_len - j * PAGE) - lane_f)
            srows.append(s * keep + (1.0 - keep) * NEG)
        acc, w_self, l = _decode_head(srows, s_own[:, h:h + 1], v_refs, h)
        acc_scr[h * HD:(h + 1) * HD, :] = acc
        w_row = jnp.where(grp == h, w_self, w_row)
        l_row = jnp.where(grp == h, l, l_row)
    o_ref[0] = (_lane_sums(acc_scr[...]) + w_row * vn_ref[0]) / l_row


def _moba_decode(pt_flat, layer, q, kn, proj_s3, kc, vc, *, n_pages):
    bd = q.shape[0]
    assert (n_pages * PAGE) % MOBA_BLOCK == 0 and n_pages * PAGE // MOBA_BLOCK <= LANES
    row = pl.BlockSpec((1, 1, GROUP), lambda b, pt: (b, 0, 0))

    def page(j):
        return lambda b, pt: (layer, pt[b * n_pages + j], 0, 0, 0)

    kv_specs = [pl.BlockSpec((1, 1, HEADS, HD, PAGE), page(j)) for j in range(n_pages)]
    return pl.pallas_call(
        functools.partial(_moba_dec_kernel, n_pages=n_pages, scale=HD ** -0.5),
        grid_spec=pltpu.PrefetchScalarGridSpec(
            num_scalar_prefetch=1, grid=(bd,),
            in_specs=[row, row, pl.BlockSpec((1, 1, GROUP), lambda b, pt: (b, 0, COL_MV))] + kv_specs + kv_specs,
            out_specs=row,
            scratch_shapes=[pltpu.VMEM((GROUP, LANES), F32), pltpu.VMEM((GROUP, LANES), F32)]),
        out_shape=jax.ShapeDtypeStruct((bd, 1, GROUP), F32),
        compiler_params=_cparams(("parallel",)), name="moba_decode",
    )(pt_flat, q, kn, proj_s3, *([kc] * n_pages), *([vc] * n_pages))


def _chunk_conv(u, cw, ext_scr, c_len):
    ext_scr[8:8 + c_len, :] = u
    out = (cw[3:4, :] * u + cw[2:3, :] * ext_scr[7:7 + c_len, :]
           + cw[1:2, :] * ext_scr[6:6 + c_len, :] + cw[0:1, :] * ext_scr[5:5 + c_len, :])
    ext_scr[0:8, :] = ext_scr[c_len:c_len + 8, :]
    return out


def _ssd_kernel(xbc_ref, z_ref, sm_ref, cw_ref, cb_ref, dtb_ref, alog_ref, d_ref, ng_ref,
                y_ref, hs_ref, cs_ref, ext_scr, h_scr, y_scr, *, c_len):
    c, nc = pl.program_id(1), pl.num_programs(1)

    @pl.when(c == 0)
    def _():
        ext_scr[0:8, :] = jnp.zeros((8, XBC_W), F32)
        h_scr[...] = jnp.zeros_like(h_scr)

    u = xbc_ref[0]
    act = _silu(_chunk_conv(u, cw_ref[...], ext_scr, c_len) + cb_ref[...])

    @pl.when(c == nc - 1)
    def _():
        cs_ref[0] = u[c_len - (CONV_W - 1):c_len, :]

    xa, bm, cm = act[:, 0:GROUP], act[:, GROUP:GROUP + 2 * HD], act[:, GROUP + 2 * HD:]
    dt = _softplus(sm_ref[0] + dtb_ref[...])
    a_neg = -jnp.exp(alog_ref[...])
    row, colm = _iota((c_len, c_len), 0), _iota((c_len, c_len), 1)
    causal = row >= colm
    gam = _mm01(_bf(causal), dt * a_neg)
    gam_t, dt_t = _tr(gam), _tr(dt)
    glast = gam[c_len - 1:c_len, :]
    eg = jnp.exp(gam)
    wst = jnp.exp(glast - gam) * dt
    elast = jnp.exp(glast)
    dvec = d_ref[...]
    for g in range(2):
        cg, bg = cm[:, g * HD:(g + 1) * HD], bm[:, g * HD:(g + 1) * HD]
        cb = _nt(cg, bg)
        for h in (2 * g, 2 * g + 1):
            sl = slice(h * HD, (h + 1) * HD)
            hl = slice(L_DT + h, L_DT + h + 1)
            dec = jnp.exp(jnp.where(causal, gam[:, hl] - gam_t[hl, :], NEG))
            xh = xa[:, sl]
            hprev = h_scr[h]
            y_scr[:, sl] = (_mm(cb * dec * dt_t[hl, :], xh) + _nt(cg, hprev) * eg[:, hl] + dvec[:, hl] * xh)
            h_scr[h] = elast[:, hl] * hprev + _tn3c(xh * wst[:, hl], bg)
    yg = y_scr[...] * _silu(z_ref[0])
    y_ref[0] = yg * lax.rsqrt(jnp.mean(yg * yg, axis=-1, keepdims=True) + EPS) * ng_ref[...]

    @pl.when(c == nc - 1)
    def _():
        hs_ref[0] = h_scr[...]


def _ssd_prompt(proj3, cw, cb, dtb, alog, dvec, ng):
    b, ln, _ = proj3.shape
    c_len = SSD_CHUNK
    vec = lambda w: pl.BlockSpec((1, w), lambda bb, c: (0, 0))
    return pl.pallas_call(
        functools.partial(_ssd_kernel, c_len=c_len), grid=(b, ln // c_len),
        in_specs=[pl.BlockSpec((1, c_len, XBC_W), lambda bb, c: (bb, c, COL_XBC)),
                  pl.BlockSpec((1, c_len, GROUP), lambda bb, c: (bb, c, COL_ZA)),
                  pl.BlockSpec((1, c_len, LANES), lambda bb, c: (bb, c, COL_SMALL)),
                  pl.BlockSpec((CONV_W, XBC_W), lambda bb, c: (0, 0)), vec(XBC_W), vec(LANES), vec(LANES), vec(LANES),
                  vec(GROUP)],
        out_specs=[pl.BlockSpec((1, c_len, GROUP), lambda bb, c: (bb, c, 0)),
                   pl.BlockSpec((1, HEADS, HD, HD), lambda bb, c: (bb, 0, 0, 0)),
                   pl.BlockSpec((1, CONV_W - 1, XBC_W), lambda bb, c: (bb, 0, 0))],
        out_shape=[jax.ShapeDtypeStruct((b, ln, GROUP), F32), jax.ShapeDtypeStruct((b, HEADS, HD, HD), F32),
                   jax.ShapeDtypeStruct((b, CONV_W - 1, XBC_W), F32)],
        scratch_shapes=[pltpu.VMEM((c_len + 8, XBC_W), F32), pltpu.VMEM((HEADS, HD, HD), F32),
                        pltpu.VMEM((c_len, GROUP), F32)],
        compiler_params=_cparams(("parallel", "arbitrary")), name="ssd_prompt",
    )(proj3, proj3, proj3, cw, cb, dtb, alog, dvec, ng)


def _gdn_kernel(qkv_ref, z_ref, sm_ref, cw_ref, alog_ref, dtb_ref, ng_ref,
                y_ref, ss_ref, cs_ref, ext_scr, s_scr, *, tok, c_len):
    c, nc = pl.program_id(1), pl.num_programs(1)

    @pl.when(c == 0)
    def _():
        ext_scr[0:8, :] = jnp.zeros((8, QKV_W), F32)
        s_scr[...] = jnp.zeros_like(s_scr)

    u = qkv_ref[0]
    act = _silu(_chunk_conv(u, cw_ref[...], ext_scr, tok))

    @pl.when(c == nc - 1)
    def _():
        cs_ref[0] = u[tok - (CONV_W - 1):tok, :]

    hm = _head_mat()
    q, k, v = act[:, 0:GROUP], act[:, GROUP:2 * GROUP], act[:, 2 * GROUP:]
    qn = q * lax.rsqrt(_headsum(q * q, hm) + EPS) * (HD ** -0.5)
    kn = k * lax.rsqrt(_headsum(k * k, hm) + EPS)
    sm = sm_ref[0]
    beta = jax.nn.sigmoid(sm)
    gl = -jnp.exp(alog_ref[...]) * _softplus(sm + dtb_ref[...])
    shift = c_len.bit_length() - 1
    ri, ci = _iota((tok, tok), 0), _iota((tok, tok), 1)
    same_chunk = lax.shift_right_logical(ri, shift) == lax.shift_right_logical(ci, shift)
    gam = _mm01(_bf(jnp.logical_and(ri >= ci, same_chunk)), gl)
    gam_t = _tr(gam)
    eg = jnp.exp(gam)
    row, colm = _iota((c_len, c_len), 0), _iota((c_len, c_len), 1)
    incl = row >= colm
    strict = row > colm
    eye = (row == colm).astype(F32)
    z, ng = z_ref[0], ng_ref[...]
    units = [(i, h) for i in range(tok // c_len) for h in range(HEADS)]
    n_fac = c_len.bit_length() - 2

    dec, kh, qh, bcol, pinv, mpow = {}, {}, {}, {}, {}, {}
    for un in units:
        i, h = un
        r = slice(i * c_len, (i + 1) * c_len)
        g = L_GA + h
        dec[un] = jnp.exp(jnp.where(incl, gam[r, g:g + 1] - gam_t[g:g + 1, r], NEG))
        kh[un], qh[un] = kn[r, h * HD:(h + 1) * HD], qn[r, h * HD:(h + 1) * HD]
        bcol[un] = beta[r, L_BETA + h:L_BETA + h + 1]
        nmat = bcol[un] * _nt3c(kh[un], kh[un]) * jnp.where(strict, dec[un], 0.0)
        pinv[un] = eye - nmat
        mpow[un] = _mm3c(nmat, nmat)
    for lvl in range(n_fac):
        for un in units:
            if lvl + 1 < n_fac:
                both = _mm3c(jnp.concatenate([pinv[un], mpow[un]], axis=0), mpow[un])
                pinv[un], mpow[un] = pinv[un] + both[0:c_len], both[c_len:]
            else:
                pinv[un] = pinv[un] + _mm3c(pinv[un], mpow[un])
    uu, wk, qkd = {}, {}, {}
    for un in units:
        i, h = un
        r = slice(i * c_len, (i + 1) * c_len)
        g = L_GA + h
        rhs = jnp.concatenate([bcol[un] * v[r, h * HD:(h + 1) * HD], (bcol[un] * eg[r, g:g + 1]) * kh[un]], axis=1)
        sol = _mm3c(pinv[un], rhs)
        uu[un], wk[un] = sol[:, 0:HD], sol[:, HD:]
        qkd[un] = _nt(qh[un], kh[un]) * dec[un]

    for h in range(HEADS):
        sl = slice(h * HD, (h + 1) * HD)
        g = L_GA + h
        s_run = s_scr[h]
        for i in range(tok // c_len):
            un = (i, h)
            r = slice(i * c_len, (i + 1) * c_len)
            glast = gam[(i + 1) * c_len - 1:(i + 1) * c_len, g:g + 1]
            on_state = _mm3c(jnp.concatenate([wk[un], qh[un] * eg[r, g:g + 1]], axis=0), s_run)
            w = uu[un] - on_state[0:c_len]
            o = on_state[c_len:] + _mm(qkd[un], w)
            s_run = jnp.exp(glast) * s_run + _tn3c(kh[un] * jnp.exp(glast - gam[r, g:g + 1]), w)
            on = o * lax.rsqrt(jnp.mean(o * o, axis=-1, keepdims=True) + EPS) * ng[:, sl]
            y_ref[0, r, sl] = on * _silu(z[r, sl])
        s_scr[h] = s_run

    @pl.when(c == nc - 1)
    def _():
        ss_ref[0] = s_scr[...]


def _gdn_prompt(proj3, cw, alog, dtb, ng):
    b, ln, _ = proj3.shape
    tok = GDN_TOK
    vec = lambda w: pl.BlockSpec((1, w), lambda bb, c: (0, 0))
    return pl.pallas_call(
        functools.partial(_gdn_kernel, tok=tok, c_len=GDN_CHUNK), grid=(b, ln // tok),
        in_specs=[pl.BlockSpec((1, tok, QKV_W), lambda bb, c: (bb, c, COL_QKV)),
                  pl.BlockSpec((1, tok, GROUP), lambda bb, c: (bb, c, COL_ZC)),
                  pl.BlockSpec((1, tok, LANES), lambda bb, c: (bb, c, COL_SMALL)),
                  pl.BlockSpec((CONV_W, QKV_W), lambda bb, c: (0, 0)), vec(LANES), vec(LANES), vec(GROUP)],
        out_specs=[pl.BlockSpec((1, tok, GROUP), lambda bb, c: (bb, c, 0)),
                   pl.BlockSpec((1, HEADS, HD, HD), lambda bb, c: (bb, 0, 0, 0)),
                   pl.BlockSpec((1, CONV_W - 1, QKV_W), lambda bb, c: (bb, 0, 0))],
        out_shape=[jax.ShapeDtypeStruct((b, ln, GROUP), F32), jax.ShapeDtypeStruct((b, HEADS, HD, HD), F32),
                   jax.ShapeDtypeStruct((b, CONV_W - 1, QKV_W), F32)],
        scratch_shapes=[pltpu.VMEM((tok + 8, QKV_W), F32), pltpu.VMEM((HEADS, HD, HD), F32)],
        compiler_params=_cparams(("parallel", "arbitrary")), name="gdn_prompt",
    )(proj3, proj3, proj3, cw, alog, dtb, ng)


def _sprep_kernel(fq_ref, fk_ref, fv_ref, mq_ref, mk_ref, mv_ref, xbc_ref, qkv_ref, za_ref, zc_ref, sm_ref,
                  sbuf_ref, gbuf_ref, gfq_ref, gfk_ref, gmq_ref, gmk_ref, bf_ref, scw_ref, scb_ref, gcw_ref,
                  fqn_ref, fkn_ref, mqn_ref, mkn_ref, lf_ref, fkt_ref, fvt_ref, mkt_ref, mvt_ref, lft_ref,
                  xbct_ref, zat_ref, smt_ref, sbo_ref, qkvt_ref, zct_ref, gbo_ref):
    hm = _head_mat()

    def hnorm(x, g):
        return x * lax.rsqrt(_headsum(x * x, hm) * (1.0 / HD) + EPS) * g

    fqn_ref[...] = hnorm(fq_ref[...], gfq_ref[...])
    fkn = hnorm(fk_ref[...], gfk_ref[...])
    fkn_ref[...] = fkn
    fkt_ref[...] = _tr(fkn)
    fvt_ref[...] = _tr(fv_ref[...])
    mqn_ref[...] = hnorm(mq_ref[...], gmq_ref[...])
    mkn = hnorm(mk_ref[...], gmk_ref[...])
    mkn_ref[...] = mkn
    mkt_ref[...] = _tr(mkn)
    mvt_ref[...] = _tr(mv_ref[...])
    sm = sm_ref[...]
    lf = _log_sigmoid(sm + bf_ref[...])
    lf_ref[...] = lf[:, L_FF:L_FF + HEADS]
    lft_ref[...] = _tr(lf)[L_FF:L_FF + HEADS, :]
    smt_ref[...] = _tr(sm)
    zat_ref[...] = _tr(za_ref[...])
    zct_ref[...] = _tr(zc_ref[...])

    def conv(u, buf_ref, cw):
        return cw[3:4, :] * u + cw[2:3, :] * buf_ref[2] + cw[1:2, :] * buf_ref[1] + cw[0:1, :] * buf_ref[0]

    def roll_buf(out_ref, buf_ref, u):
        out_ref[0] = buf_ref[1]
        out_ref[1] = buf_ref[2]
        out_ref[2] = u

    u = xbc_ref[...]
    xbct_ref[...] = _tr(_silu(conv(u, sbuf_ref, scw_ref[...]) + scb_ref[...]))
    roll_buf(sbo_ref, sbuf_ref, u)
    ug = qkv_ref[...]
    act = _silu(conv(ug, gbuf_ref, gcw_ref[...]))
    roll_buf(gbo_ref, gbuf_ref, ug)
    q, k = act[:, 0:GROUP], act[:, GROUP:2 * GROUP]
    qkvt_ref[0:GROUP, :] = _tr(q * lax.rsqrt(_headsum(q * q, hm) + EPS) * (HD ** -0.5))
    qkvt_ref[GROUP:2 * GROUP, :] = _tr(k * lax.rsqrt(_headsum(k * k, hm) + EPS))
    qkvt_ref[2 * GROUP:, :] = _tr(act[:, 2 * GROUP:])


def _sprep(proj_s, sbuf, gbuf, gfq, gfk, gmq, gmk, bf_full, scw, scb, gcw):
    bd = proj_s.shape[0]
    col = lambda c, w=GROUP: pl.BlockSpec((bd, w), lambda i: (0, c))
    full = lambda a: pl.BlockSpec(a.shape, lambda i: (0,) * a.ndim)
    sds = lambda *s: jax.ShapeDtypeStruct(s, F32)
    out_shape = [sds(bd, GROUP)] * 4 + [sds(bd, HEADS)] + [sds(GROUP, bd)] * 4 + [sds(HEADS, bd),
                 sds(XBC_W, bd), sds(GROUP, bd), sds(LANES, bd), sds(CONV_W - 1, bd, XBC_W),
                 sds(QKV_W, bd), sds(GROUP, bd), sds(CONV_W - 1, bd, QKV_W)]
    return pl.pallas_call(
        _sprep_kernel, grid=(1,),
        in_specs=[col(COL_FQ), col(COL_FK), col(COL_FV), col(COL_MQ), col(COL_MK), col(COL_MV),
                  col(COL_XBC, XBC_W), col(COL_QKV, QKV_W), col(COL_ZA), col(COL_ZC), col(COL_SMALL, LANES),
                  full(sbuf), full(gbuf), full(gfq), full(gfk), full(gmq), full(gmk), full(bf_full),
                  full(scw), full(scb), full(gcw)],
        out_specs=[pl.BlockSpec(s.shape, lambda i, n=len(s.shape): (0,) * n) for s in out_shape],
        out_shape=out_shape,
        compiler_params=_cparams(("arbitrary",)), name="sample_prep",
    )(proj_s, proj_s, proj_s, proj_s, proj_s, proj_s, proj_s, proj_s, proj_s, proj_s, proj_s,
      sbuf, gbuf, gfq, gfk, gmq, gmk, bf_full, scw, scb, gcw)


def _ssd_step_kernel(x_ref, b_ref, c_ref, z_ref, dtr_ref, dtb_ref, alog_ref, d_ref, st_ref, y_ref, so_ref, y_scr):
    dt = _softplus(dtr_ref[0] + dtb_ref[0])
    dec = jnp.exp(dt * (-jnp.exp(alog_ref[0])))
    xt, bt, ct = x_ref[...], b_ref[...], c_ref[...]
    xdt = xt * dt
    for p in range(HD):
        s_new = dec * st_ref[0, 0, p] + xdt[p:p + 1, :] * bt
        so_ref[0, p] = s_new
        y_scr[p:p + 1, :] = jnp.sum(s_new * ct, axis=0, keepdims=True)
    y_ref[...] = (y_scr[...] + d_ref[0] * xt) * _silu(z_ref[...])


def _ssd_step(layer, xbct, zat, smt3, state_t, dtb, alog, dvec):
    bd = xbct.shape[1]
    tile = lambda f: pl.BlockSpec((HD, bd), f)
    par = pl.BlockSpec((1, 1, bd), lambda h: (h, 0, 0))
    return pl.pallas_call(
        _ssd_step_kernel, grid=(HEADS,),
        in_specs=[tile(lambda h: (h, 0)), tile(lambda h: (HEADS + h // 2, 0)), tile(lambda h: (HEADS + 2 + h // 2, 0)),
                  tile(lambda h: (h, 0)), pl.BlockSpec((1, 1, bd), lambda h: (L_DT + h, 0, 0)), par, par, par,
                  pl.BlockSpec((1, 1, HD, HD, bd), lambda h: (layer, h, 0, 0, 0))],
        out_specs=[tile(lambda h: (h, 0)), pl.BlockSpec((1, HD, HD, bd), lambda h: (h, 0, 0, 0))],
        out_shape=[jax.ShapeDtypeStruct((GROUP, bd), F32), jax.ShapeDtypeStruct((HEADS, HD, HD, bd), F32)],
        scratch_shapes=[pltpu.VMEM((HD, bd), F32)],
        compiler_params=_cparams(("parallel",)), name="ssd_step",
    )(xbct, xbct, xbct, zat, smt3, dtb, alog, dvec, state_t)


def _gdn_step_kernel(q_ref, k_ref, v_ref, z_ref, br_ref, ar_ref, alog_ref, dtb_ref, ng_ref, st_ref, y_ref, so_ref):
    eg = jnp.exp(-jnp.exp(alog_ref[0]) * _softplus(ar_ref[0] + dtb_ref[0]))
    beta = jax.nn.sigmoid(br_ref[0])
    qt, kt, vt = q_ref[...], k_ref[...], v_ref[...]
    ks = jnp.zeros_like(vt)
    qs = jnp.zeros_like(vt)
    for dk in range(HD):
        s_old = st_ref[0, 0, dk]
        ks = ks + kt[dk:dk + 1, :] * s_old
        qs = qs + qt[dk:dk + 1, :] * s_old
    w = beta * (vt - eg * ks)
    o = eg * qs + jnp.sum(qt * kt, axis=0, keepdims=True) * w
    for dk in range(HD):
        so_ref[0, dk] = eg * st_ref[0, 0, dk] + kt[dk:dk + 1, :] * w
    on = o * lax.rsqrt(jnp.mean(o * o, axis=0, keepdims=True) + EPS) * ng_ref[...]
    y_ref[...] = on * _silu(z_ref[...])


def _gdn_step(layer, qkvt, zct, smt3, state_t, alog, dtb, ng_b):
    bd = qkvt.shape[1]
    tile = lambda f: pl.BlockSpec((HD, bd), f)
    par = pl.BlockSpec((1, 1, bd), lambda h: (h, 0, 0))
    return pl.pallas_call(
        _gdn_step_kernel, grid=(HEADS,),
        in_specs=[tile(lambda h: (h, 0)), tile(lambda h: (HEADS + h, 0)), tile(lambda h: (2 * HEADS + h, 0)),
                  tile(lambda h: (h, 0)), pl.BlockSpec((1, 1, bd), lambda h: (L_BETA + h, 0, 0)),
                  pl.BlockSpec((1, 1, bd), lambda h: (L_GA + h, 0, 0)), par, par, tile(lambda h: (0, 0)),
                  pl.BlockSpec((1, 1, HD, HD, bd), lambda h: (layer, h, 0, 0, 0))],
        out_specs=[tile(lambda h: (h, 0)), pl.BlockSpec((1, HD, HD, bd), lambda h: (h, 0, 0, 0))],
        out_shape=[jax.ShapeDtypeStruct((GROUP, bd), F32), jax.ShapeDtypeStruct((HEADS, HD, HD, bd), F32)],
        compiler_params=_cparams(("parallel",)), name="gdn_step",
    )(qkvt, qkvt, qkvt, zct, smt3, smt3, alog, dtb, ng_b, state_t)


def _outproj_kernel(ya_ref, yb_ref, yc_ref, yd_ref, w_ref, x_ref, gt_ref, o_ref):
    acc = _dot(_bf(ya_ref[...]), w_ref[0:GROUP, :])
    acc += _dot(_bf(yb_ref[...]), w_ref[GROUP:2 * GROUP, :])
    acc += _dot(_bf(yc_ref[...]), w_ref[2 * GROUP:3 * GROUP, :])
    acc += _dot(_bf(yd_ref[...]), w_ref[3 * GROUP:4 * GROUP, :])
    o_ref[...] = x_ref[...] + gt_ref[0] * acc


def _out_proj(ya, yb, yc, yd, w_bf, x2d, mod, *, tm, per_row, rows_per_batch):
    t, d = x2d.shape
    yspec = pl.BlockSpec((tm, GROUP), lambda i: (i, 0))
    return pl.pallas_call(
        _outproj_kernel, grid=(t // tm,),
        in_specs=[yspec, yspec, yspec, yspec, pl.BlockSpec((d, d), lambda i: (0, 0)),
                  pl.BlockSpec((tm, d), lambda i: (i, 0)), _mod_spec(per_row, tm, rows_per_batch, 2)],
        out_specs=pl.BlockSpec((tm, d), lambda i: (i, 0)),
        out_shape=jax.ShapeDtypeStruct((t, d), F32),
        compiler_params=_cparams(("parallel",)), name="out_proj",
    )(ya, yb, yc, yd, w_bf, x2d, mod)


def _outproj_s_kernel(yat_ref, ng_ref, yb_ref, yct_ref, yd_ref, w_ref, x_ref, gt_ref, o_ref):
    ya = yat_ref[...]
    ya = ya * lax.rsqrt(jnp.mean(ya * ya, axis=0, keepdims=True) + EPS) * ng_ref[...]
    acc = _tn3(ya, w_ref[0:GROUP, :])
    acc += _mm3(yb_ref[...], w_ref[GROUP:2 * GROUP, :])
    acc += _tn3(yct_ref[...], w_ref[2 * GROUP:3 * GROUP, :])
    acc += _mm3(yd_ref[...], w_ref[3 * GROUP:4 * GROUP, :])
    o_ref[...] = x_ref[...] + gt_ref[0] * acc


def _out_proj_s(yat, ng_b, yb, yct, yd, w_bf, x2d, mod):
    bd, d = x2d.shape
    full = lambda a: pl.BlockSpec(a.shape, lambda i: (0,) * a.ndim)
    return pl.pallas_call(
        _outproj_s_kernel, grid=(1,),
        in_specs=[full(yat), full(ng_b), full(yb), full(yct), full(yd), full(w_bf), full(x2d),
                  _mod_spec(True, bd, 1, 2)],
        out_specs=pl.BlockSpec((bd, d), lambda i: (0, 0)),
        out_shape=jax.ShapeDtypeStruct((bd, d), F32),
        compiler_params=_cparams(("arbitrary",)), name="out_proj_sample",
    )(yat, ng_b, yb, yct, yd, w_bf, x2d, mod)


def _ffn_kernel(x_ref, g_ref, sc_ref, sh_ref, gt_ref, wg_ref, wu_ref, wd_ref, o_ref, h_scr, acc_scr, *, precise):
    f, nf = pl.program_id(1), pl.num_programs(1)
    mm = _mm3 if precise else (lambda a, b: _dot(_bf(a), b))

    @pl.when(f == 0)
    def _():
        h_scr[...] = _normmod(x_ref[...], g_ref[...], sc_ref[0], sh_ref[0]).astype(h_scr.dtype)
        acc_scr[...] = jnp.zeros_like(acc_scr)

    h = h_scr[...]
    a = _silu(mm(h, wg_ref[...])) * mm(h, wu_ref[...])
    acc_scr[...] += mm(a, wd_ref[...])

    @pl.when(f == nf - 1)
    def _():
        o_ref[...] = x_ref[...] + gt_ref[0] * acc_scr[...]


def _ffn_dense(x2d, g, mod, wg, wu, wd, *, tm, tf, per_row, rows_per_batch, precise=False):
    t, d = x2d.shape
    ff = wg.shape[1]
    ms = functools.partial(_mod_spec, per_row, tm, rows_per_batch)
    return pl.pallas_call(
        functools.partial(_ffn_kernel, precise=precise), grid=(t // tm, ff // tf),
        in_specs=[pl.BlockSpec((tm, d), lambda i, f: (i, 0)), pl.BlockSpec((1, d), lambda i, f: (0, 0)),
                  ms(4), ms(3), ms(5),
                  pl.BlockSpec((d, tf), lambda i, f: (0, f)), pl.BlockSpec((d, tf), lambda i, f: (0, f)),
                  pl.BlockSpec((tf, d), lambda i, f: (f, 0))],
        out_specs=pl.BlockSpec((tm, d), lambda i, f: (i, 0)),
        out_shape=jax.ShapeDtypeStruct((t, d), F32),
        scratch_shapes=[pltpu.VMEM((tm, d), F32 if precise else BF16), pltpu.VMEM((tm, d), F32)],
        compiler_params=_cparams(("parallel", "arbitrary")), name="ffn_dense",
    )(x2d, g, mod, mod, mod, wg, wu, wd)


def _moe_kernel(x_ref, g_ref, sc_ref, sh_ref, gt_ref, r_ref, wg_ref, wu_ref, wd_ref, o_ref,
                h_scr, acc_scr, gate_scr, gatet_scr, rank_scr, rankt_scr, hc_scr, gc_scr, yc_scr, count_scr,
                *, n_experts, cap):
    e, f = pl.program_id(1), pl.program_id(2)
    ne, nf = pl.num_programs(1), pl.num_programs(2)
    tm = x_ref.shape[0]
    lane = _iota((tm, LANES), 1)

    @pl.when(jnp.logical_and(e == 0, f == 0))
    def _():
        hf = _normmod(x_ref[...], g_ref[...], sc_ref[0], sh_ref[0])
        h_scr[...] = _bf(hf)
        acc_scr[...] = jnp.zeros_like(acc_scr)
        lane_f = lane.astype(F32)
        logits = jnp.where(lane < n_experts, _mm3(hf, r_ref[...]), NEG)
        v1 = jnp.max(logits, axis=1, keepdims=True)
        i1 = jnp.min(jnp.where(logits == v1, lane_f, float(LANES)), axis=1, keepdims=True)
        rest = jnp.where(lane_f == i1, NEG, logits)
        v2 = jnp.max(rest, axis=1, keepdims=True)
        i2 = jnp.min(jnp.where(rest == v2, lane_f, float(LANES)), axis=1, keepdims=True)
        e2 = jnp.exp(v2 - v1)
        w1 = 1.0 / (1.0 + e2)
        gates = jnp.where(lane_f == i1, w1, jnp.where(lane_f == i2, e2 * w1, 0.0))
        gate_scr[...] = gates
        gates_t = _tr(gates)
        gatet_scr[...] = gates_t
        on, on_t = _bf(jnp.where(gates != 0.0, 1.0, 0.0)), _bf(jnp.where(gates_t != 0.0, 1.0, 0.0))
        ri, ci = _iota((tm, tm), 0), _iota((tm, tm), 1)
        rank_scr[...] = _dot(_bf(ri > ci), on)
        rankt_scr[...] = _dot(on_t, _bf(ri < ci))
        totals = jnp.sum(on.astype(F32), axis=0, keepdims=True)
        for k in range(n_experts):
            count_scr[k] = totals[0, k].astype(jnp.int32)

    dense_rows = 512
    ge = jnp.sum(jnp.where(lane == e, gate_scr[...], 0.0), axis=1, keepdims=True)
    ge_row = gatet_scr[pl.ds(e, 1), :]
    on_col = jnp.where(ge != 0.0, 1.0, 0.0)
    on_row = jnp.where(ge_row != 0.0, 1.0, 0.0)
    count = count_scr[e]

    @pl.when(count <= cap)
    def _():
        @pl.when(f == 0)
        def _():
            rank_row = rankt_scr[pl.ds(e, 1), :]
            slot_r = _iota((cap, tm), 0).astype(F32)
            pick = _bf(jnp.where(jnp.logical_and(rank_row == slot_r, on_row > 0.5), 1.0, 0.0))
            hc_scr[...] = _bf(_dot(pick, h_scr[...]))
            gc_scr[...] = _mm01(pick, jnp.broadcast_to(ge, (tm, LANES)))

        hc = hc_scr[...]
        a = _silu(_dot(hc, wg_ref[0])) * _dot(hc, wu_ref[0])
        part = _dot(_bf(a), wd_ref[0])

        @pl.when(f == 0)
        def _():
            yc_scr[...] = part

        @pl.when(f > 0)
        def _():
            yc_scr[...] += part

        @pl.when(f == nf - 1)
        def _():
            rank_col = jnp.sum(jnp.where(lane == e, rank_scr[...], 0.0), axis=1, keepdims=True)
            slot_c = _iota((tm, cap), 1).astype(F32)
            spread = _bf(jnp.where(jnp.logical_and(rank_col == slot_c, on_col > 0.5), 1.0, 0.0))
            yh, yl = _hi_lo(yc_scr[...] * gc_scr[:, 0:1])
            acc_scr[...] += _dot(spread, yh) + _dot(spread, yl)

    @pl.when(count > cap)
    def _():
        for r0 in range(0, tm, dense_rows):
            rows = slice(r0, min(r0 + dense_rows, tm))
            hh = h_scr[rows, :]
            a = _silu(_dot(hh, wg_ref[0])) * _dot(hh, wu_ref[0]) * ge[rows, :]
            acc_scr[rows, :] += _dot(_bf(a), wd_ref[0])

    @pl.when(jnp.logical_and(e == ne - 1, f == nf - 1))
    def _():
        o_ref[...] = x_ref[...] + gt_ref[0] * acc_scr[...]


def _ffn_moe(x2d, g, mod, router_pad, wg, wu, wd, *, tm, tf, per_row, rows_per_batch):
    t, d = x2d.shape
    ne, _, ff = wg.shape
    ms = functools.partial(_mod_spec, per_row, tm, rows_per_batch)
    cap = -(-tm * 5 // 16 // 16) * 16
    return pl.pallas_call(
        functools.partial(_moe_kernel, n_experts=ne, cap=cap), grid=(t // tm, ne, ff // tf),
        in_specs=[pl.BlockSpec((tm, d), lambda i, e, f: (i, 0)), pl.BlockSpec((1, d), lambda i, e, f: (0, 0)),
                  ms(4), ms(3), ms(5), pl.BlockSpec((d, LANES), lambda i, e, f: (0, 0)),
                  pl.BlockSpec((1, d, tf), lambda i, e, f: (e, 0, f)), pl.BlockSpec((1, d, tf), lambda i, e, f: (e, 0, f)),
                  pl.BlockSpec((1, tf, d), lambda i, e, f: (e, f, 0))],
        out_specs=pl.BlockSpec((tm, d), lambda i, e, f: (i, 0)),
        out_shape=jax.ShapeDtypeStruct((t, d), F32),
        scratch_shapes=[pltpu.VMEM((tm, d), BF16), pltpu.VMEM((tm, d), F32), pltpu.VMEM((tm, LANES), F32),
                        pltpu.VMEM((LANES, tm), F32), pltpu.VMEM((tm, LANES), F32), pltpu.VMEM((LANES, tm), F32),
                        pltpu.VMEM((cap, d), BF16), pltpu.VMEM((cap, LANES), F32), pltpu.VMEM((cap, d), F32),
                        pltpu.SMEM((ne,), jnp.int32)],
        compiler_params=_cparams(("parallel", "arbitrary", "arbitrary"), vmem_mb=56), name="ffn_moe",
    )(x2d, g, mod, mod, mod, router_pad, wg, wu, wd)


def _lane_row(vals, offset):
    return jnp.zeros((1, LANES), F32).at[0, offset:offset + vals.shape[0]].set(vals.astype(F32))


def _reorder_w_in(w):
    pad = jnp.zeros((w.shape[0], LANES - 16), w.dtype)
    return jnp.concatenate([w[:, 256:768], w[:, 0:256], w[:, 772:1540], w[:, 1544:2312], w[:, 2320:2576],
                            w[:, 2576:3344], w[:, 768:772], w[:, 1540:1544], w[:, 2312:2320], pad], axis=1)


def _tile_heads(g):
    return jnp.tile(g.astype(F32), HEADS).reshape(1, GROUP)


def _per_head_rows(vals, bd):
    return jnp.broadcast_to(vals.astype(F32).reshape(HEADS, 1, 1), (HEADS, 1, bd))


def kernel(x_prompt, x_sample, cache_fox_k, cache_fox_v, cache_fox_logf, cache_moba_k, cache_moba_v, state_ssm, state_ssm_conv, state_gdn, state_gdn_conv, page_table, c_prompt, c_sample, w_ada, b_ada, norm_mix, norm_ffn, w_in, w_out, ssd_conv_w, ssd_conv_b, ssd_dt_bias, ssd_a_log, ssd_d, ssd_norm, fox_b_f, fox_q_norm, fox_k_norm, gdn_conv_w, gdn_a_log, gdn_dt_bias, gdn_norm, moba_q_norm, moba_k_norm, ffn_w_gate, ffn_w_up, ffn_w_down, moe_router, moe_w_gate, moe_w_up, moe_w_down):
    bp, ln, d = x_prompt.shape
    bd = x_sample.shape[0]
    depth = w_in.shape[0]
    n_pages = page_table.shape[1]
    n_pool = cache_fox_k.shape[1]
    assert x_sample.shape[1] == 1 and d == D_MODEL and ln % MOBA_BLOCK == 0
    tp = bp * ln
    tm_p = 512 if ln % 512 == 0 else 256
    t_attn = 256

    mod = _ada_mod(jnp.concatenate([c_prompt, c_sample], axis=0), w_ada, b_ada)
    pt_flat = page_table.reshape(-1).astype(jnp.int32)
    page_t = lambda c: jnp.transpose(c, (0, 1, 3, 4, 2))
    fox_kc, fox_vc, moba_kc, moba_vc = page_t(cache_fox_k), page_t(cache_fox_v), page_t(cache_moba_k), page_t(cache_moba_v)
    fox_lfc = jnp.transpose(cache_fox_logf, (0, 1, 3, 2)).reshape(depth, n_pool, 1, HEADS * PAGE)
    ssm_t = jnp.transpose(state_ssm, (0, 2, 3, 4, 1))
    gdn_t = jnp.transpose(state_gdn, (0, 2, 3, 4, 1))
    xp = x_prompt.reshape(tp, d)
    xs = x_sample.reshape(bd, d)
    outs_p, outs_s = [], []
    for l in range(depth):
        mod_p = mod[l, :bp].reshape(bp, 1, 6 * d)
        mod_s = mod[l, bp:].reshape(1, bd, 6 * d)
        kw_p = dict(tm=tm_p, per_row=False, rows_per_batch=ln)
        kw_s = dict(tm=bd, per_row=True, rows_per_batch=1)
        g_mix = norm_mix[l].reshape(1, d)
        g_ffn = norm_ffn[l].reshape(1, d)
        w_in_f = _reorder_w_in(w_in[l])
        w_in_bf = _bf(w_in_f)
        w_out_bf = _bf(w_out[l])
        gfq, gfk = _tile_heads(fox_q_norm[l]), _tile_heads(fox_k_norm[l])
        gmq, gmk = _tile_heads(moba_q_norm[l]), _tile_heads(moba_k_norm[l])
        bf_full = _lane_row(fox_b_f[l], L_FF)
        scw, scb, gcw = ssd_conv_w[l], ssd_conv_b[l].reshape(1, XBC_W), gdn_conv_w[l]

        proj = _in_proj(xp, g_mix, mod_p, w_in_bf, tn=PROJ_COLS, **kw_p).reshape(bp, ln, PROJ_COLS)
        fqa, fkt, fkx, fvt, mqa, mkt, mvt, lft, kmean = _prep(proj, gfq, gfk, gmq, gmk, bf_full, tm=MOBA_BLOCK)
        y_a, ssm_h, ssm_buf = _ssd_prompt(proj, scw, scb, _lane_row(ssd_dt_bias[l], L_DT), _lane_row(ssd_a_log[l], L_DT),
                                          _lane_row(ssd_d[l], L_DT), ssd_norm[l].reshape(1, GROUP))
        y_c, gdn_s, gdn_buf = _gdn_prompt(proj, gcw, _lane_row(gdn_a_log[l], L_GA), _lane_row(gdn_dt_bias[l], L_GA),
                                          _tile_heads(gdn_norm[l]))
        y_b = _fox_prompt(fqa, fkt, fkx, fvt, t=t_attn, wide=2 if ln % (2 * t_attn) == 0 else 1)
        nblk = ln // MOBA_BLOCK
        kmean_tiles = jnp.pad(kmean.reshape(bp, nblk, HEADS, HD), ((0, 0), (0, LANES - nblk), (0, 0), (0, LANES - HD)))
        y_d = _moba_prompt(mqa, mkt, mvt, kmean_tiles.reshape(bp, LANES, HEADS * LANES))
        flat = lambda a: a.reshape(tp, GROUP)
        xp = _out_proj(flat(y_a), flat(y_b), flat(y_c), flat(y_d), w_out_bf, xp, mod_p, **kw_p)
        outs_p.append((fkt, fvt, lft, mkt, mvt, ssm_h, ssm_buf, gdn_s, gdn_buf))

        proj_s = _in_proj(xs, g_mix, mod_s, w_in_f, tn=PROJ_COLS // 9, precise=True, **kw_s)
        (sq, sk, smq, smk, slf, sfkt, sfvt, smkt, smvt, slft, xbct, zat, smt, sbo, qkvt, zct, gbo) = _sprep(
            proj_s, jnp.transpose(state_ssm_conv[l], (1, 0, 2)), jnp.transpose(state_gdn_conv[l], (1, 0, 2)),
            gfq, gfk, gmq, gmk, bf_full, scw, scb, gcw)
        smt3 = smt.reshape(LANES, 1, bd)
        yat, ssm_new = _ssd_step(l, xbct, zat, smt3, ssm_t, _per_head_rows(ssd_dt_bias[l], bd),
                                 _per_head_rows(ssd_a_log[l], bd), _per_head_rows(ssd_d[l], bd))
        yct, gdn_new = _gdn_step(l, qkvt, zct, smt3, gdn_t, _per_head_rows(gdn_a_log[l], bd),
                                 _per_head_rows(gdn_dt_bias[l], bd),
                                 jnp.broadcast_to(gdn_norm[l].astype(F32).reshape(HD, 1), (HD, bd)))
        rows = lambda a: a.reshape(bd, 1, a.shape[-1])
        ys_b = _fox_decode(pt_flat, l, rows(sq), rows(sk), rows(proj_s), rows(slf), fox_kc, fox_vc, fox_lfc[l], n_pages=n_pages)
        ys_d = _moba_decode(pt_flat, l, rows(smq), rows(smk), rows(proj_s), moba_kc, moba_vc, n_pages=n_pages)
        xs = _out_proj_s(yat, jnp.broadcast_to(ssd_norm[l].astype(F32).reshape(GROUP, 1), (GROUP, bd)),
                         ys_b.reshape(bd, GROUP), yct, ys_d.reshape(bd, GROUP), w_out[l], xs, mod_s)
        outs_s.append((sfkt, sfvt, slft, smkt, smvt, ssm_new, sbo, gdn_new, gbo))

        i = l // 2
        if l % 2 == 0:
            wg, wu, wd = _bf(ffn_w_gate[i]), _bf(ffn_w_up[i]), _bf(ffn_w_down[i])
            tf = wg.shape[1] // 2
            xp = _ffn_dense(xp, g_ffn, mod_p, wg, wu, wd, tf=tf, **kw_p)
            xs = _ffn_dense(xs, g_ffn, mod_s, ffn_w_gate[i], ffn_w_up[i], ffn_w_down[i], tf=2 * LANES, precise=True, **kw_s)
        else:
            wg, wu, wd = _bf(moe_w_gate[i]), _bf(moe_w_up[i]), _bf(moe_w_down[i])
            ne = wg.shape[0]
            router_pad = jnp.pad(moe_router[i], ((0, 0), (0, LANES - ne)))
            tf = wg.shape[2]
            wide_p = ln % 1024 == 0 and tf % (2 * LANES) == 0
            xp = _ffn_moe(xp, g_ffn, mod_p, router_pad, wg, wu, wd, tf=tf // 2 if wide_p else tf,
                          **dict(kw_p, tm=1024 if wide_p else tm_p))
            xs = _ffn_moe(xs, g_ffn, mod_s, router_pad, wg, wu, wd, tf=tf, **kw_s)

    stack = lambda outs, j: jnp.stack([o[j] for o in outs], axis=0)
    kv_p = lambda j: jnp.transpose(stack(outs_p, j), (0, 1, 4, 2, 3))
    kv_s = lambda j: jnp.transpose(stack(outs_s, j).reshape(depth, HEADS, HD, bd), (0, 3, 1, 2)).reshape(depth, bd, 1, HEADS, HD)
    st_s = lambda j: jnp.transpose(stack(outs_s, j), (0, 4, 1, 2, 3))
    buf_s = lambda j: jnp.transpose(stack(outs_s, j), (0, 2, 1, 3))
    return (xp.reshape(bp, ln, d), xs.reshape(bd, 1, d),
            kv_p(0), kv_p(1), jnp.transpose(stack(outs_p, 2), (0, 1, 3, 2)), kv_p(3), kv_p(4),
            stack(outs_p, 5), stack(outs_p, 6), stack(outs_p, 7), stack(outs_p, 8),
            kv_s(0), kv_s(1), jnp.transpose(stack(outs_s, 2), (0, 2, 1)).reshape(depth, bd, 1, HEADS), kv_s(3), kv_s(4),
            st_s(5), buf_s(6), st_s(7), buf_s(8))
```

```python
import functools

import jax
import jax.numpy as jnp
from jax import lax
from jax.experimental import pallas as pl
from jax.experimental.pallas import tpu as pltpu

F32 = jnp.float32
BF16 = jnp.bfloat16
EPS = 1e-6
NEG = -1e30

D_MODEL = 1024
GROUP = 256
HEADS = 4
HD = 64
PAGE = 128
MOBA_BLOCK = 256
MOBA_TOPK = 3
CONV_W = 4
SSD_CHUNK = 128
GDN_CHUNK = 64
GDN_TOK = 256
LANES = 128

XBC_W, QKV_W = 512, 768
COL_XBC = 0
COL_ZA = 2
COL_FQ, COL_FK, COL_FV = 3, 4, 5
COL_QKV = 2
COL_ZC = 9
COL_MQ, COL_MK, COL_MV = 10, 11, 12
COL_SMALL = 26
PROJ_COLS = 3456
L_DT, L_FF, L_BETA, L_GA = 0, 4, 8, 12


def _dot(a, b):
    return jnp.dot(a, b, preferred_element_type=F32)


def _dg(a, b, ca, cb):
    return lax.dot_general(a, b, (((ca,), (cb,)), ((), ())), preferred_element_type=F32)


def _bf(a):
    return a.astype(BF16)


def _hi_lo(a):
    hi = _bf(a)
    return hi, _bf(a - hi.astype(F32))


def _split3(a):
    a1 = _bf(a)
    r = a - a1.astype(F32)
    a2 = _bf(r)
    return a1, a2, _bf(r - a2.astype(F32))


def _mm(a, b):
    return _dot(_bf(a), _bf(b))


def _nt(a, b):
    return _dg(_bf(a), _bf(b), 1, 1)


def _tn(a, b):
    return _dg(_bf(a), _bf(b), 0, 0)


def _x3(f, a, b):
    a1, a2 = _hi_lo(a)
    b1, b2 = _hi_lo(b)
    return f(a1, b1) + (f(a1, b2) + f(a2, b1))


def _mm3(a, b):
    return _x3(_dot, a, b)


def _nt3(a, b):
    return _x3(lambda x, y: _dg(x, y, 1, 1), a, b)


def _tn3(a, b):
    return _x3(lambda x, y: _dg(x, y, 0, 0), a, b)


def _hi_lo_f32(a):
    hi = _bf(a).astype(F32)
    return hi, a - hi


def _mm3c(a, b):
    a1, a2 = _hi_lo_f32(a)
    b1, b2 = _hi_lo_f32(b)
    return _dot(_bf(jnp.concatenate([a1, a2, a1], axis=1)), _bf(jnp.concatenate([b1, b1, b2], axis=0)))


def _nt3c(a, b):
    a1, a2 = _hi_lo_f32(a)
    b1, b2 = _hi_lo_f32(b)
    return _dg(_bf(jnp.concatenate([a1, a2, a1], axis=1)), _bf(jnp.concatenate([b1, b1, b2], axis=1)), 1, 1)


def _tn3c(a, b):
    a1, a2 = _hi_lo_f32(a)
    b1, b2 = _hi_lo_f32(b)
    return _dg(_bf(jnp.concatenate([a1, a2, a1], axis=0)), _bf(jnp.concatenate([b1, b1, b2], axis=0)), 0, 0)


def _mm01(m01, x):
    x1, x2, x3 = _split3(x)
    return _dot(m01, x1) + (_dot(m01, x2) + _dot(m01, x3))


def _mmx01(x, m01):
    x1, x2, x3 = _split3(x)
    return _dot(x1, m01) + (_dot(x2, m01) + _dot(x3, m01))


def _tr(x):
    n = x.shape[1]
    eye = _bf(lax.broadcasted_iota(jnp.int32, (n, n), 0) == lax.broadcasted_iota(jnp.int32, (n, n), 1))
    x1, x2, x3 = _split3(x)
    return _dg(eye, x1, 1, 1) + (_dg(eye, x2, 1, 1) + _dg(eye, x3, 1, 1))


def _iota(shape, dim):
    return lax.broadcasted_iota(jnp.int32, shape, dim)


def _silu(x):
    return x * jax.nn.sigmoid(x)


def _softplus(x):
    return jnp.maximum(x, 0.0) + jnp.log1p(jnp.exp(-jnp.abs(x)))


def _log_sigmoid(x):
    return -_softplus(-x)


def _head_mat():
    r = lax.shift_right_logical(_iota((GROUP, GROUP), 0), 6)
    c = lax.shift_right_logical(_iota((GROUP, GROUP), 1), 6)
    return _bf(r == c)


def _headsum(x, hm):
    x1, x2 = _hi_lo(x)
    return _dot(x1, hm) + _dot(x2, hm)


def _normmod(x, g, sc, sh):
    y = x * lax.rsqrt(jnp.mean(x * x, axis=-1, keepdims=True) + EPS) * g
    return y * (1.0 + sc) + sh


def _cparams(sem, vmem_mb=48):
    return pltpu.CompilerParams(dimension_semantics=sem, vmem_limit_bytes=vmem_mb * 1024 * 1024)


def _mod_spec(per_row, tm, rows_per_batch, chunk):
    if per_row:
        return pl.BlockSpec((1, tm, D_MODEL), lambda i, *_: (0, i, chunk))
    return pl.BlockSpec((1, 1, D_MODEL), lambda i, *_: ((i * tm) // rows_per_batch, 0, chunk))


def _ada_kernel(c_ref, w_ref, b_ref, o_ref):
    o_ref[0] = _mm3(_silu(c_ref[...]), w_ref[0]) + b_ref[0]


def _ada_mod(c_all, w_ada, b_ada):
    depth, d, n = w_ada.shape
    r = c_all.shape[0]
    tn = 1536
    return pl.pallas_call(
        _ada_kernel, grid=(depth, n // tn),
        in_specs=[pl.BlockSpec((r, d), lambda l, j: (0, 0)),
                  pl.BlockSpec((1, d, tn), lambda l, j: (l, 0, j)),
                  pl.BlockSpec((1, 1, tn), lambda l, j: (l, 0, j))],
        out_specs=pl.BlockSpec((1, r, tn), lambda l, j: (l, 0, j)),
        out_shape=jax.ShapeDtypeStruct((depth, r, n), F32),
        compiler_params=_cparams(("parallel", "parallel")), name="ada_mod",
    )(c_all, w_ada, b_ada.reshape(depth, 1, n))


def _inproj_kernel(x_ref, g_ref, sc_ref, sh_ref, w_ref, o_ref, *, precise):
    h = _normmod(x_ref[...], g_ref[...], sc_ref[0], sh_ref[0])
    o_ref[...] = _mm3(h, w_ref[...]) if precise else _dot(_bf(h), w_ref[...])


def _in_proj(x2d, g, mod, w, *, tm, tn, per_row, rows_per_batch, precise=False):
    t, d = x2d.shape
    n = w.shape[1]
    return pl.pallas_call(
        functools.partial(_inproj_kernel, precise=precise), grid=(t // tm, n // tn),
        in_specs=[pl.BlockSpec((tm, d), lambda i, j: (i, 0)),
                  pl.BlockSpec((1, d), lambda i, j: (0, 0)),
                  _mod_spec(per_row, tm, rows_per_batch, 1),
                  _mod_spec(per_row, tm, rows_per_batch, 0),
                  pl.BlockSpec((d, tn), lambda i, j: (0, j))],
        out_specs=pl.BlockSpec((tm, tn), lambda i, j: (i, j)),
        out_shape=jax.ShapeDtypeStruct((t, n), F32),
        compiler_params=_cparams(("parallel", "arbitrary")), name="in_proj",
    )(x2d, g, mod, mod, w)


def _split3_f32(a):
    a1 = _bf(a).astype(F32)
    r = a - a1
    a2 = _bf(r).astype(F32)
    return a1, a2, _bf(r - a2).astype(F32)


def _query_tiles(qn, extra_fn, scale):
    return jnp.concatenate([jnp.concatenate([qn[:, h * HD:(h + 1) * HD] * scale, extra_fn(h)], axis=1)
                            for h in range(HEADS)], axis=1)


def _prep_kernel(fq_ref, fk_ref, fv_ref, mq_ref, mk_ref, mv_ref, sm_ref, gfq_ref, gfk_ref, gmq_ref, gmk_ref, bf_ref,
                 fqa_ref, fkt_ref, fkx_ref, fvt_ref, mqa_ref, mkt_ref, mvt_ref, lft_ref, kmean_ref, carry_scr):
    hm = _head_mat()
    tm = sm_ref.shape[1]
    scale = HD ** -0.5

    def hnorm(x, g):
        return x * lax.rsqrt(_headsum(x * x, hm) * (1.0 / HD) + EPS) * g

    def put_t(ref, x):
        xt = _tr(x)
        for h in range(HEADS):
            ref[0, h] = xt[h * HD:(h + 1) * HD, :]

    put_t(fkt_ref, hnorm(fk_ref[0], gfk_ref[...]))
    put_t(fvt_ref, fv_ref[0])
    mkn = hnorm(mk_ref[0], gmk_ref[...])
    put_t(mkt_ref, mkn)
    put_t(mvt_ref, mv_ref[0])
    kmean_ref[0, 0] = jnp.mean(mkn, axis=0, keepdims=True)
    lf = _log_sigmoid(sm_ref[0] + bf_ref[...])
    lft_ref[0] = _tr(lf)[L_FF:L_FF + HEADS, :]

    @pl.when(pl.program_id(1) == 0)
    def _():
        carry_scr[...] = jnp.zeros_like(carry_scr)

    tril = _bf(_iota((tm, tm), 0) >= _iota((tm, tm), 1))
    cum = _mm01(tril, lf) + carry_scr[...]
    carry_scr[...] = cum[tm - 1:tm, :]
    cum_t = _tr(cum)
    lane64 = _iota((tm, HD), 1)
    row8 = _iota((8, tm), 0)

    def fox_extra(h):
        c1, c2, c3 = _split3_f32(cum[:, L_FF + h:L_FF + h + 1])
        return jnp.where(lane64 == 0, c1, jnp.where(lane64 == 1, c2, jnp.where(lane64 == 2, c3,
                         jnp.where(lane64 < 6, 1.0, 0.0))))

    fqa_ref[0] = _query_tiles(hnorm(fq_ref[0], gfq_ref[...]), fox_extra, scale)
    for h in range(HEADS):
        k1, k2, k3 = _split3_f32(cum_t[L_FF + h:L_FF + h + 1, :])
        fkx_ref[0, h] = jnp.where(row8 < 3, 1.0, jnp.where(row8 == 3, -k1, jnp.where(row8 == 4, -k2,
                                  jnp.where(row8 == 5, -k3, 0.0))))

    q_hi = (pl.program_id(1) * tm).astype(F32)
    q_lo = _iota((tm, HD), 0).astype(F32)

    def moba_extra(h):
        slope = 2.0 ** (-2 * (h + 1))
        return jnp.where(lane64 == 0, -slope * q_hi, jnp.where(lane64 == 1, -slope * q_lo,
                         jnp.where(lane64 < 4, slope, 0.0)))

    mqa_ref[0] = _query_tiles(hnorm(mq_ref[0], gmq_ref[...]), moba_extra, scale)


def _prep(proj3, gfq, gfk, gmq, gmk, bf_full, *, tm):
    nb, ln, _ = proj3.shape

    def col(c):
        return pl.BlockSpec((1, tm, GROUP), lambda b, i: (b, i, c))

    assert tm == MOBA_BLOCK
    row = pl.BlockSpec((1, GROUP), lambda b, i: (0, 0))
    q_blk = pl.BlockSpec((1, tm, HEADS * LANES), lambda b, i: (b, i, 0))
    t_blk = pl.BlockSpec((1, HEADS, HD, tm), lambda b, i: (b, 0, 0, i))
    q_shape = jax.ShapeDtypeStruct((nb, ln, HEADS * LANES), F32)
    t_shape = jax.ShapeDtypeStruct((nb, HEADS, HD, ln), F32)
    return pl.pallas_call(
        _prep_kernel, grid=(nb, ln // tm),
        in_specs=[col(COL_FQ), col(COL_FK), col(COL_FV), col(COL_MQ), col(COL_MK), col(COL_MV),
                  pl.BlockSpec((1, tm, LANES), lambda b, i: (b, i, COL_SMALL)),
                  row, row, row, row, pl.BlockSpec((1, LANES), lambda b, i: (0, 0))],
        out_specs=[q_blk, t_blk, pl.BlockSpec((1, HEADS, 8, tm), lambda b, i: (b, 0, 0, i)), t_blk, q_blk, t_blk, t_blk,
                   pl.BlockSpec((1, HEADS, tm), lambda b, i: (b, 0, i)),
                   pl.BlockSpec((1, 1, 1, GROUP), lambda b, i: (b, i, 0, 0))],
        out_shape=[q_shape, t_shape, jax.ShapeDtypeStruct((nb, HEADS, 8, ln), F32), t_shape, q_shape, t_shape, t_shape,
                   jax.ShapeDtypeStruct((nb, HEADS, ln), F32), jax.ShapeDtypeStruct((nb, ln // tm, 1, GROUP), F32)],
        scratch_shapes=[pltpu.VMEM((1, LANES), F32)],
        compiler_params=_cparams(("parallel", "arbitrary")), name="attn_prep",
    )(proj3, proj3, proj3, proj3, proj3, proj3, proj3, gfq, gfk, gmq, gmk, bf_full)


def _attn_update(s, vt, h, m_scr, l_scr, acc_scr):
    rep = s.shape[1] // LANES
    m_old = m_scr[h]
    m_new = jnp.maximum(m_old, jnp.max(s, axis=1, keepdims=True))
    p = jnp.exp(s - jnp.concatenate([m_new] * rep, axis=1))
    alpha = jnp.exp(m_old - m_new)
    l_scr[h] = alpha * l_scr[h] + jnp.sum(p, axis=1, keepdims=True)
    acc_scr[h] = alpha[:, 0:HD] * acc_scr[h] + _nt(p, vt)
    m_scr[h] = m_new


def _attn_init(m_scr, l_scr, acc_scr):
    m_scr[...] = jnp.full_like(m_scr, NEG)
    l_scr[...] = jnp.zeros_like(l_scr)
    acc_scr[...] = jnp.zeros_like(acc_scr)


def _attn_finish(o_ref, l_scr, acc_scr):
    for h in range(HEADS):
        o_ref[0, :, h * HD:(h + 1) * HD] = acc_scr[h] / l_scr[h][:, 0:HD]


def _attn_scratch(t):
    return [pltpu.VMEM((HEADS, t, LANES), F32), pltpu.VMEM((HEADS, t, LANES), F32), pltpu.VMEM((HEADS, t, HD), F32)]


def _causal_pairs(n):
    pairs = [(qb, kb) for qb in range(n) for kb in range(qb + 1)]
    return jnp.asarray([p[0] for p in pairs], jnp.int32), jnp.asarray([p[1] for p in pairs], jnp.int32)


def _key_tile(kt, extra):
    return jnp.concatenate([kt, extra, jnp.zeros((LANES - HD - 8, kt.shape[1]), F32)], axis=0)


def _fox_kernel(qtab_ref, ktab_ref, q_ref, kt_ref, kx_ref, vt_ref, o_ref, m_scr, l_scr, acc_scr, *, t, wide):
    step_id = pl.program_id(1)
    qi, ki = qtab_ref[step_id], ktab_ref[step_id]
    last = qi // wide

    @pl.when(ki == 0)
    def _():
        _attn_init(m_scr, l_scr, acc_scr)

    def step(diagonal):
        q = q_ref[0]
        for h in range(HEADS):
            s = _dot(_bf(q[:, h * LANES:(h + 1) * LANES]), _bf(_key_tile(kt_ref[0, h], kx_ref[0, h])))
            if diagonal:
                shift = (qi - last * wide) * t
                s = jnp.where(_iota((t, wide * t), 1) <= _iota((t, wide * t), 0) + shift, s, NEG)
            _attn_update(s, vt_ref[0, h], h, m_scr, l_scr, acc_scr)

    @pl.when(ki < last)
    def _():
        step(False)

    @pl.when(ki == last)
    def _():
        step(True)
        _attn_finish(o_ref, l_scr, acc_scr)


def _fox_prompt(qa, kt, kx, vt, *, t, wide):
    b, ln, _ = qa.shape
    assert ln % (wide * t) == 0
    pairs = [(qb, kb) for qb in range(ln // t) for kb in range(qb // wide + 1)]
    qtab = jnp.asarray([p[0] for p in pairs], jnp.int32)
    ktab = jnp.asarray([p[1] for p in pairs], jnp.int32)
    kv = pl.BlockSpec((1, HEADS, HD, wide * t), lambda bb, s, qt, kt_: (bb, 0, 0, kt_[s]))
    return pl.pallas_call(
        functools.partial(_fox_kernel, t=t, wide=wide),
        grid_spec=pltpu.PrefetchScalarGridSpec(
            num_scalar_prefetch=2, grid=(b, len(pairs)),
            in_specs=[pl.BlockSpec((1, t, HEADS * LANES), lambda bb, s, qt, kt_: (bb, qt[s], 0)), kv,
                      pl.BlockSpec((1, HEADS, 8, wide * t), lambda bb, s, qt, kt_: (bb, 0, 0, kt_[s])), kv],
            out_specs=pl.BlockSpec((1, t, GROUP), lambda bb, s, qt, kt_: (bb, qt[s], 0)),
            scratch_shapes=_attn_scratch(t)),
        out_shape=jax.ShapeDtypeStruct((b, ln, GROUP), F32),
        compiler_params=_cparams(("parallel", "arbitrary")), name="fox_prompt",
    )(qtab, ktab, qa, kt, kx, vt)


def _moba_kernel(qtab_ref, jtab_ref, q_ref, kt_ref, vt_ref, km_ref, o_ref, m_scr, l_scr, acc_scr, sel_scr):
    t = MOBA_BLOCK
    step_id = pl.program_id(1)
    qi, j = qtab_ref[step_id], jtab_ref[step_id]
    lane = _iota((t, LANES), 1)
    row8 = _iota((8, t), 0)

    def key_extra(blk):
        k_hi = (blk * t).astype(F32)
        k_lo = _iota((8, t), 1).astype(F32)
        return jnp.where(row8 < 2, 1.0, jnp.where(row8 == 2, k_hi, jnp.where(row8 == 3, k_lo, 0.0)))

    @pl.when(j == 0)
    def _():
        _attn_init(m_scr, l_scr, acc_scr)
        q, km = q_ref[0], km_ref[0]
        lane_f = lane.astype(F32)
        causal = _iota((t, t), 1) <= _iota((t, t), 0)
        kx = key_extra(qi)
        for h in range(HEADS):
            tl = slice(h * LANES, (h + 1) * LANES)
            gate = jnp.where(lane < qi, _nt3(q[:, tl], km[:, tl]), NEG)
            sel = jnp.zeros((t, LANES), F32)
            for k in range(MOBA_TOPK):
                mx = jnp.max(gate, axis=1, keepdims=True)
                idx = jnp.min(jnp.where(gate == mx, lane_f, float(LANES)), axis=1, keepdims=True)
                hit = lane_f == idx
                sel = jnp.where(hit, jnp.maximum(sel, jnp.where(qi > k, 1.0, 0.0)), sel)
                gate = jnp.where(hit, NEG, gate)
            sel_scr[h] = sel
            s = _dot(_bf(q[:, tl]), _bf(_key_tile(kt_ref[0, h], kx)))
            _attn_update(jnp.where(causal, s, NEG), vt_ref[0, h], h, m_scr, l_scr, acc_scr)

    @pl.when(jnp.logical_and(j >= 1, j <= qi))
    def _():
        q = q_ref[0]
        n = j - 1
        kx = key_extra(n)
        for h in range(HEADS):
            picked = jnp.sum(jnp.where(lane == n, sel_scr[h], 0.0), axis=1, keepdims=True)
            s = _dot(_bf(q[:, h * LANES:(h + 1) * LANES]), _bf(_key_tile(kt_ref[0, h], kx)))
            _attn_update(jnp.where(picked > 0.5, s, NEG), vt_ref[0, h], h, m_scr, l_scr, acc_scr)

    @pl.when(j == qi)
    def _():
        _attn_finish(o_ref, l_scr, acc_scr)


def _moba_prompt(qa, kt, vt, kmean_tiles):
    b, ln, _ = qa.shape
    t = MOBA_BLOCK
    n = ln // t
    qtab, jtab = _causal_pairs(n)

    def kv_idx(bb, s, qt, jt):
        return (bb, 0, 0, jnp.where(jt[s] == 0, qt[s], jt[s] - 1))

    kv = pl.BlockSpec((1, HEADS, HD, t), kv_idx)
    return pl.pallas_call(
        _moba_kernel,
        grid_spec=pltpu.PrefetchScalarGridSpec(
            num_scalar_prefetch=2, grid=(b, qtab.shape[0]),
            in_specs=[pl.BlockSpec((1, t, HEADS * LANES), lambda bb, s, qt, jt: (bb, qt[s], 0)), kv, kv,
                      pl.BlockSpec((1, LANES, HEADS * LANES), lambda bb, s, qt, jt: (bb, 0, 0))],
            out_specs=pl.BlockSpec((1, t, GROUP), lambda bb, s, qt, jt: (bb, qt[s], 0)),
            scratch_shapes=_attn_scratch(t) + [pltpu.VMEM((HEADS, t, LANES), F32)]),
        out_shape=jax.ShapeDtypeStruct((b, ln, GROUP), F32),
        compiler_params=_cparams(("parallel", "arbitrary")), name="moba_prompt",
    )(qtab, jtab, qa, kt, vt, kmean_tiles)


def _query_cols(q):
    return _tr(jnp.broadcast_to(q, (LANES, GROUP)))


def _head_dots(a, b):
    ind = _bf(lax.shift_right_logical(_iota((GROUP, LANES), 0), 6) == _iota((GROUP, LANES), 1))
    return _mmx01(jnp.broadcast_to(a * b, (8, GROUP)), ind)[0:1, :]


def _lane_sums(acc):
    a1, a2, a3 = _split3(acc)
    ones = jnp.ones((8, LANES), BF16)
    return (_dg(ones, a1, 1, 1) + (_dg(ones, a2, 1, 1) + _dg(ones, a3, 1, 1)))[0:1, :]


def _decode_head(srows, s_self, v_refs, h):
    mrow = srows[0]
    for r in srows[1:]:
        mrow = jnp.maximum(mrow, r)
    m = jnp.maximum(jnp.max(mrow, axis=1, keepdims=True), s_self)
    acc = jnp.zeros((HD, LANES), F32)
    lrow = jnp.zeros((1, LANES), F32)
    for j, r in enumerate(srows):
        p = jnp.exp(r - m)
        lrow = lrow + p
        acc = acc + p * v_refs[j][0, h * HD:(h + 1) * HD, :]
    w_self = jnp.exp(s_self - m)
    return acc, w_self, jnp.sum(lrow, axis=1, keepdims=True) + w_self


def _fox_dec_kernel(pt_ref, q_ref, kn_ref, vn_ref, lfn_ref, lfc_ref, *rest, n_pages, scale, page0):
    n = n_pages
    k_refs, v_refs = rest[:n], rest[n:2 * n]
    o_ref, qb_scr, lf_scr, acc_scr = rest[2 * n:]
    q = q_ref[0]
    qb_scr[...] = _query_cols(q)
    s_new = _head_dots(q, kn_ref[0]) * scale
    lfn = lfn_ref[0]
    first = pl.program_id(0) * n
    for j in range(n):
        row = lfc_ref[pt_ref[first + j] - page0]
        for h in range(HEADS):
            lf_scr[h * n + j:h * n + j + 1, :] = row[:, h * PAGE:(h + 1) * PAGE]
    lf_all = lf_scr[...]
    later = _bf(_iota((PAGE, PAGE), 0) > _iota((PAGE, PAGE), 1))
    suffix = _mmx01(lf_all, later)
    page_sum = jnp.sum(lf_all, axis=1, keepdims=True)
    grp = lax.shift_right_logical(_iota((1, GROUP), 1), 6)
    w_row = jnp.zeros((1, GROUP), F32)
    l_row = jnp.ones((1, GROUP), F32)
    for h in range(HEADS):
        carry = lfn[:, h:h + 1]
        offs = [None] * n
        for j in reversed(range(n)):
            offs[j] = carry
            carry = carry + page_sum[h * n + j:h * n + j + 1, :]
        qh = qb_scr[h * HD:(h + 1) * HD, :]
        srows = [jnp.sum(qh * k_refs[j][0, h * HD:(h + 1) * HD, :], axis=0, keepdims=True) * scale
                 + (suffix[h * n + j:h * n + j + 1, :] + offs[j]) for j in range(n)]
        acc, w_self, l = _decode_head(srows, s_new[:, h:h + 1], v_refs, h)
        acc_scr[h * HD:(h + 1) * HD, :] = acc
        w_row = jnp.where(grp == h, w_self, w_row)
        l_row = jnp.where(grp == h, l, l_row)
    o_ref[0] = (_lane_sums(acc_scr[...]) + w_row * vn_ref[0]) / l_row


def _fox_decode(pt_flat, layer, q, kn, proj_s3, lfn, kc, vc, lfc, *, n_pages):
    bd = q.shape[0]
    row = pl.BlockSpec((1, 1, GROUP), lambda b, pt: (b, 0, 0))

    def page(j):
        return lambda b, pt: (pt[b * n_pages + j], 0, 0)

    kv_specs = [pl.BlockSpec((1, GROUP, PAGE), page(j)) for j in range(n_pages)]
    return pl.pallas_call(
        functools.partial(_fox_dec_kernel, n_pages=n_pages, scale=HD ** -0.5, page0=layer * lfc.shape[0]),
        grid_spec=pltpu.PrefetchScalarGridSpec(
            num_scalar_prefetch=1, grid=(bd,),
            in_specs=[row, row, pl.BlockSpec((1, 1, GROUP), lambda b, pt: (b, 0, COL_FV)),
                      pl.BlockSpec((1, 1, HEADS), lambda b, pt: (b, 0, 0)),
                      pl.BlockSpec(lfc.shape, lambda b, pt: (0, 0, 0))] + kv_specs + kv_specs,
            out_specs=row,
            scratch_shapes=[pltpu.VMEM((GROUP, LANES), F32), pltpu.VMEM((HEADS * n_pages, PAGE), F32),
                            pltpu.VMEM((GROUP, LANES), F32)]),
        out_shape=jax.ShapeDtypeStruct((bd, 1, GROUP), F32),
        compiler_params=_cparams(("parallel",)), name="fox_decode",
    )(pt_flat, q, kn, proj_s3, lfn, lfc, *([kc] * n_pages), *([vc] * n_pages))


def _moba_dec_kernel(pt_ref, q_ref, kn_ref, vn_ref, *rest, n_pages, scale):
    del pt_ref
    n = n_pages
    ppb = MOBA_BLOCK // PAGE
    n_past = n // ppb
    past_len = n * PAGE
    k_refs, v_refs = rest[:n], rest[n:2 * n]
    o_ref, qb_scr, acc_scr = rest[2 * n:]
    q = q_ref[0]
    qb_scr[...] = _query_cols(q)
    s_own = _head_dots(q, kn_ref[0]) * scale
    rows8, lanes8 = _iota((8, LANES), 0), _iota((8, LANES), 1)
    raws = []
    gates = jnp.full((8, LANES), NEG, F32)
    for h in range(HEADS):
        qh = qb_scr[h * HD:(h + 1) * HD, :]
        raw_h = [jnp.sum(qh * k_refs[j][0, h * HD:(h + 1) * HD, :], axis=0, keepdims=True) for j in range(n)]
        raws.append(raw_h)
        for blk in range(n_past):
            tot = raw_h[ppb * blk]
            for j in range(ppb * blk + 1, ppb * (blk + 1)):
                tot = tot + raw_h[j]
            gate = jnp.sum(tot, axis=1, keepdims=True) * (1.0 / MOBA_BLOCK)
            gates = jnp.where(jnp.logical_and(rows8 == h, lanes8 == blk), gate, gates)
    lanes_f = lanes8.astype(F32)
    sel = jnp.zeros((8, LANES), F32)
    for _ in range(min(MOBA_TOPK, n_past)):
        mx = jnp.max(gates, axis=1, keepdims=True)
        idx = jnp.min(jnp.where(gates == mx, lanes_f, float(LANES)), axis=1, keepdims=True)
        hit = lanes_f == idx
        sel = jnp.where(hit, 1.0, sel)
        gates = jnp.where(hit, NEG, gates)
    lane_f = _iota((1, LANES), 1).astype(F32)
    grp = lax.shift_right_logical(_iota((1, GROUP), 1), 6)
    w_row = jnp.zeros((1, GROUP), F32)
    l_row = jnp.ones((1, GROUP), F32)
    for h in range(HEADS):
        slope = 2.0 ** (-2 * (h + 1))
        srows = []
        for j in range(n):
            keep = sel[h:h + 1, j // ppb:j // ppb + 1]
            s = raws[h][j] * scale - slope * (float(past_len - j * PAGE) - lane_f)
            srows.append(s * keep + (1.0 - keep) * NEG)
        acc, w_self, l = _decode_head(srows, s_own[:, h:h + 1], v_refs, h)
        acc_scr[h * HD:(h + 1) * HD, :] = acc
        w_row = jnp.where(grp == h, w_self, w_row)
        l_row = jnp.where(grp == h, l, l_row)
    o_ref[0] = (_lane_sums(acc_scr[...]) + w_row * vn_ref[0]) / l_row


def _moba_decode(pt_flat, layer, q, kn, proj_s3, kc, vc, *, n_pages):
    bd = q.shape[0]
    assert (n_pages * PAGE) % MOBA_BLOCK == 0 and n_pages * PAGE // MOBA_BLOCK <= LANES
    row = pl.BlockSpec((1, 1, GROUP), lambda b, pt: (b, 0, 0))

    def page(j):
        return lambda b, pt: (pt[b * n_pages + j], 0, 0)

    kv_specs = [pl.BlockSpec((1, GROUP, PAGE), page(j)) for j in range(n_pages)]
    return pl.pallas_call(
        functools.partial(_moba_dec_kernel, n_pages=n_pages, scale=HD ** -0.5),
        grid_spec=pltpu.PrefetchScalarGridSpec(
            num_scalar_prefetch=1, grid=(bd,),
            in_specs=[row, row, pl.BlockSpec((1, 1, GROUP), lambda b, pt: (b, 0, COL_MV))] + kv_specs + kv_specs,
            out_specs=row,
            scratch_shapes=[pltpu.VMEM((GROUP, LANES), F32), pltpu.VMEM((GROUP, LANES), F32)]),
        out_shape=jax.ShapeDtypeStruct((bd, 1, GROUP), F32),
        compiler_params=_cparams(("parallel",)), name="moba_decode",
    )(pt_flat, q, kn, proj_s3, *([kc] * n_pages), *([vc] * n_pages))


def _chunk_conv(u, cw, ext_scr, c_len):
    ext_scr[8:8 + c_len, :] = u
    out = (cw[3:4, :] * u + cw[2:3, :] * ext_scr[7:7 + c_len, :]
           + cw[1:2, :] * ext_scr[6:6 + c_len, :] + cw[0:1, :] * ext_scr[5:5 + c_len, :])
    ext_scr[0:8, :] = ext_scr[c_len:c_len + 8, :]
    return out


def _ssd_kernel(xbc_ref, z_ref, sm_ref, cw_ref, cb_ref, dtb_ref, alog_ref, d_ref, ng_ref,
                y_ref, hs_ref, cs_ref, ext_scr, h_scr, y_scr, *, c_len):
    c, nc = pl.program_id(1), pl.num_programs(1)

    @pl.when(c == 0)
    def _():
        ext_scr[0:8, :] = jnp.zeros((8, XBC_W), F32)
        h_scr[...] = jnp.zeros_like(h_scr)

    u = xbc_ref[0]
    act = _silu(_chunk_conv(u, cw_ref[...], ext_scr, c_len) + cb_ref[...])

    @pl.when(c == nc - 1)
    def _():
        cs_ref[0] = u[c_len - (CONV_W - 1):c_len, :]

    xa, bm, cm = act[:, 0:GROUP], act[:, GROUP:GROUP + 2 * HD], act[:, GROUP + 2 * HD:]
    dt = _softplus(sm_ref[0] + dtb_ref[...])
    a_neg = -jnp.exp(alog_ref[...])
    row, colm = _iota((c_len, c_len), 0), _iota((c_len, c_len), 1)
    causal = row >= colm
    gam = _mm01(_bf(causal), dt * a_neg)
    gam_t, dt_t = _tr(gam), _tr(dt)
    glast = gam[c_len - 1:c_len, :]
    eg = jnp.exp(gam)
    wst = jnp.exp(glast - gam) * dt
    elast = jnp.exp(glast)
    dvec = d_ref[...]
    for g in range(2):
        cg, bg = cm[:, g * HD:(g + 1) * HD], bm[:, g * HD:(g + 1) * HD]
        cb = _nt(cg, bg)
        for h in (2 * g, 2 * g + 1):
            sl = slice(h * HD, (h + 1) * HD)
            hl = slice(L_DT + h, L_DT + h + 1)
            dec = jnp.exp(jnp.where(causal, gam[:, hl] - gam_t[hl, :], NEG))
            xh = xa[:, sl]
            hprev = h_scr[h]
            y_scr[:, sl] = (_mm(cb * dec * dt_t[hl, :], xh) + _nt(cg, hprev) * eg[:, hl] + dvec[:, hl] * xh)
            h_scr[h] = elast[:, hl] * hprev + _tn3c(xh * wst[:, hl], bg)
    yg = y_scr[...] * _silu(z_ref[0])
    y_ref[0] = yg * lax.rsqrt(jnp.mean(yg * yg, axis=-1, keepdims=True) + EPS) * ng_ref[...]

    @pl.when(c == nc - 1)
    def _():
        hs_ref[0] = h_scr[...]


def _ssd_prompt(proj3, cw, cb, dtb, alog, dvec, ng):
    b, ln, _ = proj3.shape
    c_len = SSD_CHUNK
    vec = lambda w: pl.BlockSpec((1, w), lambda bb, c: (0, 0))
    return pl.pallas_call(
        functools.partial(_ssd_kernel, c_len=c_len), grid=(b, ln // c_len),
        in_specs=[pl.BlockSpec((1, c_len, XBC_W), lambda bb, c: (bb, c, COL_XBC)),
                  pl.BlockSpec((1, c_len, GROUP), lambda bb, c: (bb, c, COL_ZA)),
                  pl.BlockSpec((1, c_len, LANES), lambda bb, c: (bb, c, COL_SMALL)),
                  pl.BlockSpec((CONV_W, XBC_W), lambda bb, c: (0, 0)), vec(XBC_W), vec(LANES), vec(LANES), vec(LANES),
                  vec(GROUP)],
        out_specs=[pl.BlockSpec((1, c_len, GROUP), lambda bb, c: (bb, c, 0)),
                   pl.BlockSpec((1, HEADS, HD, HD), lambda bb, c: (bb, 0, 0, 0)),
                   pl.BlockSpec((1, CONV_W - 1, XBC_W), lambda bb, c: (bb, 0, 0))],
        out_shape=[jax.ShapeDtypeStruct((b, ln, GROUP), F32), jax.ShapeDtypeStruct((b, HEADS, HD, HD), F32),
                   jax.ShapeDtypeStruct((b, CONV_W - 1, XBC_W), F32)],
        scratch_shapes=[pltpu.VMEM((c_len + 8, XBC_W), F32), pltpu.VMEM((HEADS, HD, HD), F32),
                        pltpu.VMEM((c_len, GROUP), F32)],
        compiler_params=_cparams(("parallel", "arbitrary")), name="ssd_prompt",
    )(proj3, proj3, proj3, cw, cb, dtb, alog, dvec, ng)


def _gdn_kernel(qkv_ref, z_ref, sm_ref, cw_ref, alog_ref, dtb_ref, ng_ref,
                y_ref, ss_ref, cs_ref, ext_scr, s_scr, *, tok, c_len):
    c, nc = pl.program_id(1), pl.num_programs(1)

    @pl.when(c == 0)
    def _():
        ext_scr[0:8, :] = jnp.zeros((8, QKV_W), F32)
        s_scr[...] = jnp.zeros_like(s_scr)

    u = qkv_ref[0]
    act = _silu(_chunk_conv(u, cw_ref[...], ext_scr, tok))

    @pl.when(c == nc - 1)
    def _():
        cs_ref[0] = u[tok - (CONV_W - 1):tok, :]

    hm = _head_mat()
    q, k, v = act[:, 0:GROUP], act[:, GROUP:2 * GROUP], act[:, 2 * GROUP:]
    qn = q * lax.rsqrt(_headsum(q * q, hm) + EPS) * (HD ** -0.5)
    kn = k * lax.rsqrt(_headsum(k * k, hm) + EPS)
    sm = sm_ref[0]
    beta = jax.nn.sigmoid(sm)
    gl = -jnp.exp(alog_ref[...]) * _softplus(sm + dtb_ref[...])
    shift = c_len.bit_length() - 1
    ri, ci = _iota((tok, tok), 0), _iota((tok, tok), 1)
    same_chunk = lax.shift_right_logical(ri, shift) == lax.shift_right_logical(ci, shift)
    gam = _mm01(_bf(jnp.logical_and(ri >= ci, same_chunk)), gl)
    gam_t = _tr(gam)
    eg = jnp.exp(gam)
    row, colm = _iota((c_len, c_len), 0), _iota((c_len, c_len), 1)
    incl = row >= colm
    strict = row > colm
    eye = (row == colm).astype(F32)
    z, ng = z_ref[0], ng_ref[...]
    units = [(i, h) for i in range(tok // c_len) for h in range(HEADS)]
    n_fac = c_len.bit_length() - 2

    dec, kh, qh, bcol, pinv, mpow = {}, {}, {}, {}, {}, {}
    for un in units:
        i, h = un
        r = slice(i * c_len, (i + 1) * c_len)
        g = L_GA + h
        dec[un] = jnp.exp(jnp.where(incl, gam[r, g:g + 1] - gam_t[g:g + 1, r], NEG))
        kh[un], qh[un] = kn[r, h * HD:(h + 1) * HD], qn[r, h * HD:(h + 1) * HD]
        bcol[un] = beta[r, L_BETA + h:L_BETA + h + 1]
        nmat = bcol[un] * _nt3c(kh[un], kh[un]) * jnp.where(strict, dec[un], 0.0)
        pinv[un] = eye - nmat
        mpow[un] = _mm3c(nmat, nmat)
    for lvl in range(n_fac):
        for un in units:
            if lvl + 1 < n_fac:
                both = _mm3c(jnp.concatenate([pinv[un], mpow[un]], axis=0), mpow[un])
                pinv[un], mpow[un] = pinv[un] + both[0:c_len], both[c_len:]
            else:
                pinv[un] = pinv[un] + _mm3c(pinv[un], mpow[un])
    uu, wk, qkd = {}, {}, {}
    for un in units:
        i, h = un
        r = slice(i * c_len, (i + 1) * c_len)
        g = L_GA + h
        rhs = jnp.concatenate([bcol[un] * v[r, h * HD:(h + 1) * HD], (bcol[un] * eg[r, g:g + 1]) * kh[un]], axis=1)
        sol = _mm3c(pinv[un], rhs)
        uu[un], wk[un] = sol[:, 0:HD], sol[:, HD:]
        qkd[un] = _nt(qh[un], kh[un]) * dec[un]

    for h in range(HEADS):
        sl = slice(h * HD, (h + 1) * HD)
        g = L_GA + h
        s_run = s_scr[h]
        for i in range(tok // c_len):
            un = (i, h)
            r = slice(i * c_len, (i + 1) * c_len)
            glast = gam[(i + 1) * c_len - 1:(i + 1) * c_len, g:g + 1]
            on_state = _mm3c(jnp.concatenate([wk[un], qh[un] * eg[r, g:g + 1]], axis=0), s_run)
            w = uu[un] - on_state[0:c_len]
            o = on_state[c_len:] + _mm(qkd[un], w)
            s_run = jnp.exp(glast) * s_run + _tn3c(kh[un] * jnp.exp(glast - gam[r, g:g + 1]), w)
            on = o * lax.rsqrt(jnp.mean(o * o, axis=-1, keepdims=True) + EPS) * ng[:, sl]
            y_ref[0, r, sl] = on * _silu(z[r, sl])
        s_scr[h] = s_run

    @pl.when(c == nc - 1)
    def _():
        ss_ref[0] = s_scr[...]


def _gdn_prompt(proj3, cw, alog, dtb, ng):
    b, ln, _ = proj3.shape
    tok = GDN_TOK
    vec = lambda w: pl.BlockSpec((1, w), lambda bb, c: (0, 0))
    return pl.pallas_call(
        functools.partial(_gdn_kernel, tok=tok, c_len=GDN_CHUNK), grid=(b, ln // tok),
        in_specs=[pl.BlockSpec((1, tok, QKV_W), lambda bb, c: (bb, c, COL_QKV)),
                  pl.BlockSpec((1, tok, GROUP), lambda bb, c: (bb, c, COL_ZC)),
                  pl.BlockSpec((1, tok, LANES), lambda bb, c: (bb, c, COL_SMALL)),
                  pl.BlockSpec((CONV_W, QKV_W), lambda bb, c: (0, 0)), vec(LANES), vec(LANES), vec(GROUP)],
        out_specs=[pl.BlockSpec((1, tok, GROUP), lambda bb, c: (bb, c, 0)),
                   pl.BlockSpec((1, HEADS, HD, HD), lambda bb, c: (bb, 0, 0, 0)),
                   pl.BlockSpec((1, CONV_W - 1, QKV_W), lambda bb, c: (bb, 0, 0))],
        out_shape=[jax.ShapeDtypeStruct((b, ln, GROUP), F32), jax.ShapeDtypeStruct((b, HEADS, HD, HD), F32),
                   jax.ShapeDtypeStruct((b, CONV_W - 1, QKV_W), F32)],
        scratch_shapes=[pltpu.VMEM((tok + 8, QKV_W), F32), pltpu.VMEM((HEADS, HD, HD), F32)],
        compiler_params=_cparams(("parallel", "arbitrary")), name="gdn_prompt",
    )(proj3, proj3, proj3, cw, alog, dtb, ng)


def _sprep_kernel(fq_ref, fk_ref, fv_ref, mq_ref, mk_ref, mv_ref, xbc_ref, qkv_ref, za_ref, zc_ref, sm_ref,
                  sbuf_ref, gbuf_ref, gfq_ref, gfk_ref, gmq_ref, gmk_ref, bf_ref, scw_ref, scb_ref, gcw_ref,
                  fqn_ref, fkn_ref, mqn_ref, mkn_ref, lf_ref, fkt_ref, fvt_ref, mkt_ref, mvt_ref, lft_ref,
                  xbct_ref, zat_ref, smt_ref, sbo_ref, qkvt_ref, zct_ref, gbo_ref):
    hm = _head_mat()

    def hnorm(x, g):
        return x * lax.rsqrt(_headsum(x * x, hm) * (1.0 / HD) + EPS) * g

    fqn_ref[...] = hnorm(fq_ref[...], gfq_ref[...])
    fkn = hnorm(fk_ref[...], gfk_ref[...])
    fkn_ref[...] = fkn
    fkt_ref[...] = _tr(fkn)
    fvt_ref[...] = _tr(fv_ref[...])
    mqn_ref[...] = hnorm(mq_ref[...], gmq_ref[...])
    mkn = hnorm(mk_ref[...], gmk_ref[...])
    mkn_ref[...] = mkn
    mkt_ref[...] = _tr(mkn)
    mvt_ref[...] = _tr(mv_ref[...])
    sm = sm_ref[...]
    lf = _log_sigmoid(sm + bf_ref[...])
    lf_ref[...] = lf[:, L_FF:L_FF + HEADS]
    lft_ref[...] = _tr(lf)[L_FF:L_FF + HEADS, :]
    smt_ref[...] = _tr(sm)
    zat_ref[...] = _tr(za_ref[...])
    zct_ref[...] = _tr(zc_ref[...])

    def conv(u, buf_ref, cw):
        return cw[3:4, :] * u + cw[2:3, :] * buf_ref[2] + cw[1:2, :] * buf_ref[1] + cw[0:1, :] * buf_ref[0]

    def roll_buf(out_ref, buf_ref, u):
        out_ref[0] = buf_ref[1]
        out_ref[1] = buf_ref[2]
        out_ref[2] = u

    u = xbc_ref[...]
    xbct_ref[...] = _tr(_silu(conv(u, sbuf_ref, scw_ref[...]) + scb_ref[...]))
    roll_buf(sbo_ref, sbuf_ref, u)
    ug = qkv_ref[...]
    act = _silu(conv(ug, gbuf_ref, gcw_ref[...]))
    roll_buf(gbo_ref, gbuf_ref, ug)
    q, k = act[:, 0:GROUP], act[:, GROUP:2 * GROUP]
    qkvt_ref[0:GROUP, :] = _tr(q * lax.rsqrt(_headsum(q * q, hm) + EPS) * (HD ** -0.5))
    qkvt_ref[GROUP:2 * GROUP, :] = _tr(k * lax.rsqrt(_headsum(k * k, hm) + EPS))
    qkvt_ref[2 * GROUP:, :] = _tr(act[:, 2 * GROUP:])


def _sprep(proj_s, sbuf, gbuf, gfq, gfk, gmq, gmk, bf_full, scw, scb, gcw):
    bd = proj_s.shape[0]
    col = lambda c, w=GROUP: pl.BlockSpec((bd, w), lambda i: (0, c))
    full = lambda a: pl.BlockSpec(a.shape, lambda i: (0,) * a.ndim)
    sds = lambda *s: jax.ShapeDtypeStruct(s, F32)
    out_shape = [sds(bd, GROUP)] * 4 + [sds(bd, HEADS)] + [sds(GROUP, bd)] * 4 + [sds(HEADS, bd),
                 sds(XBC_W, bd), sds(GROUP, bd), sds(LANES, bd), sds(CONV_W - 1, bd, XBC_W),
                 sds(QKV_W, bd), sds(GROUP, bd), sds(CONV_W - 1, bd, QKV_W)]
    return pl.pallas_call(
        _sprep_kernel, grid=(1,),
        in_specs=[col(COL_FQ), col(COL_FK), col(COL_FV), col(COL_MQ), col(COL_MK), col(COL_MV),
                  col(COL_XBC, XBC_W), col(COL_QKV, QKV_W), col(COL_ZA), col(COL_ZC), col(COL_SMALL, LANES),
                  full(sbuf), full(gbuf), full(gfq), full(gfk), full(gmq), full(gmk), full(bf_full),
                  full(scw), full(scb), full(gcw)],
        out_specs=[pl.BlockSpec(s.shape, lambda i, n=len(s.shape): (0,) * n) for s in out_shape],
        out_shape=out_shape,
        compiler_params=_cparams(("arbitrary",)), name="sample_prep",
    )(proj_s, proj_s, proj_s, proj_s, proj_s, proj_s, proj_s, proj_s, proj_s, proj_s, proj_s,
      sbuf, gbuf, gfq, gfk, gmq, gmk, bf_full, scw, scb, gcw)


def _ssd_step_kernel(x_ref, b_ref, c_ref, z_ref, dtr_ref, dtb_ref, alog_ref, d_ref, st_ref, y_ref, so_ref, y_scr):
    dt = _softplus(dtr_ref[0] + dtb_ref[0])
    dec = jnp.exp(dt * (-jnp.exp(alog_ref[0])))
    xt, bt, ct = x_ref[...], b_ref[...], c_ref[...]
    xdt = xt * dt
    for p in range(HD):
        s_new = dec * st_ref[0, 0, p] + xdt[p:p + 1, :] * bt
        so_ref[0, p] = s_new
        y_scr[p:p + 1, :] = jnp.sum(s_new * ct, axis=0, keepdims=True)
    y_ref[...] = (y_scr[...] + d_ref[0] * xt) * _silu(z_ref[...])


def _ssd_step(layer, xbct, zat, smt3, state_t, dtb, alog, dvec):
    bd = xbct.shape[1]
    tile = lambda f: pl.BlockSpec((HD, bd), f)
    par = pl.BlockSpec((1, 1, bd), lambda h: (h, 0, 0))
    return pl.pallas_call(
        _ssd_step_kernel, grid=(HEADS,),
        in_specs=[tile(lambda h: (h, 0)), tile(lambda h: (HEADS + h // 2, 0)), tile(lambda h: (HEADS + 2 + h // 2, 0)),
                  tile(lambda h: (h, 0)), pl.BlockSpec((1, 1, bd), lambda h: (L_DT + h, 0, 0)), par, par, par,
                  pl.BlockSpec((1, 1, HD, HD, bd), lambda h: (layer, h, 0, 0, 0))],
        out_specs=[tile(lambda h: (h, 0)), pl.BlockSpec((1, HD, HD, bd), lambda h: (h, 0, 0, 0))],
        out_shape=[jax.ShapeDtypeStruct((GROUP, bd), F32), jax.ShapeDtypeStruct((HEADS, HD, HD, bd), F32)],
        scratch_shapes=[pltpu.VMEM((HD, bd), F32)],
        compiler_params=_cparams(("parallel",)), name="ssd_step",
    )(xbct, xbct, xbct, zat, smt3, dtb, alog, dvec, state_t)


def _gdn_step_kernel(q_ref, k_ref, v_ref, z_ref, br_ref, ar_ref, alog_ref, dtb_ref, ng_ref, st_ref, y_ref, so_ref):
    eg = jnp.exp(-jnp.exp(alog_ref[0]) * _softplus(ar_ref[0] + dtb_ref[0]))
    beta = jax.nn.sigmoid(br_ref[0])
    qt, kt, vt = q_ref[...], k_ref[...], v_ref[...]
    ks = jnp.zeros_like(vt)
    qs = jnp.zeros_like(vt)
    for dk in range(HD):
        s_old = st_ref[0, 0, dk]
        ks = ks + kt[dk:dk + 1, :] * s_old
        qs = qs + qt[dk:dk + 1, :] * s_old
    w = beta * (vt - eg * ks)
    o = eg * qs + jnp.sum(qt * kt, axis=0, keepdims=True) * w
    for dk in range(HD):
        so_ref[0, dk] = eg * st_ref[0, 0, dk] + kt[dk:dk + 1, :] * w
    on = o * lax.rsqrt(jnp.mean(o * o, axis=0, keepdims=True) + EPS) * ng_ref[...]
    y_ref[...] = on * _silu(z_ref[...])


def _gdn_step(layer, qkvt, zct, smt3, state_t, alog, dtb, ng_b):
    bd = qkvt.shape[1]
    tile = lambda f: pl.BlockSpec((HD, bd), f)
    par = pl.BlockSpec((1, 1, bd), lambda h: (h, 0, 0))
    return pl.pallas_call(
        _gdn_step_kernel, grid=(HEADS,),
        in_specs=[tile(lambda h: (h, 0)), tile(lambda h: (HEADS + h, 0)), tile(lambda h: (2 * HEADS + h, 0)),
                  tile(lambda h: (h, 0)), pl.BlockSpec((1, 1, bd), lambda h: (L_BETA + h, 0, 0)),
                  pl.BlockSpec((1, 1, bd), lambda h: (L_GA + h, 0, 0)), par, par, tile(lambda h: (0, 0)),
                  pl.BlockSpec((1, 1, HD, HD, bd), lambda h: (layer, h, 0, 0, 0))],
        out_specs=[tile(lambda h: (h, 0)), pl.BlockSpec((1, HD, HD, bd), lambda h: (h, 0, 0, 0))],
        out_shape=[jax.ShapeDtypeStruct((GROUP, bd), F32), jax.ShapeDtypeStruct((HEADS, HD, HD, bd), F32)],
        compiler_params=_cparams(("parallel",)), name="gdn_step",
    )(qkvt, qkvt, qkvt, zct, smt3, smt3, alog, dtb, ng_b, state_t)


def _outproj_kernel(ya_ref, yb_ref, yc_ref, yd_ref, w_ref, x_ref, gt_ref, o_ref):
    acc = _dot(_bf(ya_ref[...]), w_ref[0:GROUP, :])
    acc += _dot(_bf(yb_ref[...]), w_ref[GROUP:2 * GROUP, :])
    acc += _dot(_bf(yc_ref[...]), w_ref[2 * GROUP:3 * GROUP, :])
    acc += _dot(_bf(yd_ref[...]), w_ref[3 * GROUP:4 * GROUP, :])
    o_ref[...] = x_ref[...] + gt_ref[0] * acc


def _out_proj(ya, yb, yc, yd, w_bf, x2d, mod, *, tm, per_row, rows_per_batch):
    t, d = x2d.shape
    yspec = pl.BlockSpec((tm, GROUP), lambda i: (i, 0))
    return pl.pallas_call(
        _outproj_kernel, grid=(t // tm,),
        in_specs=[yspec, yspec, yspec, yspec, pl.BlockSpec((d, d), lambda i: (0, 0)),
                  pl.BlockSpec((tm, d), lambda i: (i, 0)), _mod_spec(per_row, tm, rows_per_batch, 2)],
        out_specs=pl.BlockSpec((tm, d), lambda i: (i, 0)),
        out_shape=jax.ShapeDtypeStruct((t, d), F32),
        compiler_params=_cparams(("parallel",)), name="out_proj",
    )(ya, yb, yc, yd, w_bf, x2d, mod)


def _outproj_s_kernel(yat_ref, ng_ref, yb_ref, yct_ref, yd_ref, w_ref, x_ref, gt_ref, o_ref):
    ya = yat_ref[...]
    ya = ya * lax.rsqrt(jnp.mean(ya * ya, axis=0, keepdims=True) + EPS) * ng_ref[...]
    acc = _tn3(ya, w_ref[0:GROUP, :])
    acc += _mm3(yb_ref[...], w_ref[GROUP:2 * GROUP, :])
    acc += _tn3(yct_ref[...], w_ref[2 * GROUP:3 * GROUP, :])
    acc += _mm3(yd_ref[...], w_ref[3 * GROUP:4 * GROUP, :])
    o_ref[...] = x_ref[...] + gt_ref[0] * acc


def _out_proj_s(yat, ng_b, yb, yct, yd, w_bf, x2d, mod):
    bd, d = x2d.shape
    full = lambda a: pl.BlockSpec(a.shape, lambda i: (0,) * a.ndim)
    return pl.pallas_call(
        _outproj_s_kernel, grid=(1,),
        in_specs=[full(yat), full(ng_b), full(yb), full(yct), full(yd), full(w_bf), full(x2d),
                  _mod_spec(True, bd, 1, 2)],
        out_specs=pl.BlockSpec((bd, d), lambda i: (0, 0)),
        out_shape=jax.ShapeDtypeStruct((bd, d), F32),
        compiler_params=_cparams(("arbitrary",)), name="out_proj_sample",
    )(yat, ng_b, yb, yct, yd, w_bf, x2d, mod)


def _ffn_kernel(x_ref, g_ref, sc_ref, sh_ref, gt_ref, wg_ref, wu_ref, wd_ref, o_ref, h_scr, acc_scr, *, precise):
    f, nf = pl.program_id(1), pl.num_programs(1)
    mm = _mm3 if precise else (lambda a, b: _dot(_bf(a), b))

    @pl.when(f == 0)
    def _():
        h_scr[...] = _normmod(x_ref[...], g_ref[...], sc_ref[0], sh_ref[0]).astype(h_scr.dtype)
        acc_scr[...] = jnp.zeros_like(acc_scr)

    h = h_scr[...]
    a = _silu(mm(h, wg_ref[...])) * mm(h, wu_ref[...])
    acc_scr[...] += mm(a, wd_ref[...])

    @pl.when(f == nf - 1)
    def _():
        o_ref[...] = x_ref[...] + gt_ref[0] * acc_scr[...]


def _ffn_dense(x2d, g, mod, wg, wu, wd, *, tm, tf, per_row, rows_per_batch, precise=False):
    t, d = x2d.shape
    ff = wg.shape[1]
    ms = functools.partial(_mod_spec, per_row, tm, rows_per_batch)
    return pl.pallas_call(
        functools.partial(_ffn_kernel, precise=precise), grid=(t // tm, ff // tf),
        in_specs=[pl.BlockSpec((tm, d), lambda i, f: (i, 0)), pl.BlockSpec((1, d), lambda i, f: (0, 0)),
                  ms(4), ms(3), ms(5),
                  pl.BlockSpec((d, tf), lambda i, f: (0, f)), pl.BlockSpec((d, tf), lambda i, f: (0, f)),
                  pl.BlockSpec((tf, d), lambda i, f: (f, 0))],
        out_specs=pl.BlockSpec((tm, d), lambda i, f: (i, 0)),
        out_shape=jax.ShapeDtypeStruct((t, d), F32),
        scratch_shapes=[pltpu.VMEM((tm, d), F32 if precise else BF16), pltpu.VMEM((tm, d), F32)],
        compiler_params=_cparams(("parallel", "arbitrary")), name="ffn_dense",
    )(x2d, g, mod, mod, mod, wg, wu, wd)


def _moe_kernel(x_ref, g_ref, sc_ref, sh_ref, gt_ref, r_ref, wg_ref, wu_ref, wd_ref, o_ref,
                h_scr, acc_scr, gate_scr, gatet_scr, rank_scr, rankt_scr, count_scr, *, n_experts, cap):
    e, f = pl.program_id(1), pl.program_id(2)
    ne, nf = pl.num_programs(1), pl.num_programs(2)
    tm = x_ref.shape[0]
    lane = _iota((tm, LANES), 1)

    @pl.when(jnp.logical_and(e == 0, f == 0))
    def _():
        hf = _normmod(x_ref[...], g_ref[...], sc_ref[0], sh_ref[0])
        h_scr[...] = _bf(hf)
        acc_scr[...] = jnp.zeros_like(acc_scr)
        lane_f = lane.astype(F32)
        logits = jnp.where(lane < n_experts, _mm3(hf, r_ref[...]), NEG)
        v1 = jnp.max(logits, axis=1, keepdims=True)
        i1 = jnp.min(jnp.where(logits == v1, lane_f, float(LANES)), axis=1, keepdims=True)
        rest = jnp.where(lane_f == i1, NEG, logits)
        v2 = jnp.max(rest, axis=1, keepdims=True)
        i2 = jnp.min(jnp.where(rest == v2, lane_f, float(LANES)), axis=1, keepdims=True)
        e2 = jnp.exp(v2 - v1)
        w1 = 1.0 / (1.0 + e2)
        gates = jnp.where(lane_f == i1, w1, jnp.where(lane_f == i2, e2 * w1, 0.0))
        gate_scr[...] = gates
        gates_t = _tr(gates)
        gatet_scr[...] = gates_t
        on, on_t = _bf(jnp.where(gates != 0.0, 1.0, 0.0)), _bf(jnp.where(gates_t != 0.0, 1.0, 0.0))
        ri, ci = _iota((tm, tm), 0), _iota((tm, tm), 1)
        rank_scr[...] = _dot(_bf(ri > ci), on)
        rankt_scr[...] = _dot(on_t, _bf(ri < ci))
        totals = jnp.sum(on.astype(F32), axis=0, keepdims=True)
        for k in range(n_experts):
            count_scr[k] = totals[0, k].astype(jnp.int32)

    h = h_scr[...]
    ge = jnp.sum(jnp.where(lane == e, gate_scr[...], 0.0), axis=1, keepdims=True)
    ge_row = gatet_scr[pl.ds(e, 1), :]
    on_col = jnp.where(ge != 0.0, 1.0, 0.0)
    on_row = jnp.where(ge_row != 0.0, 1.0, 0.0)
    count = count_scr[e]

    @pl.when(count <= cap)
    def _():
        rank_col = jnp.sum(jnp.where(lane == e, rank_scr[...], 0.0), axis=1, keepdims=True)
        rank_row = rankt_scr[pl.ds(e, 1), :]
        slot_r = _iota((cap, tm), 0).astype(F32)
        pick = _bf(jnp.where(jnp.logical_and(rank_row == slot_r, on_row > 0.5), 1.0, 0.0))
        slot_c = _iota((tm, cap), 1).astype(F32)
        spread = _bf(jnp.where(jnp.logical_and(rank_col == slot_c, on_col > 0.5), 1.0, 0.0))
        hc = _bf(_dot(pick, h))
        gc = _mm01(pick, jnp.broadcast_to(ge, (tm, LANES)))[:, 0:1]
        a = _silu(_dot(hc, wg_ref[0])) * _dot(hc, wu_ref[0])
        yh, yl = _hi_lo(_dot(_bf(a), wd_ref[0]) * gc)
        acc_scr[...] += _dot(spread, yh) + _dot(spread, yl)

    @pl.when(count > cap)
    def _():
        a = _silu(_dot(h, wg_ref[0])) * _dot(h, wu_ref[0]) * ge
        acc_scr[...] += _dot(_bf(a), wd_ref[0])

    @pl.when(jnp.logical_and(e == ne - 1, f == nf - 1))
    def _():
        o_ref[...] = x_ref[...] + gt_ref[0] * acc_scr[...]


def _ffn_moe(x2d, g, mod, router_pad, wg, wu, wd, *, tm, tf, per_row, rows_per_batch):
    t, d = x2d.shape
    ne, _, ff = wg.shape
    ms = functools.partial(_mod_spec, per_row, tm, rows_per_batch)
    return pl.pallas_call(
        functools.partial(_moe_kernel, n_experts=ne, cap=-(-tm * 5 // 16 // 16) * 16), grid=(t // tm, ne, ff // tf),
        in_specs=[pl.BlockSpec((tm, d), lambda i, e, f: (i, 0)), pl.BlockSpec((1, d), lambda i, e, f: (0, 0)),
                  ms(4), ms(3), ms(5), pl.BlockSpec((d, LANES), lambda i, e, f: (0, 0)),
                  pl.BlockSpec((1, d, tf), lambda i, e, f: (e, 0, f)), pl.BlockSpec((1, d, tf), lambda i, e, f: (e, 0, f)),
                  pl.BlockSpec((1, tf, d), lambda i, e, f: (e, f, 0))],
        out_specs=pl.BlockSpec((tm, d), lambda i, e, f: (i, 0)),
        out_shape=jax.ShapeDtypeStruct((t, d), F32),
        scratch_shapes=[pltpu.VMEM((tm, d), BF16), pltpu.VMEM((tm, d), F32), pltpu.VMEM((tm, LANES), F32),
                        pltpu.VMEM((LANES, tm), F32), pltpu.VMEM((tm, LANES), F32), pltpu.VMEM((LANES, tm), F32),
                        pltpu.SMEM((ne,), jnp.int32)],
        compiler_params=_cparams(("parallel", "arbitrary", "arbitrary"), vmem_mb=56), name="ffn_moe",
    )(x2d, g, mod, mod, mod, router_pad, wg, wu, wd)


def _lane_row(vals, offset):
    return jnp.zeros((1, LANES), F32).at[0, offset:offset + vals.shape[0]].set(vals.astype(F32))


def _reorder_w_in(w):
    pad = jnp.zeros((w.shape[0], LANES - 16), w.dtype)
    return jnp.concatenate([w[:, 256:768], w[:, 0:256], w[:, 772:1540], w[:, 1544:2312], w[:, 2320:2576],
                            w[:, 2576:3344], w[:, 768:772], w[:, 1540:1544], w[:, 2312:2320], pad], axis=1)


def _tile_heads(g):
    return jnp.tile(g.astype(F32), HEADS).reshape(1, GROUP)


def _per_head_rows(vals, bd):
    return jnp.broadcast_to(vals.astype(F32).reshape(HEADS, 1, 1), (HEADS, 1, bd))


def kernel(x_prompt, x_sample, cache_fox_k, cache_fox_v, cache_fox_logf, cache_moba_k, cache_moba_v, state_ssm, state_ssm_conv, state_gdn, state_gdn_conv, page_table, c_prompt, c_sample, w_ada, b_ada, norm_mix, norm_ffn, w_in, w_out, ssd_conv_w, ssd_conv_b, ssd_dt_bias, ssd_a_log, ssd_d, ssd_norm, fox_b_f, fox_q_norm, fox_k_norm, gdn_conv_w, gdn_a_log, gdn_dt_bias, gdn_norm, moba_q_norm, moba_k_norm, ffn_w_gate, ffn_w_up, ffn_w_down, moe_router, moe_w_gate, moe_w_up, moe_w_down):
    bp, ln, d = x_prompt.shape
    bd = x_sample.shape[0]
    depth = w_in.shape[0]
    n_pages = page_table.shape[1]
    n_pool = cache_fox_k.shape[1]
    assert x_sample.shape[1] == 1 and d == D_MODEL and ln % MOBA_BLOCK == 0
    tp = bp * ln
    tm_p = 512 if ln % 512 == 0 else 256
    t_attn = 256

    mod = _ada_mod(jnp.concatenate([c_prompt, c_sample], axis=0), w_ada, b_ada)
    pt_flat = page_table.reshape(-1).astype(jnp.int32)
    page_t = lambda c: jnp.transpose(c, (0, 1, 3, 4, 2)).reshape(depth * n_pool, GROUP, PAGE)
    fox_kc, fox_vc, moba_kc, moba_vc = page_t(cache_fox_k), page_t(cache_fox_v), page_t(cache_moba_k), page_t(cache_moba_v)
    fox_lfc = jnp.transpose(cache_fox_logf, (0, 1, 3, 2)).reshape(depth, n_pool, 1, HEADS * PAGE)
    ssm_t = jnp.transpose(state_ssm, (0, 2, 3, 4, 1))
    gdn_t = jnp.transpose(state_gdn, (0, 2, 3, 4, 1))
    xp = x_prompt.reshape(tp, d)
    xs = x_sample.reshape(bd, d)
    outs_p, outs_s = [], []
    for l in range(depth):
        mod_p = mod[l, :bp].reshape(bp, 1, 6 * d)
        mod_s = mod[l, bp:].reshape(1, bd, 6 * d)
        kw_p = dict(tm=tm_p, per_row=False, rows_per_batch=ln)
        kw_s = dict(tm=bd, per_row=True, rows_per_batch=1)
        g_mix = norm_mix[l].reshape(1, d)
        g_ffn = norm_ffn[l].reshape(1, d)
        w_in_f = _reorder_w_in(w_in[l])
        w_in_bf = _bf(w_in_f)
        w_out_bf = _bf(w_out[l])
        gfq, gfk = _tile_heads(fox_q_norm[l]), _tile_heads(fox_k_norm[l])
        gmq, gmk = _tile_heads(moba_q_norm[l]), _tile_heads(moba_k_norm[l])
        bf_full = _lane_row(fox_b_f[l], L_FF)
        scw, scb, gcw = ssd_conv_w[l], ssd_conv_b[l].reshape(1, XBC_W), gdn_conv_w[l]

        proj = _in_proj(xp, g_mix, mod_p, w_in_bf, tn=PROJ_COLS, **kw_p).reshape(bp, ln, PROJ_COLS)
        fqa, fkt, fkx, fvt, mqa, mkt, mvt, lft, kmean = _prep(proj, gfq, gfk, gmq, gmk, bf_full, tm=MOBA_BLOCK)
        y_a, ssm_h, ssm_buf = _ssd_prompt(proj, scw, scb, _lane_row(ssd_dt_bias[l], L_DT), _lane_row(ssd_a_log[l], L_DT),
                                          _lane_row(ssd_d[l], L_DT), ssd_norm[l].reshape(1, GROUP))
        y_c, gdn_s, gdn_buf = _gdn_prompt(proj, gcw, _lane_row(gdn_a_log[l], L_GA), _lane_row(gdn_dt_bias[l], L_GA),
                                          _tile_heads(gdn_norm[l]))
        y_b = _fox_prompt(fqa, fkt, fkx, fvt, t=t_attn, wide=2 if ln % (2 * t_attn) == 0 else 1)
        nblk = ln // MOBA_BLOCK
        kmean_tiles = jnp.pad(kmean.reshape(bp, nblk, HEADS, HD), ((0, 0), (0, LANES - nblk), (0, 0), (0, LANES - HD)))
        y_d = _moba_prompt(mqa, mkt, mvt, kmean_tiles.reshape(bp, LANES, HEADS * LANES))
        flat = lambda a: a.reshape(tp, GROUP)
        xp = _out_proj(flat(y_a), flat(y_b), flat(y_c), flat(y_d), w_out_bf, xp, mod_p, **kw_p)
        outs_p.append((fkt, fvt, lft, mkt, mvt, ssm_h, ssm_buf, gdn_s, gdn_buf))

        proj_s = _in_proj(xs, g_mix, mod_s, w_in_f, tn=PROJ_COLS // 9, precise=True, **kw_s)
        (sq, sk, smq, smk, slf, sfkt, sfvt, smkt, smvt, slft, xbct, zat, smt, sbo, qkvt, zct, gbo) = _sprep(
            proj_s, jnp.transpose(state_ssm_conv[l], (1, 0, 2)), jnp.transpose(state_gdn_conv[l], (1, 0, 2)),
            gfq, gfk, gmq, gmk, bf_full, scw, scb, gcw)
        smt3 = smt.reshape(LANES, 1, bd)
        yat, ssm_new = _ssd_step(l, xbct, zat, smt3, ssm_t, _per_head_rows(ssd_dt_bias[l], bd),
                                 _per_head_rows(ssd_a_log[l], bd), _per_head_rows(ssd_d[l], bd))
        yct, gdn_new = _gdn_step(l, qkvt, zct, smt3, gdn_t, _per_head_rows(gdn_a_log[l], bd),
                                 _per_head_rows(gdn_dt_bias[l], bd),
                                 jnp.broadcast_to(gdn_norm[l].astype(F32).reshape(HD, 1), (HD, bd)))
        rows = lambda a: a.reshape(bd, 1, a.shape[-1])
        pt_l = pt_flat + l * n_pool
        ys_b = _fox_decode(pt_l, l, rows(sq), rows(sk), rows(proj_s), rows(slf), fox_kc, fox_vc, fox_lfc[l], n_pages=n_pages)
        ys_d = _moba_decode(pt_l, l, rows(smq), rows(smk), rows(proj_s), moba_kc, moba_vc, n_pages=n_pages)
        xs = _out_proj_s(yat, jnp.broadcast_to(ssd_norm[l].astype(F32).reshape(GROUP, 1), (GROUP, bd)),
                         ys_b.reshape(bd, GROUP), yct, ys_d.reshape(bd, GROUP), w_out[l], xs, mod_s)
        outs_s.append((sfkt, sfvt, slft, smkt, smvt, ssm_new, sbo, gdn_new, gbo))

        i = l // 2
        if l % 2 == 0:
            wg, wu, wd = _bf(ffn_w_gate[i]), _bf(ffn_w_up[i]), _bf(ffn_w_down[i])
            tf = wg.shape[1] // 2
            xp = _ffn_dense(xp, g_ffn, mod_p, wg, wu, wd, tf=tf, **kw_p)
            xs = _ffn_dense(xs, g_ffn, mod_s, ffn_w_gate[i], ffn_w_up[i], ffn_w_down[i], tf=2 * LANES, precise=True, **kw_s)
        else:
            wg, wu, wd = _bf(moe_w_gate[i]), _bf(moe_w_up[i]), _bf(moe_w_down[i])
            ne = wg.shape[0]
            router_pad = jnp.pad(moe_router[i], ((0, 0), (0, LANES - ne)))
            tf = wg.shape[2]
            xp = _ffn_moe(xp, g_ffn, mod_p, router_pad, wg, wu, wd, tf=tf, **kw_p)
            xs = _ffn_moe(xs, g_ffn, mod_s, router_pad, wg, wu, wd, tf=tf, **kw_s)

    stack = lambda outs, j: jnp.stack([o[j] for o in outs], axis=0)
    kv_p = lambda j: jnp.transpose(stack(outs_p, j), (0, 1, 4, 2, 3))
    kv_s = lambda j: jnp.transpose(stack(outs_s, j).reshape(depth, HEADS, HD, bd), (0, 3, 1, 2)).reshape(depth, bd, 1, HEADS, HD)
    st_s = lambda j: jnp.transpose(stack(outs_s, j), (0, 4, 1, 2, 3))
    buf_s = lambda j: jnp.transpose(stack(outs_s, j), (0, 2, 1, 3))
    return (xp.reshape(bp, ln, d), xs.reshape(bd, 1, d),
            kv_p(0), kv_p(1), jnp.transpose(stack(outs_p, 2), (0, 1, 3, 2)), kv_p(3), kv_p(4),
            stack(outs_p, 5), stack(outs_p, 6), stack(outs_p, 7), stack(outs_p, 8),
            kv_s(0), kv_s(1), jnp.transpose(stack(outs_s, 2), (0, 2, 1)).reshape(depth, bd, 1, HEADS), kv_s(3), kv_s(4),
            st_s(5), buf_s(6), st_s(7), buf_s(8))
```
